```python
import math
import jax, jax.numpy as jnp
from jax import lax
import numpy as np

D_MODEL = 1024
BATCH = 2
SEQ = 8192
DEPTH = 2

PLE_DIM = 256
N_MIXERS = 2
HEAD_DIM = 64
BLOCK = 128
DIL_CONFIGS = ((128, 1), (512, 4), (2048, 16))
N_GROUPS_A = len(DIL_CONFIGS)
HEADS_A = D_MODEL // HEAD_DIM
HEADS_B = D_MODEL // (2 * HEAD_DIM)
D_FF = 4 * D_MODEL
N_BUCKETS = 32
MAX_DISTANCE = 2048
N_BIAS_HEADS = 16
EPS = 1e-6
N_LAYERS_A = (DEPTH + 1) // 2
N_LAYERS_B = DEPTH // 2

kernel_name = "hybrid_dilated_diffattn_trunk"


def rmsnorm(x, g):
    xf = x.astype(jnp.float32)
    y = xf * lax.rsqrt(jnp.mean(xf * xf, axis=-1, keepdims=True) + EPS)
    return (y * g.astype(jnp.float32)).astype(x.dtype)


def rel_bucket(dist):
    n = jnp.maximum(dist, 0)
    max_exact = N_BUCKETS // 2
    nf = jnp.maximum(n, 1).astype(jnp.float32)
    large = max_exact + (jnp.log(nf / max_exact) / math.log(MAX_DISTANCE / max_exact)
                         * (N_BUCKETS - max_exact)).astype(jnp.int32)
    large = jnp.minimum(large, N_BUCKETS - 1)
    return jnp.where(n < max_exact, n, large)


def dilated_group_attention(q, k, v, window, dilation, rel_table):
    B, T, H, Dh = q.shape
    span = dilation * BLOCK
    Tp = -(-T // span) * span
    L = Tp // dilation
    nb = L // BLOCK
    sub_window = window // dilation

    def to_blocks(t):
        t = jnp.pad(t, ((0, 0), (0, Tp - T), (0, 0), (0, 0)))
        t = t.reshape(B, L, dilation, H, Dh).transpose(0, 2, 1, 3, 4)
        return t.reshape(B, dilation, nb, BLOCK, H, Dh)

    def with_prev(t):
        prev = jnp.pad(t, ((0, 0), (0, 0), (1, 0), (0, 0), (0, 0), (0, 0)))[:, :, :-1]
        return jnp.concatenate([prev, t], axis=3)

    qb = to_blocks(q)
    kw = with_prev(to_blocks(k))
    vw = with_prev(to_blocks(v))

    logits = jnp.einsum('bdnqhe,bdnkhe->bdnhqk', qb, kw,
                        preferred_element_type=jnp.float32) * (Dh ** -0.5)
    qi = jnp.arange(BLOCK)[:, None]
    kj = jnp.arange(2 * BLOCK)[None, :]
    sub_dist = qi + BLOCK - kj
    band = (sub_dist >= 0) & (sub_dist <= sub_window)
    not_first = (jnp.arange(nb) > 0)[:, None, None]
    valid = band[None] & (not_first | (kj >= BLOCK)[None])
    bias = rel_table[rel_bucket(sub_dist * dilation)]
    logits = logits + bias.transpose(2, 0, 1).astype(jnp.float32)[None, None, None]
    logits = jnp.where(valid[None, None, :, None], logits, -jnp.inf)
    lse = jax.nn.logsumexp(logits, axis=-1)
    probs = jnp.exp(logits - lse[..., None])
    out = jnp.einsum('bdnhqk,bdnkhe->bdnqhe', probs.astype(v.dtype), vw)

    out = out.reshape(B, dilation, L, H, Dh).transpose(0, 2, 1, 3, 4).reshape(B, Tp, H, Dh)[:, :T]
    lse = lse.transpose(0, 1, 2, 4, 3).reshape(B, dilation, L, H)
    lse = lse.transpose(0, 2, 1, 3).reshape(B, Tp, H)[:, :T]
    return out, lse


def dilated_mixture_attention(h, w_qkv, w_o, rel_table):
    B, T, _ = h.shape
    qkv = (h @ w_qkv).reshape(B, T, N_GROUPS_A, 3, HEADS_A, HEAD_DIM)
    outs, lses = [], []
    for g, (window, dilation) in enumerate(DIL_CONFIGS):
        o, l = dilated_group_attention(qkv[:, :, g, 0], qkv[:, :, g, 1], qkv[:, :, g, 2],
                                       window, dilation, rel_table)
        outs.append(o)
        lses.append(l)
    alpha = jax.nn.softmax(jnp.stack(lses, axis=0), axis=0)
    o = jnp.sum(alpha[..., None].astype(h.dtype) * jnp.stack(outs, axis=0), axis=0)
    return o.reshape(B, T, D_MODEL) @ w_o


def differential_attention(h, w_qkv, w_o, lq1, lk1, lq2, lk2, subln_g, rel_table, lambda_init):
    B, T, _ = h.shape
    q, k, v = jnp.split(h @ w_qkv, 3, axis=-1)
    q = q.reshape(B, T, HEADS_B, 2, HEAD_DIM)
    k = k.reshape(B, T, HEADS_B, 2, HEAD_DIM)
    v = v.reshape(B, T, HEADS_B, 2 * HEAD_DIM)
    lam = (jnp.exp(jnp.sum(lq1.astype(jnp.float32) * lk1.astype(jnp.float32)))
           - jnp.exp(jnp.sum(lq2.astype(jnp.float32) * lk2.astype(jnp.float32)))
           + lambda_init)
    nq = T // BLOCK
    qb = q.reshape(B, nq, BLOCK, HEADS_B, 2, HEAD_DIM).transpose(1, 0, 2, 3, 4, 5)
    starts = jnp.arange(nq, dtype=jnp.int32) * BLOCK
    kpos = jnp.arange(T, dtype=jnp.int32)
    table = rel_table.reshape(N_BUCKETS, HEADS_B, 2)
    scale = HEAD_DIM ** -0.5

    def one_block(args):
        qblk, start = args
        dist = (start + jnp.arange(BLOCK, dtype=jnp.int32))[:, None] - kpos[None, :]
        bias = table[rel_bucket(dist)].transpose(2, 3, 0, 1).astype(jnp.float32)
        logits = jnp.einsum('bqhje,bkhje->bhjqk', qblk, k,
                            preferred_element_type=jnp.float32) * scale + bias
        logits = jnp.where(dist >= 0, logits, -jnp.inf)
        probs = jax.nn.softmax(logits, axis=-1)
        attn = probs[:, :, 0] - lam * probs[:, :, 1]
        return jnp.einsum('bhqk,bkhe->bqhe', attn.astype(v.dtype), v)

    o = lax.map(one_block, (qb, starts))
    o = o.transpose(1, 0, 2, 3, 4).reshape(B, T, HEADS_B, 2 * HEAD_DIM)
    o = rmsnorm(o, subln_g) * (1.0 - lambda_init)
    return o.reshape(B, T, D_MODEL) @ w_o


def squared_relu_mlp(h, w1, w2):
    a = jax.nn.relu(h @ w1)
    return (a * a) @ w2


def setup_inputs(seed: int = 0) -> dict:
    key = jax.random.key(seed)
    ks = jax.random.split(key, 20)
    f32 = jnp.float32

    def w(k, shape, fan_in):
        return jax.random.normal(k, shape, f32) * fan_in ** -0.5

    def gain(k, shape):
        return 1.0 + 0.02 * jax.random.normal(k, shape, f32)

    return {
        "x": jax.random.normal(ks[0], (BATCH, SEQ, D_MODEL), f32),
        "p": jax.random.normal(ks[1], (DEPTH, BATCH, SEQ, PLE_DIM), f32),
        "rel_bias": 0.5 * jax.random.normal(ks[2], (N_BUCKETS, N_BIAS_HEADS), f32),
        "a_w_qkv": w(ks[3], (N_LAYERS_A, D_MODEL, N_GROUPS_A * 3 * HEADS_A * HEAD_DIM), D_MODEL),
        "a_w_o": w(ks[4], (N_LAYERS_A, D_MODEL, D_MODEL), D_MODEL),
        "b_w_qkv": w(ks[5], (N_LAYERS_B, D_MODEL, 3 * D_MODEL), D_MODEL),
        "b_w_o": w(ks[6], (N_LAYERS_B, D_MODEL, D_MODEL), D_MODEL),
        "b_lambda_q1": 0.1 * jax.random.normal(ks[7], (N_LAYERS_B, HEAD_DIM), f32),
        "b_lambda_k1": 0.1 * jax.random.normal(ks[8], (N_LAYERS_B, HEAD_DIM), f32),
        "b_lambda_q2": 0.1 * jax.random.normal(ks[9], (N_LAYERS_B, HEAD_DIM), f32),
        "b_lambda_k2": 0.1 * jax.random.normal(ks[10], (N_LAYERS_B, HEAD_DIM), f32),
        "b_subln": gain(ks[11], (N_LAYERS_B, 2 * HEAD_DIM)),
        "norm_mix": gain(ks[12], (DEPTH, D_MODEL)),
        "norm_mlp": gain(ks[13], (DEPTH, D_MODEL)),
        "w_ff1": w(ks[14], (DEPTH, D_MODEL, D_FF), D_MODEL),
        "w_ff2": w(ks[15], (DEPTH, D_FF, D_MODEL), D_FF),
        "norm_ple": gain(ks[16], (DEPTH, D_MODEL)),
        "w_ple_gate": w(ks[17], (DEPTH, D_MODEL, D_MODEL), D_MODEL),
        "w_ple_proj": w(ks[18], (DEPTH, PLE_DIM, D_MODEL), PLE_DIM),
        "final_norm": gain(ks[19], (D_MODEL,)),
    }


def reference(x, p, rel_bias, a_w_qkv, a_w_o, b_w_qkv, b_w_o, b_lambda_q1, b_lambda_k1,
              b_lambda_q2, b_lambda_k2, b_subln, norm_mix, norm_mlp, w_ff1, w_ff2,
              norm_ple, w_ple_gate, w_ple_proj, final_norm):
    h = x
    for i in range(DEPTH):
        hn = rmsnorm(h, norm_mix[i])
        j = i // N_MIXERS
        if i % N_MIXERS == 0:
            mix = dilated_mixture_attention(hn, a_w_qkv[j], a_w_o[j], rel_bias)
        else:
            lambda_init = 0.8 - 0.6 * math.exp(-0.3 * i)
            mix = differential_attention(hn, b_w_qkv[j], b_w_o[j], b_lambda_q1[j], b_lambda_k1[j],
                                         b_lambda_q2[j], b_lambda_k2[j], b_subln[j], rel_bias,
                                         lambda_init)
        h = h + mix
        h = h + squared_relu_mlp(rmsnorm(h, norm_mlp[i]), w_ff1[i], w_ff2[i])
        gate = jax.nn.sigmoid(rmsnorm(h, norm_ple[i]) @ w_ple_gate[i])
        h = h + gate * (p[i] @ w_ple_proj[i])
    return rmsnorm(h, final_norm)
```

```python
import functools
import math

import jax
import jax.numpy as jnp
from jax import lax
from jax.experimental import pallas as pl
from jax.experimental.pallas import tpu as pltpu

F32 = jnp.float32
BF16 = jnp.bfloat16

D_MODEL = 1024
HEAD_DIM = 64
BLOCK = 128
DIL_CONFIGS = ((128, 1), (512, 4), (2048, 16))
N_GROUPS = len(DIL_CONFIGS)
N_HEADS_A = D_MODEL // HEAD_DIM
N_HEADS_B = D_MODEL // (2 * HEAD_DIM)
N_BUCKETS = 32
MAX_DISTANCE = 2048
N_BIAS_COLS = 16
EPS = 1e-6
NEG = -1e30
LANES = 128
QK_SCALE = HEAD_DIM ** -0.5

DIFF_CONST_TILE = (MAX_DISTANCE + LANES - 1) // LANES + 1
DIFF_MASK_TILE = DIFF_CONST_TILE + 1
DIFF_N_TILES = DIFF_MASK_TILE + 1

VMEM_LIMIT = 48 * 1024 * 1024


def _cparams(sem):
    return pltpu.CompilerParams(dimension_semantics=sem, vmem_limit_bytes=VMEM_LIMIT)


def _rmsnorm_f32(x, g):
    ms = jnp.mean(x * x, axis=-1, keepdims=True)
    return x * lax.rsqrt(ms + EPS) * g


def _rel_bucket(dist):
    n = jnp.maximum(dist, 0)
    max_exact = N_BUCKETS // 2
    nf = jnp.maximum(n, 1).astype(F32)
    large = max_exact + (jnp.log(nf / max_exact) / math.log(MAX_DISTANCE / max_exact)
                         * (N_BUCKETS - max_exact)).astype(jnp.int32)
    large = jnp.minimum(large, N_BUCKETS - 1)
    return jnp.where(n < max_exact, n, large)


def _table_lookup(bucket, tab_ref, col):
    acc = jnp.zeros(bucket.shape, F32)
    for b in range(N_BUCKETS):
        acc = jnp.where(bucket == b, tab_ref[b, col], acc)
    return acc


def _dilated_bias_kernel(tab_ref, o_ref):
    g = pl.program_id(0)
    first = pl.program_id(1)
    dilation = jnp.where(g == 0, DIL_CONFIGS[0][1],
                         jnp.where(g == 1, DIL_CONFIGS[1][1], DIL_CONFIGS[2][1]))
    qi = lax.broadcasted_iota(jnp.int32, (BLOCK, 2 * BLOCK), 0)
    kj = lax.broadcasted_iota(jnp.int32, (BLOCK, 2 * BLOCK), 1)
    sub = qi + BLOCK - kj
    valid = (sub >= 0) & (sub <= BLOCK) & ((first == 0) | (kj >= BLOCK))
    bucket = _rel_bucket(sub * dilation)
    for c in range(N_BIAS_COLS):
        o_ref[c] = jnp.where(valid, _table_lookup(bucket, tab_ref, c), NEG)


def _build_dilated_bias(rel_bias):
    return pl.pallas_call(
        _dilated_bias_kernel,
        grid=(N_GROUPS, 2),
        in_specs=[pl.BlockSpec(memory_space=pltpu.SMEM)],
        out_specs=pl.BlockSpec((None, None, N_BIAS_COLS, BLOCK, 2 * BLOCK),
                               lambda g, f: (g, f, 0, 0, 0)),
        out_shape=jax.ShapeDtypeStruct((N_GROUPS, 2, N_BIAS_COLS, BLOCK, 2 * BLOCK), F32),
        compiler_params=_cparams(("arbitrary", "arbitrary")),
        name="dilated_bias",
    )(rel_bias)


def _diff_bias_kernel(tab_ref, o_ref):
    t = pl.program_id(0)
    i = lax.broadcasted_iota(jnp.int32, (LANES, LANES), 0)
    j = lax.broadcasted_iota(jnp.int32, (LANES, LANES), 1)
    dist = t * LANES + i - j
    masked = (dist < 0) | (t == DIFF_MASK_TILE)
    bucket = _rel_bucket(dist)
    for c in range(N_BIAS_COLS):
        o_ref[c] = jnp.where(masked, NEG, _table_lookup(bucket, tab_ref, c))


def _build_diff_bias(rel_bias):
    return pl.pallas_call(
        _diff_bias_kernel,
        grid=(DIFF_N_TILES,),
        in_specs=[pl.BlockSpec(memory_space=pltpu.SMEM)],
        out_specs=pl.BlockSpec((None, N_BIAS_COLS, LANES, LANES), lambda t: (t, 0, 0, 0)),
        out_shape=jax.ShapeDtypeStruct((DIFF_N_TILES, N_BIAS_COLS, LANES, LANES), F32),
        compiler_params=_cparams(("arbitrary",)),
        name="diff_bias",
    )(rel_bias)


def _norm_matmul_kernel(x_ref, g_ref, w_ref, o_ref, xn_ref):
    @pl.when(pl.program_id(1) == 0)
    def _():
        xn_ref[...] = _rmsnorm_f32(x_ref[...], g_ref[...]).astype(BF16)

    o_ref[...] = jnp.dot(xn_ref[...], w_ref[...],
                         preferred_element_type=F32).astype(o_ref.dtype)


def _norm_matmul(x, g, w, *, tm=1024, tn=1024):
    n, d = x.shape
    nout = w.shape[1]
    return pl.pallas_call(
        _norm_matmul_kernel,
        grid=(n // tm, nout // tn),
        in_specs=[pl.BlockSpec((tm, d), lambda i, j: (i, 0)),
                  pl.BlockSpec((1, d), lambda i, j: (0, 0)),
                  pl.BlockSpec((d, tn), lambda i, j: (0, j))],
        out_specs=pl.BlockSpec((tm, tn), lambda i, j: (i, j)),
        out_shape=jax.ShapeDtypeStruct((n, nout), BF16),
        scratch_shapes=[pltpu.VMEM((tm, d), BF16)],
        compiler_params=_cparams(("parallel", "arbitrary")),
        name="norm_matmul",
    )(x, g.reshape(1, d), w)


def _dilated_kernel(q_ref, kp_ref, kc_ref, vp_ref, vc_ref, bias_ref, o_ref, lse_ref):
    first = (pl.program_id(2) == 0).astype(jnp.int32)
    lane = lax.broadcasted_iota(jnp.int32, (BLOCK, LANES), 1)
    lo = lane < HEAD_DIM
    lse_tile = jnp.zeros((BLOCK, LANES), F32)
    nt = (((1,), (1,)), ((), ()))
    for hp in range(N_HEADS_A // 2):
        sl = slice(hp * LANES, (hp + 1) * LANES)
        qf = q_ref[:, sl].astype(F32) * QK_SCALE
        k = jnp.concatenate([kp_ref[:, sl], kc_ref[:, sl]], axis=0)
        v = jnp.concatenate([vp_ref[:, sl], vc_ref[:, sl]], axis=0)
        outs = []
        for j in range(2):
            h = 2 * hp + j
            qm = jnp.where(lo if j == 0 else ~lo, qf, 0.0).astype(BF16)
            s = lax.dot_general(qm, k, nt, preferred_element_type=F32)
            s = s + bias_ref[first, h]
            m = jnp.max(s, axis=-1, keepdims=True)
            p = jnp.exp(s - m)
            l = jnp.sum(p, axis=-1, keepdims=True)
            pv = jnp.dot(p.astype(BF16), v, preferred_element_type=F32)
            outs.append(pv / l)
            lse_tile = jnp.where(lane == h, m + jnp.log(l), lse_tile)
        o_ref[:, sl] = jnp.where(lo, outs[0], outs[1]).astype(o_ref.dtype)
    lse_ref[...] = lse_tile


def _dilated_attention(qkv, bias, group, dilation, batch, seq):
    assert seq % (dilation * BLOCK) == 0
    sub_len = seq // dilation
    nb = sub_len // BLOCK
    ncol = qkv.shape[1] // D_MODEL
    qkv_v = qkv.reshape(batch, sub_len, dilation * qkv.shape[1])

    def col(c):
        return lambda b, r, n: (b, n, r * ncol + group * 3 + c)

    def col_prev(c):
        return lambda b, r, n: (b, jnp.maximum(n - 1, 0), r * ncol + group * 3 + c)

    blk = (None, BLOCK, D_MODEL)
    o, lse = pl.pallas_call(
        _dilated_kernel,
        grid=(batch, dilation, nb),
        in_specs=[pl.BlockSpec(blk, col(0)),
                  pl.BlockSpec(blk, col_prev(1)),
                  pl.BlockSpec(blk, col(1)),
                  pl.BlockSpec(blk, col_prev(2)),
                  pl.BlockSpec(blk, col(2)),
                  pl.BlockSpec((None, 2, N_BIAS_COLS, BLOCK, 2 * BLOCK),
                               lambda b, r, n: (group, 0, 0, 0, 0))],
        out_specs=[pl.BlockSpec(blk, lambda b, r, n: (b, n, r)),
                   pl.BlockSpec((None, BLOCK, LANES), lambda b, r, n: (b, n, r))],
        out_shape=[jax.ShapeDtypeStruct((batch, sub_len, dilation * D_MODEL), BF16),
                   jax.ShapeDtypeStruct((batch, sub_len, dilation * LANES), F32)],
        compiler_params=_cparams(("parallel", "parallel", "arbitrary")),
        name=f"dilated_attn_d{dilation}",
    )(qkv_v, qkv_v, qkv_v, qkv_v, qkv_v, bias)
    return o.reshape(batch * seq, D_MODEL), lse.reshape(batch * seq, LANES)


def _combine_proj_kernel(h_ref, o0_ref, o1_ref, o2_ref, l0_ref, l1_ref, l2_ref,
                         e_ref, w_ref, out_ref):
    lses = [l0_ref[...], l1_ref[...], l2_ref[...]]
    outs = [o0_ref, o1_ref, o2_ref]
    mx = jnp.maximum(jnp.maximum(lses[0], lses[1]), lses[2])
    ws = [jnp.exp(l - mx) for l in lses]
    tot = ws[0] + ws[1] + ws[2]
    e = e_ref[...]
    o = None
    for g in range(N_GROUPS):
        a = ws[g] / tot
        a_hi = a.astype(BF16)
        a_lo = (a - a_hi.astype(F32)).astype(BF16)
        ae = (jnp.dot(a_hi, e, preferred_element_type=F32)
              + jnp.dot(a_lo, e, preferred_element_type=F32))
        term = ae * outs[g][...].astype(F32)
        o = term if o is None else o + term
    out_ref[...] = h_ref[...] + jnp.dot(o.astype(BF16), w_ref[...], preferred_element_type=F32)


def _combine_proj(h, outs, lses, w, *, tm=512):
    n, d = h.shape
    head_of_col = jnp.arange(d, dtype=jnp.int32) // HEAD_DIM
    expand = (jnp.arange(LANES, dtype=jnp.int32)[:, None] == head_of_col[None, :]).astype(BF16)
    row = lambda i: (i, 0)
    full = lambda i: (0, 0)
    return pl.pallas_call(
        _combine_proj_kernel,
        grid=(n // tm,),
        in_specs=[pl.BlockSpec((tm, d), row)] + [pl.BlockSpec((tm, d), row)] * 3
                 + [pl.BlockSpec((tm, LANES), row)] * 3
                 + [pl.BlockSpec((LANES, d), full), pl.BlockSpec((d, d), full)],
        out_specs=pl.BlockSpec((tm, d), row),
        out_shape=jax.ShapeDtypeStruct((n, d), F32),
        compiler_params=_cparams(("parallel",)),
        name="combine_proj",
    )(h, *outs, *lses, expand, w)


def _proj_residual_kernel(h_ref, x_ref, w_ref, out_ref):
    out_ref[...] = h_ref[...] + jnp.dot(x_ref[...], w_ref[...], preferred_element_type=F32)


def _proj_residual(h, x, w, *, tm=1024):
    n, d = h.shape
    return pl.pallas_call(
        _proj_residual_kernel,
        grid=(n // tm,),
        in_specs=[pl.BlockSpec((tm, d), lambda i: (i, 0)),
                  pl.BlockSpec((tm, x.shape[1]), lambda i: (i, 0)),
                  pl.BlockSpec(w.shape, lambda i: (0, 0))],
        out_specs=pl.BlockSpec((tm, d), lambda i: (i, 0)),
        out_shape=jax.ShapeDtypeStruct((n, d), F32),
        compiler_params=_cparams(("parallel",)),
        name="proj_residual",
    )(h, x, w)


def _mlp_kernel(h_ref, g_ref, w1_ref, w2_ref, out_ref, xn_ref, acc_ref):
    f = pl.program_id(1)

    @pl.when(f == 0)
    def _():
        xn_ref[...] = _rmsnorm_f32(h_ref[...], g_ref[...]).astype(BF16)
        acc_ref[...] = jnp.zeros_like(acc_ref)

    a = jnp.dot(xn_ref[...], w1_ref[...], preferred_element_type=F32)
    a = jnp.maximum(a, 0.0)
    a = (a * a).astype(BF16)
    acc_ref[...] += jnp.dot(a, w2_ref[...], preferred_element_type=F32)

    @pl.when(f == pl.num_programs(1) - 1)
    def _():
        out_ref[...] = h_ref[...] + acc_ref[...]


def _mlp(h, g, w1, w2, *, tm=1024, tf=512):
    n, d = h.shape
    dff = w1.shape[1]
    return pl.pallas_call(
        _mlp_kernel,
        grid=(n // tm, dff // tf),
        in_specs=[pl.BlockSpec((tm, d), lambda i, f: (i, 0)),
                  pl.BlockSpec((1, d), lambda i, f: (0, 0)),
                  pl.BlockSpec((d, tf), lambda i, f: (0, f)),
                  pl.BlockSpec((tf, d), lambda i, f: (f, 0))],
        out_specs=pl.BlockSpec((tm, d), lambda i, f: (i, 0)),
        out_shape=jax.ShapeDtypeStruct((n, d), F32),
        scratch_shapes=[pltpu.VMEM((tm, d), BF16), pltpu.VMEM((tm, d), F32)],
        compiler_params=_cparams(("parallel", "arbitrary")),
        name="mlp",
    )(h, g.reshape(1, d), w1, w2)


def _ple_kernel(h_ref, p_ref, g_ref, wg_ref, wp_ref, fg_ref, out_ref, *, final_norm):
    x = h_ref[...]
    xn = _rmsnorm_f32(x, g_ref[...]).astype(BF16)
    gate = jax.nn.sigmoid(jnp.dot(xn, wg_ref[...], preferred_element_type=F32))
    proj = jnp.dot(p_ref[...].astype(BF16), wp_ref[...], preferred_element_type=F32)
    y = x + gate * proj
    if final_norm:
        y = _rmsnorm_f32(y, fg_ref[...])
    out_ref[...] = y


def _ple(h, p_all, layer, g, wg, wp, fg, *, final_norm, tm=512):
    n, d = h.shape
    pd = p_all.shape[-1]
    return pl.pallas_call(
        functools.partial(_ple_kernel, final_norm=final_norm),
        grid=(n // tm,),
        in_specs=[pl.BlockSpec((tm, d), lambda i: (i, 0)),
                  pl.BlockSpec((None, tm, pd), lambda i: (layer, i, 0)),
                  pl.BlockSpec((1, d), lambda i: (0, 0)),
                  pl.BlockSpec((d, d), lambda i: (0, 0)),
                  pl.BlockSpec((pd, d), lambda i: (0, 0)),
                  pl.BlockSpec((1, d), lambda i: (0, 0))],
        out_specs=pl.BlockSpec((tm, d), lambda i: (i, 0)),
        out_shape=jax.ShapeDtypeStruct((n, d), F32),
        compiler_params=_cparams(("parallel",)),
        name="ple",
    )(h, p_all, g.reshape(1, d), wg, wp, fg.reshape(1, d))


def _diff_attn_kernel(lq1_ref, lk1_ref, lq2_ref, lk2_ref, q_ref, k_ref, v_ref, u_ref, sg_ref,
                      o_ref, m_sc, l_sc, acc_sc, *, bq, bk, lambda_init):
    q0 = pl.program_id(2) * bq
    nt = (((1,), (1,)), ((), ()))
    lane = lax.broadcasted_iota(jnp.int32, (bq, LANES), 1)
    qf = q_ref[...].astype(F32) * QK_SCALE
    q2 = jnp.concatenate([jnp.where(lane < HEAD_DIM, qf, 0.0),
                          jnp.where(lane >= HEAD_DIM, qf, 0.0)], axis=0).astype(BF16)

    m_sc[...] = jnp.full(m_sc.shape, NEG, F32)
    l_sc[...] = jnp.zeros(l_sc.shape, F32)
    acc_sc[...] = jnp.zeros(acc_sc.shape, F32)

    rb = bq // LANES
    n_kt = (q0 + bq + bk - 1) // bk

    def body(kt, carry):
        k0 = pl.multiple_of(kt * bk, bk)
        k = k_ref[pl.ds(k0, bk), :]
        v = v_ref[pl.ds(k0, bk), :]
        s = lax.dot_general(q2, k, nt, preferred_element_type=F32)
        rows = []
        for ib in range(2 * rb):
            col = ib // rb
            i0 = (ib % rb) * LANES
            tiles = []
            for jb in range(bk // LANES):
                t = (q0 + i0 - k0 - jb * LANES) // LANES
                t = jnp.where(t < 0, DIFF_MASK_TILE, jnp.minimum(t, DIFF_CONST_TILE))
                tiles.append(u_ref[t, col])
            rows.append(jnp.concatenate(tiles, axis=1))
        s = s + jnp.concatenate(rows, axis=0)
        m_prev = m_sc[...]
        m_new = jnp.maximum(m_prev, jnp.max(s, axis=-1, keepdims=True))
        alpha = jnp.exp(m_prev - m_new)
        p = jnp.exp(s - m_new)
        l_sc[...] = alpha * l_sc[...] + jnp.sum(p, axis=-1, keepdims=True)
        acc_sc[...] = alpha * acc_sc[...] + jnp.dot(p.astype(BF16), v, preferred_element_type=F32)
        m_sc[...] = m_new
        return carry

    lax.fori_loop(0, n_kt, body, 0)

    lam = (jnp.exp(jnp.sum(lq1_ref[...] * lk1_ref[...], keepdims=True))
           - jnp.exp(jnp.sum(lq2_ref[...] * lk2_ref[...], keepdims=True)) + lambda_init)
    acc = acc_sc[...]
    l = l_sc[...]
    o = acc[:bq] / l[:bq] - lam * (acc[bq:] / l[bq:])
    y = _rmsnorm_f32(o, sg_ref[...]) * (1.0 - lambda_init)
    o_ref[...] = y.astype(o_ref.dtype)


def _diff_attention(qkv, u, lq1, lk1, lq2, lk2, subln, lambda_init, batch, seq, *, bq=256, bk=512):
    nh = N_HEADS_B
    vec = lambda a: a.reshape(1, -1)
    small = pl.BlockSpec((1, HEAD_DIM), lambda b, h, i: (0, 0))
    return pl.pallas_call(
        functools.partial(_diff_attn_kernel, bq=bq, bk=bk, lambda_init=lambda_init),
        grid=(batch, nh, seq // bq),
        in_specs=[small, small, small, small,
                  pl.BlockSpec((None, bq, LANES), lambda b, h, i: (b, i, h)),
                  pl.BlockSpec((None, seq, LANES), lambda b, h, i: (b, 0, nh + h)),
                  pl.BlockSpec((None, seq, LANES), lambda b, h, i: (b, 0, 2 * nh + h)),
                  pl.BlockSpec((DIFF_N_TILES, 2, LANES, LANES), lambda b, h, i: (0, h, 0, 0)),
                  pl.BlockSpec((1, LANES), lambda b, h, i: (0, 0))],
        out_specs=pl.BlockSpec((None, bq, LANES), lambda b, h, i: (b, i, h)),
        out_shape=jax.ShapeDtypeStruct((batch, seq, D_MODEL), BF16),
        scratch_shapes=[pltpu.VMEM((2 * bq, 1), F32), pltpu.VMEM((2 * bq, 1), F32),
                        pltpu.VMEM((2 * bq, LANES), F32)],
        compiler_params=_cparams(("parallel", "parallel", "arbitrary")),
        name="diff_attn",
    )(vec(lq1), vec(lk1), vec(lq2), vec(lk2), qkv, qkv, qkv, u, vec(subln))


def kernel(x, p, rel_bias, a_w_qkv, a_w_o, b_w_qkv, b_w_o, b_lambda_q1, b_lambda_k1, b_lambda_q2, b_lambda_k2, b_subln, norm_mix, norm_mlp, w_ff1, w_ff2, norm_ple, w_ple_gate, w_ple_proj, final_norm):
    batch, seq, d = x.shape
    depth = p.shape[0]
    n = batch * seq
    h = x.reshape(n, d)
    p_all = p.reshape(depth, n, p.shape[-1])
    n_mixers = 2

    for i in range(depth):
        j = i // n_mixers
        if i % n_mixers == 0:
            qkv = _norm_matmul(h, norm_mix[i], a_w_qkv[j].astype(BF16))
            bias = _build_dilated_bias(rel_bias)
            outs, lses = [], []
            for g, (_, dilation) in enumerate(DIL_CONFIGS):
                o_g, lse_g = _dilated_attention(qkv, bias, g, dilation, batch, seq)
                outs.append(o_g)
                lses.append(lse_g)
            h = _combine_proj(h, outs, lses, a_w_o[j].astype(BF16))
        else:
            lambda_init = 0.8 - 0.6 * math.exp(-0.3 * i)
            qkv = _norm_matmul(h, norm_mix[i], b_w_qkv[j].astype(BF16))
            u = _build_diff_bias(rel_bias)
            o = _diff_attention(qkv.reshape(batch, seq, 3 * d), u,
                                b_lambda_q1[j], b_lambda_k1[j], b_lambda_q2[j], b_lambda_k2[j],
                                b_subln[j], lambda_init, batch, seq)
            h = _proj_residual(h, o.reshape(n, d), b_w_o[j].astype(BF16))
        h = _mlp(h, norm_mlp[i], w_ff1[i].astype(BF16), w_ff2[i].astype(BF16))
        h = _ple(h, p_all, i, norm_ple[i], w_ple_gate[i].astype(BF16), w_ple_proj[i].astype(BF16),
                 final_norm, final_norm=(i == depth - 1))
    return h.reshape(batch, seq, d)
```

```python
import functools
import math

import jax
import jax.numpy as jnp
from jax import lax
from jax.experimental import pallas as pl
from jax.experimental.pallas import tpu as pltpu

F32 = jnp.float32
BF16 = jnp.bfloat16

D_MODEL = 1024
HEAD_DIM = 64
BLOCK = 128
DIL_CONFIGS = ((128, 1), (512, 4), (2048, 16))
N_GROUPS = len(DIL_CONFIGS)
N_HEADS_A = D_MODEL // HEAD_DIM
N_HEADS_B = D_MODEL // (2 * HEAD_DIM)
N_BUCKETS = 32
MAX_DISTANCE = 2048
N_BIAS_COLS = 16
EPS = 1e-6
NEG = -1e30
LANES = 128
QK_SCALE = HEAD_DIM ** -0.5
LOG2E = math.log2(math.e)

DIFF_CONST_TILE = (MAX_DISTANCE + LANES - 1) // LANES + 1
DIFF_MASK_TILE = DIFF_CONST_TILE + 1
DIFF_N_TILES = DIFF_MASK_TILE + 1
SUM_ROWS = 16

VMEM_LIMIT = 48 * 1024 * 1024


def _cparams(sem):
    return pltpu.CompilerParams(dimension_semantics=sem, vmem_limit_bytes=VMEM_LIMIT)


def _rmsnorm_f32(x, g):
    ms = jnp.mean(x * x, axis=-1, keepdims=True)
    return x * lax.rsqrt(ms + EPS) * g


def _rel_bucket(dist):
    n = jnp.maximum(dist, 0)
    max_exact = N_BUCKETS // 2
    nf = jnp.maximum(n, 1).astype(F32)
    large = max_exact + (jnp.log(nf / max_exact) / math.log(MAX_DISTANCE / max_exact)
                         * (N_BUCKETS - max_exact)).astype(jnp.int32)
    large = jnp.minimum(large, N_BUCKETS - 1)
    return jnp.where(n < max_exact, n, large)


def _table_lookup(bucket, tab_ref, col):
    acc = jnp.zeros(bucket.shape, F32)
    for b in range(N_BUCKETS):
        acc = jnp.where(bucket == b, tab_ref[b, col], acc)
    return acc


def _dilated_bias_kernel(tab_ref, o_ref):
    g = pl.program_id(0)
    first = pl.program_id(1)
    dilation = jnp.where(g == 0, DIL_CONFIGS[0][1],
                         jnp.where(g == 1, DIL_CONFIGS[1][1], DIL_CONFIGS[2][1]))
    qi = lax.broadcasted_iota(jnp.int32, (BLOCK, 2 * BLOCK), 0)
    kj = lax.broadcasted_iota(jnp.int32, (BLOCK, 2 * BLOCK), 1)
    sub = qi + BLOCK - kj
    valid = (sub >= 0) & (sub <= BLOCK) & ((first == 0) | (kj >= BLOCK))
    bucket = _rel_bucket(sub * dilation)
    for c in range(N_BIAS_COLS):
        o_ref[c] = jnp.where(valid, _table_lookup(bucket, tab_ref, c), NEG)


def _build_dilated_bias(rel_bias):
    return pl.pallas_call(
        _dilated_bias_kernel,
        grid=(N_GROUPS, 2),
        in_specs=[pl.BlockSpec(memory_space=pltpu.SMEM)],
        out_specs=pl.BlockSpec((None, None, N_BIAS_COLS, BLOCK, 2 * BLOCK),
                               lambda g, f: (g, f, 0, 0, 0)),
        out_shape=jax.ShapeDtypeStruct((N_GROUPS, 2, N_BIAS_COLS, BLOCK, 2 * BLOCK), F32),
        compiler_params=_cparams(("arbitrary", "arbitrary")),
        name="dilated_bias",
    )(rel_bias)


def _diff_bias_kernel(tab_ref, o_ref):
    t = pl.program_id(0)
    kj = lax.broadcasted_iota(jnp.int32, (LANES, LANES), 0)
    qi = lax.broadcasted_iota(jnp.int32, (LANES, LANES), 1)
    dist = t * LANES + qi - kj
    masked = (dist < 0) | (t == DIFF_MASK_TILE)
    bucket = _rel_bucket(dist)
    for c in range(N_BIAS_COLS):
        o_ref[c] = jnp.where(masked, NEG, _table_lookup(bucket, tab_ref, c) * LOG2E)


def _build_diff_bias(rel_bias):
    return pl.pallas_call(
        _diff_bias_kernel,
        grid=(DIFF_N_TILES,),
        in_specs=[pl.BlockSpec(memory_space=pltpu.SMEM)],
        out_specs=pl.BlockSpec((None, N_BIAS_COLS, LANES, LANES), lambda t: (t, 0, 0, 0)),
        out_shape=jax.ShapeDtypeStruct((DIFF_N_TILES, N_BIAS_COLS, LANES, LANES), F32),
        compiler_params=_cparams(("arbitrary",)),
        name="diff_bias",
    )(rel_bias)


def _norm_matmul_kernel(x_ref, g_ref, w_ref, o_ref, xn_ref, *, tile_scales):
    j = pl.program_id(1)

    @pl.when(j == 0)
    def _():
        xn_ref[...] = _rmsnorm_f32(x_ref[...], g_ref[...]).astype(BF16)

    y = jnp.dot(xn_ref[...], w_ref[...], preferred_element_type=F32)
    if tile_scales is not None:
        scale = jnp.float32(tile_scales[-1])
        for t in range(len(tile_scales) - 2, -1, -1):
            scale = jnp.where(j == t, jnp.float32(tile_scales[t]), scale)
        y = y * scale
    o_ref[...] = y.astype(o_ref.dtype)


def _norm_matmul(x, g, w, *, tm=1024, tn=1024, tile_scales=None):
    n, d = x.shape
    nout = w.shape[1]
    assert tile_scales is None or len(tile_scales) == nout // tn
    return pl.pallas_call(
        functools.partial(_norm_matmul_kernel, tile_scales=tile_scales),
        grid=(n // tm, nout // tn),
        in_specs=[pl.BlockSpec((tm, d), lambda i, j: (i, 0)),
                  pl.BlockSpec((1, d), lambda i, j: (0, 0)),
                  pl.BlockSpec((d, tn), lambda i, j: (0, j))],
        out_specs=pl.BlockSpec((tm, tn), lambda i, j: (i, j)),
        out_shape=jax.ShapeDtypeStruct((n, nout), BF16),
        scratch_shapes=[pltpu.VMEM((tm, d), BF16)],
        compiler_params=_cparams(("parallel", "arbitrary")),
        name="norm_matmul",
    )(x, g.reshape(1, d), w)


def _norm_matmul_t_kernel(x_ref, g_ref, wt_ref, o_ref):
    xn = _rmsnorm_f32(x_ref[...], g_ref[...]).astype(BF16)
    nt = (((1,), (1,)), ((), ()))
    o_ref[...] = lax.dot_general(wt_ref[...], xn, nt,
                                 preferred_element_type=F32).astype(o_ref.dtype)


def _norm_matmul_t(x, g, wt, *, tm):
    n, d = x.shape
    nout = wt.shape[0]
    return pl.pallas_call(
        _norm_matmul_t_kernel,
        grid=(n // tm,),
        in_specs=[pl.BlockSpec((tm, d), lambda i: (i, 0)),
                  pl.BlockSpec((1, d), lambda i: (0, 0)),
                  pl.BlockSpec((nout, d), lambda i: (0, 0))],
        out_specs=pl.BlockSpec((None, nout, tm), lambda i: (i, 0, 0)),
        out_shape=jax.ShapeDtypeStruct((n // tm, nout, tm), BF16),
        compiler_params=_cparams(("parallel",)),
        name="norm_matmul_t",
    )(x, g.reshape(1, d), wt)


def _dilated_kernel(q_ref, kp_ref, kc_ref, vp_ref, vc_ref, bias_ref, o_ref, lse_ref):
    first = (pl.program_id(2) == 0).astype(jnp.int32)
    lane = lax.broadcasted_iota(jnp.int32, (BLOCK, LANES), 1)
    lo = lane < HEAD_DIM
    lse_tile = jnp.zeros((BLOCK, LANES), F32)
    nt = (((1,), (1,)), ((), ()))
    for hp in range(N_HEADS_A // 2):
        sl = slice(hp * LANES, (hp + 1) * LANES)
        qf = q_ref[:, sl].astype(F32) * QK_SCALE
        k = jnp.concatenate([kp_ref[:, sl], kc_ref[:, sl]], axis=0)
        v = jnp.concatenate([vp_ref[:, sl], vc_ref[:, sl]], axis=0)
        outs = []
        for j in range(2):
            h = 2 * hp + j
            qm = jnp.where(lo if j == 0 else ~lo, qf, 0.0).astype(BF16)
            s = lax.dot_general(qm, k, nt, preferred_element_type=F32)
            s = s + bias_ref[first, h]
            m = jnp.max(s, axis=-1, keepdims=True)
            p = jnp.exp(s - m)
            l = jnp.sum(p, axis=-1, keepdims=True)
            pv = jnp.dot(p.astype(BF16), v, preferred_element_type=F32)
            outs.append(pv / l)
            lse_tile = jnp.where(lane == h, m + jnp.log(l), lse_tile)
        o_ref[:, sl] = jnp.where(lo, outs[0], outs[1]).astype(o_ref.dtype)
    lse_ref[...] = lse_tile


def _dilated_attention(qkv, bias, group, dilation, batch, seq):
    assert seq % (dilation * BLOCK) == 0
    sub_len = seq // dilation
    nb = sub_len // BLOCK
    ncol = qkv.shape[1] // D_MODEL
    qkv_v = qkv.reshape(batch, sub_len, dilation * qkv.shape[1])

    def col(c):
        return lambda b, r, n: (b, n, r * ncol + group * 3 + c)

    def col_prev(c):
        return lambda b, r, n: (b, jnp.maximum(n - 1, 0), r * ncol + group * 3 + c)

    blk = (None, BLOCK, D_MODEL)
    o, lse = pl.pallas_call(
        _dilated_kernel,
        grid=(batch, dilation, nb),
        in_specs=[pl.BlockSpec(blk, col(0)),
                  pl.BlockSpec(blk, col_prev(1)),
                  pl.BlockSpec(blk, col(1)),
                  pl.BlockSpec(blk, col_prev(2)),
                  pl.BlockSpec(blk, col(2)),
                  pl.BlockSpec((None, 2, N_BIAS_COLS, BLOCK, 2 * BLOCK),
                               lambda b, r, n: (group, 0, 0, 0, 0))],
        out_specs=[pl.BlockSpec(blk, lambda b, r, n: (b, n, r)),
                   pl.BlockSpec((None, BLOCK, LANES), lambda b, r, n: (b, n, r))],
        out_shape=[jax.ShapeDtypeStruct((batch, sub_len, dilation * D_MODEL), BF16),
                   jax.ShapeDtypeStruct((batch, sub_len, dilation * LANES), F32)],
        compiler_params=_cparams(("parallel", "parallel", "arbitrary")),
        name=f"dilated_attn_d{dilation}",
    )(qkv_v, qkv_v, qkv_v, qkv_v, qkv_v, bias)
    return o.reshape(batch * seq, D_MODEL), lse.reshape(batch * seq, LANES)


def _combine_proj_kernel(h_ref, o0_ref, o1_ref, o2_ref, l0_ref, l1_ref, l2_ref,
                         e_ref, w_ref, out_ref):
    lses = [l0_ref[...], l1_ref[...], l2_ref[...]]
    outs = [o0_ref, o1_ref, o2_ref]
    mx = jnp.maximum(jnp.maximum(lses[0], lses[1]), lses[2])
    ws = [jnp.exp(l - mx) for l in lses]
    tot = ws[0] + ws[1] + ws[2]
    e = e_ref[...]
    o = None
    for g in range(N_GROUPS):
        a = ws[g] / tot
        a_hi = a.astype(BF16)
        a_lo = (a - a_hi.astype(F32)).astype(BF16)
        ae = (jnp.dot(a_hi, e, preferred_element_type=F32)
              + jnp.dot(a_lo, e, preferred_element_type=F32))
        term = ae * outs[g][...].astype(F32)
        o = term if o is None else o + term
    out_ref[...] = h_ref[...] + jnp.dot(o.astype(BF16), w_ref[...], preferred_element_type=F32)


def _combine_proj(h, outs, lses, w, *, tm=512):
    n, d = h.shape
    head_of_col = jnp.arange(d, dtype=jnp.int32) // HEAD_DIM
    expand = (jnp.arange(LANES, dtype=jnp.int32)[:, None] == head_of_col[None, :]).astype(BF16)
    row = lambda i: (i, 0)
    full = lambda i: (0, 0)
    return pl.pallas_call(
        _combine_proj_kernel,
        grid=(n // tm,),
        in_specs=[pl.BlockSpec((tm, d), row)] + [pl.BlockSpec((tm, d), row)] * 3
                 + [pl.BlockSpec((tm, LANES), row)] * 3
                 + [pl.BlockSpec((LANES, d), full), pl.BlockSpec((d, d), full)],
        out_specs=pl.BlockSpec((tm, d), row),
        out_shape=jax.ShapeDtypeStruct((n, d), F32),
        compiler_params=_cparams(("parallel",)),
        name="combine_proj",
    )(h, *outs, *lses, expand, w)


def _proj_residual_kernel(h_ref, x_ref, w_ref, out_ref):
    out_ref[...] = h_ref[...] + jnp.dot(x_ref[...], w_ref[...], preferred_element_type=F32)


def _proj_residual(h, x, w, *, tm=1024):
    n, d = h.shape
    return pl.pallas_call(
        _proj_residual_kernel,
        grid=(n // tm,),
        in_specs=[pl.BlockSpec((tm, d), lambda i: (i, 0)),
                  pl.BlockSpec((tm, x.shape[1]), lambda i: (i, 0)),
                  pl.BlockSpec(w.shape, lambda i: (0, 0))],
        out_specs=pl.BlockSpec((tm, d), lambda i: (i, 0)),
        out_shape=jax.ShapeDtypeStruct((n, d), F32),
        compiler_params=_cparams(("parallel",)),
        name="proj_residual",
    )(h, x, w)


def _mlp_kernel(h_ref, g_ref, w1_ref, w2_ref, out_ref, xn_ref, acc_ref):
    f = pl.program_id(1)

    @pl.when(f == 0)
    def _():
        xn_ref[...] = _rmsnorm_f32(h_ref[...], g_ref[...]).astype(BF16)
        acc_ref[...] = jnp.zeros_like(acc_ref)

    a = jnp.dot(xn_ref[...], w1_ref[...], preferred_element_type=F32)
    a = jnp.maximum(a, 0.0)
    a = (a * a).astype(BF16)
    acc_ref[...] += jnp.dot(a, w2_ref[...], preferred_element_type=F32)

    @pl.when(f == pl.num_programs(1) - 1)
    def _():
        out_ref[...] = h_ref[...] + acc_ref[...]


def _mlp(h, g, w1, w2, *, tm=1024, tf=512):
    n, d = h.shape
    dff = w1.shape[1]
    return pl.pallas_call(
        _mlp_kernel,
        grid=(n // tm, dff // tf),
        in_specs=[pl.BlockSpec((tm, d), lambda i, f: (i, 0)),
                  pl.BlockSpec((1, d), lambda i, f: (0, 0)),
                  pl.BlockSpec((d, tf), lambda i, f: (0, f)),
                  pl.BlockSpec((tf, d), lambda i, f: (f, 0))],
        out_specs=pl.BlockSpec((tm, d), lambda i, f: (i, 0)),
        out_shape=jax.ShapeDtypeStruct((n, d), F32),
        scratch_shapes=[pltpu.VMEM((tm, d), BF16), pltpu.VMEM((tm, d), F32)],
        compiler_params=_cparams(("parallel", "arbitrary")),
        name="mlp",
    )(h, g.reshape(1, d), w1, w2)


def _ple_kernel(h_ref, p_ref, g_ref, wg_ref, wp_ref, fg_ref, out_ref, *, final_norm):
    x = h_ref[...]
    xn = _rmsnorm_f32(x, g_ref[...]).astype(BF16)
    gate = jax.nn.sigmoid(jnp.dot(xn, wg_ref[...], preferred_element_type=F32))
    proj = jnp.dot(p_ref[...].astype(BF16), wp_ref[...], preferred_element_type=F32)
    y = x + gate * proj
    if final_norm:
        y = _rmsnorm_f32(y, fg_ref[...])
    out_ref[...] = y


def _ple(h, p_all, layer, g, wg, wp, fg, *, final_norm, tm=512):
    n, d = h.shape
    pd = p_all.shape[-1]
    return pl.pallas_call(
        functools.partial(_ple_kernel, final_norm=final_norm),
        grid=(n // tm,),
        in_specs=[pl.BlockSpec((tm, d), lambda i: (i, 0)),
                  pl.BlockSpec((None, tm, pd), lambda i: (layer, i, 0)),
                  pl.BlockSpec((1, d), lambda i: (0, 0)),
                  pl.BlockSpec((d, d), lambda i: (0, 0)),
                  pl.BlockSpec((pd, d), lambda i: (0, 0)),
                  pl.BlockSpec((1, d), lambda i: (0, 0))],
        out_specs=pl.BlockSpec((tm, d), lambda i: (i, 0)),
        out_shape=jax.ShapeDtypeStruct((n, d), F32),
        compiler_params=_cparams(("parallel",)),
        name="ple",
    )(h, p_all, g.reshape(1, d), wg, wp, fg.reshape(1, d))


def _diff_attn_kernel(lq1_ref, lk1_ref, lq2_ref, lk2_ref, q_ref, k_ref, vt_ref, u_ref, sg_ref,
                      o_ref, m_sc, acc_sc, s_sc, smax_sc, *, bq, bk, lambda_init):
    assert bq == 2 * bk
    q0 = pl.program_id(2) * bq
    nt = (((1,), (1,)), ((), ()))
    lane = lax.broadcasted_iota(jnp.int32, (bq, LANES), 1)
    qf = q_ref[...].astype(F32)
    q2 = jnp.concatenate([jnp.where(lane < HEAD_DIM, qf, 0.0),
                          jnp.where(lane >= HEAD_DIM, qf, 0.0)], axis=0).astype(BF16)

    m_sc[...] = jnp.full(m_sc.shape, NEG, F32)
    acc_sc[...] = jnp.zeros(acc_sc.shape, F32)

    qb = bq // LANES
    n_pairs = q0 // bq + 1
    last_kt = 2 * n_pairs - 1
    ones = jnp.ones((SUM_ROWS, bk), BF16)

    def scores(kt):
        k0 = pl.multiple_of(kt * bk, bk)
        s = lax.dot_general(k_ref[pl.ds(k0, bk), :], q2, nt, preferred_element_type=F32)
        rows = []
        for jb in range(bk // LANES):
            tiles = []
            for ib in range(2 * qb):
                col = ib // qb
                i0 = (ib % qb) * LANES
                t = (q0 + i0 - k0 - jb * LANES) // LANES
                t = jnp.where(t < 0, DIFF_MASK_TILE, jnp.minimum(t, DIFF_CONST_TILE))
                tiles.append(u_ref[t, col])
            rows.append(jnp.concatenate(tiles, axis=1))
        s = s + jnp.concatenate(rows, axis=0)
        return s, jnp.max(s, axis=0, keepdims=True)

    def accumulate(kt, s, smax):
        m_prev = m_sc[...]
        m_new = jnp.maximum(m_prev, smax)
        alpha = jnp.exp2(m_prev - m_new)
        p = jnp.exp2(s - m_new).astype(BF16)
        v1 = jnp.concatenate([vt_ref[kt], ones], axis=0)
        acc_sc[...] = alpha * acc_sc[...] + jnp.dot(v1, p, preferred_element_type=F32)
        m_sc[...] = m_new

    s_sc[...], smax_sc[...] = scores(0)

    def body(i, carry):
        sb, sbmax = scores(2 * i + 1)
        accumulate(2 * i, s_sc[...], smax_sc[...])
        s_sc[...], smax_sc[...] = scores(jnp.minimum(2 * i + 2, last_kt))
        accumulate(2 * i + 1, sb, sbmax)
        return carry

    lax.fori_loop(0, n_pairs, body, 0)

    lam = (jnp.exp(jnp.sum(lq1_ref[...] * lk1_ref[...], keepdims=True))
           - jnp.exp(jnp.sum(lq2_ref[...] * lk2_ref[...], keepdims=True)) + lambda_init)
    accl = acc_sc[...]
    acc = accl[:LANES] / accl[LANES:LANES + 1]
    o = (acc[:, :bq] - lam * acc[:, bq:]).T
    y = _rmsnorm_f32(o, sg_ref[...]) * (1.0 - lambda_init)
    o_ref[...] = y.astype(o_ref.dtype)


def _diff_attention(qk, vt, u, lq1, lk1, lq2, lk2, subln, lambda_init, batch, seq, *, bq, bk):
    nh = N_HEADS_B
    nkt = seq // bk
    vec = lambda a: a.reshape(1, -1)
    small = pl.BlockSpec((1, HEAD_DIM), lambda b, h, i: (0, 0))
    return pl.pallas_call(
        functools.partial(_diff_attn_kernel, bq=bq, bk=bk, lambda_init=lambda_init),
        grid=(batch, nh, seq // bq),
        in_specs=[small, small, small, small,
                  pl.BlockSpec((None, bq, LANES), lambda b, h, i: (b, i, h)),
                  pl.BlockSpec((None, seq, LANES), lambda b, h, i: (b, 0, nh + h)),
                  pl.BlockSpec((nkt, LANES, bk), lambda b, h, i: (b, h, 0)),
                  pl.BlockSpec((DIFF_N_TILES, 2, LANES, LANES), lambda b, h, i: (0, h, 0, 0)),
                  pl.BlockSpec((1, LANES), lambda b, h, i: (0, 0))],
        out_specs=pl.BlockSpec((None, bq, LANES), lambda b, h, i: (b, i, h)),
        out_shape=jax.ShapeDtypeStruct((batch, seq, D_MODEL), BF16),
        scratch_shapes=[pltpu.VMEM((1, 2 * bq), F32),
                        pltpu.VMEM((LANES + SUM_ROWS, 2 * bq), F32),
                        pltpu.VMEM((bk, 2 * bq), F32),
                        pltpu.VMEM((1, 2 * bq), F32)],
        compiler_params=_cparams(("parallel", "parallel", "arbitrary")),
        name="diff_attn",
    )(vec(lq1), vec(lk1), vec(lq2), vec(lk2), qk, qk, vt, u, vec(subln))


def kernel(x, p, rel_bias, a_w_qkv, a_w_o, b_w_qkv, b_w_o, b_lambda_q1, b_lambda_k1, b_lambda_q2, b_lambda_k2, b_subln, norm_mix, norm_mlp, w_ff1, w_ff2, norm_ple, w_ple_gate, w_ple_proj, final_norm):
    batch, seq, d = x.shape
    depth = p.shape[0]
    n = batch * seq
    h = x.reshape(n, d)
    p_all = p.reshape(depth, n, p.shape[-1])
    n_mixers = 2

    for i in range(depth):
        j = i // n_mixers
        if i % n_mixers == 0:
            qkv = _norm_matmul(h, norm_mix[i], a_w_qkv[j].astype(BF16))
            bias = _build_dilated_bias(rel_bias)
            outs, lses = [], []
            for g, (_, dilation) in enumerate(DIL_CONFIGS):
                o_g, lse_g = _dilated_attention(qkv, bias, g, dilation, batch, seq)
                outs.append(o_g)
                lses.append(lse_g)
            h = _combine_proj(h, outs, lses, a_w_o[j].astype(BF16))
        else:
            lambda_init = 0.8 - 0.6 * math.exp(-0.3 * i)
            bq, bk = 512, 256
            w_b = b_w_qkv[j].astype(BF16)
            qk = _norm_matmul(h, norm_mix[i], w_b[:, :2 * d], tile_scales=(QK_SCALE * LOG2E, 1.0))
            vt = _norm_matmul_t(h, norm_mix[i], w_b[:, 2 * d:].T, tm=bk)
            u = _build_diff_bias(rel_bias)
            o = _diff_attention(qk.reshape(batch, seq, 2 * d), vt, u,
                                b_lambda_q1[j], b_lambda_k1[j], b_lambda_q2[j], b_lambda_k2[j],
                                b_subln[j], lambda_init, batch, seq, bq=bq, bk=bk)
            h = _proj_residual(h, o.reshape(n, d), b_w_o[j].astype(BF16))
        h = _mlp(h, norm_mlp[i], w_ff1[i].astype(BF16), w_ff2[i].astype(BF16))
        h = _ple(h, p_all, i, norm_ple[i], w_ple_gate[i].astype(BF16), w_ple_proj[i].astype(BF16),
                 final_norm, final_norm=(i == depth - 1))
    return h.reshape(batch, seq, d)
```

```python
import functools
import math

import jax
import jax.numpy as jnp
from jax import lax
from jax.experimental import pallas as pl
from jax.experimental.pallas import tpu as pltpu

F32 = jnp.float32
BF16 = jnp.bfloat16

D_MODEL = 1024
HEAD_DIM = 64
BLOCK = 128
DIL_CONFIGS = ((128, 1), (512, 4), (2048, 16))
N_GROUPS = len(DIL_CONFIGS)
N_HEADS_A = D_MODEL // HEAD_DIM
N_HEADS_B = D_MODEL // (2 * HEAD_DIM)
N_BUCKETS = 32
MAX_DISTANCE = 2048
N_BIAS_COLS = 16
EPS = 1e-6
NEG = -1e30
LANES = 128
LOG2_LANES = 7
QK_SCALE = HEAD_DIM ** -0.5
LOG2E = math.log2(math.e)

DIFF_CONST_TILE = (MAX_DISTANCE + LANES - 1) // LANES + 1
DIFF_MASK_TILE = DIFF_CONST_TILE + 1
DIFF_N_TILES = DIFF_MASK_TILE + 1
SUM_ROWS = 16

VMEM_LIMIT = 48 * 1024 * 1024


def _cparams(sem):
    return pltpu.CompilerParams(dimension_semantics=sem, vmem_limit_bytes=VMEM_LIMIT)


def _rmsnorm_f32(x, g):
    ms = jnp.mean(x * x, axis=-1, keepdims=True)
    return x * lax.rsqrt(ms + EPS) * g


def _rel_bucket(dist):
    n = jnp.maximum(dist, 0)
    max_exact = N_BUCKETS // 2
    nf = jnp.maximum(n, 1).astype(F32)
    large = max_exact + (jnp.log(nf / max_exact) / math.log(MAX_DISTANCE / max_exact)
                         * (N_BUCKETS - max_exact)).astype(jnp.int32)
    large = jnp.minimum(large, N_BUCKETS - 1)
    return jnp.where(n < max_exact, n, large)


def _table_lookup(bucket, tab_ref, col):
    acc = jnp.zeros(bucket.shape, F32)
    for b in range(N_BUCKETS):
        acc = jnp.where(bucket == b, tab_ref[b, col], acc)
    return acc


def _dilated_bias_kernel(tab_ref, o_ref):
    g = pl.program_id(0)
    first = pl.program_id(1)
    dilation = jnp.where(g == 0, DIL_CONFIGS[0][1],
                         jnp.where(g == 1, DIL_CONFIGS[1][1], DIL_CONFIGS[2][1]))
    qi = lax.broadcasted_iota(jnp.int32, (BLOCK, 2 * BLOCK), 0)
    kj = lax.broadcasted_iota(jnp.int32, (BLOCK, 2 * BLOCK), 1)
    sub = qi + BLOCK - kj
    valid = (sub >= 0) & (sub <= BLOCK) & ((first == 0) | (kj >= BLOCK))
    bucket = _rel_bucket(sub * dilation)
    for c in range(N_BIAS_COLS):
        o_ref[c] = jnp.where(valid, _table_lookup(bucket, tab_ref, c), NEG)


def _build_dilated_bias(rel_bias):
    return pl.pallas_call(
        _dilated_bias_kernel,
        grid=(N_GROUPS, 2),
        in_specs=[pl.BlockSpec(memory_space=pltpu.SMEM)],
        out_specs=pl.BlockSpec((None, None, N_BIAS_COLS, BLOCK, 2 * BLOCK),
                               lambda g, f: (g, f, 0, 0, 0)),
        out_shape=jax.ShapeDtypeStruct((N_GROUPS, 2, N_BIAS_COLS, BLOCK, 2 * BLOCK), F32),
        compiler_params=_cparams(("arbitrary", "arbitrary")),
        name="dilated_bias",
    )(rel_bias)


def _diff_bias_kernel(tab_ref, o_ref):
    t = pl.program_id(0)
    kj = lax.broadcasted_iota(jnp.int32, (LANES, LANES), 0)
    qi = lax.broadcasted_iota(jnp.int32, (LANES, LANES), 1)
    dist = t * LANES + qi - kj
    masked = (dist < 0) | (t == DIFF_MASK_TILE)
    bucket = _rel_bucket(dist)
    for c in range(N_BIAS_COLS):
        o_ref[c] = jnp.where(masked, NEG, _table_lookup(bucket, tab_ref, c) * LOG2E)


def _build_diff_bias(rel_bias):
    return pl.pallas_call(
        _diff_bias_kernel,
        grid=(DIFF_N_TILES,),
        in_specs=[pl.BlockSpec(memory_space=pltpu.SMEM)],
        out_specs=pl.BlockSpec((None, N_BIAS_COLS, LANES, LANES), lambda t: (t, 0, 0, 0)),
        out_shape=jax.ShapeDtypeStruct((DIFF_N_TILES, N_BIAS_COLS, LANES, LANES), F32),
        compiler_params=_cparams(("arbitrary",)),
        name="diff_bias",
    )(rel_bias)


def _norm_matmul_kernel(x_ref, g_ref, w_ref, o_ref, xn_ref, *, tile_scales):
    j = pl.program_id(1)

    @pl.when(j == 0)
    def _():
        xn_ref[...] = _rmsnorm_f32(x_ref[...], g_ref[...]).astype(BF16)

    y = jnp.dot(xn_ref[...], w_ref[...], preferred_element_type=F32)
    if tile_scales is not None:
        scale = jnp.float32(tile_scales[-1])
        for t in range(len(tile_scales) - 2, -1, -1):
            scale = jnp.where(j == t, jnp.float32(tile_scales[t]), scale)
        y = y * scale
    o_ref[...] = y.astype(o_ref.dtype)


def _norm_matmul(x, g, w, *, tm=1024, tn=1024, tile_scales=None):
    n, d = x.shape
    nout = w.shape[1]
    assert tile_scales is None or len(tile_scales) == nout // tn
    return pl.pallas_call(
        functools.partial(_norm_matmul_kernel, tile_scales=tile_scales),
        grid=(n // tm, nout // tn),
        in_specs=[pl.BlockSpec((tm, d), lambda i, j: (i, 0)),
                  pl.BlockSpec((1, d), lambda i, j: (0, 0)),
                  pl.BlockSpec((d, tn), lambda i, j: (0, j))],
        out_specs=pl.BlockSpec((tm, tn), lambda i, j: (i, j)),
        out_shape=jax.ShapeDtypeStruct((n, nout), BF16),
        scratch_shapes=[pltpu.VMEM((tm, d), BF16)],
        compiler_params=_cparams(("parallel", "arbitrary")),
        name="norm_matmul",
    )(x, g.reshape(1, d), w)


def _norm_matmul_dilated_kernel(x_ref, g_ref, w_ref, o_ref, y_sc, *, dilation):
    tm = x_ref.shape[0]
    nout = w_ref.shape[1]
    xn = _rmsnorm_f32(x_ref[...], g_ref[...]).astype(BF16)
    y = jnp.dot(xn, w_ref[...], preferred_element_type=F32)
    n_slabs = nout // LANES
    for c in range(n_slabs):
        y_sc[c] = y[:, c * LANES:(c + 1) * LANES]
    for r in range(dilation):
        for c in range(n_slabs):
            rows = y_sc[c, pl.ds(r, tm // dilation, stride=dilation), :]
            o_ref[:, r * nout + c * LANES:r * nout + (c + 1) * LANES] = rows.astype(o_ref.dtype)


def _norm_matmul_dilated(x, g, w, dilation, *, tm=512):
    n, d = x.shape
    nout = w.shape[1]
    assert tm % (16 * dilation) == 0 and nout % LANES == 0
    return pl.pallas_call(
        functools.partial(_norm_matmul_dilated_kernel, dilation=dilation),
        grid=(n // tm,),
        in_specs=[pl.BlockSpec((tm, d), lambda i: (i, 0)),
                  pl.BlockSpec((1, d), lambda i: (0, 0)),
                  pl.BlockSpec((d, nout), lambda i: (0, 0))],
        out_specs=pl.BlockSpec((tm // dilation, dilation * nout), lambda i: (i, 0)),
        out_shape=jax.ShapeDtypeStruct((n // dilation, dilation * nout), BF16),
        scratch_shapes=[pltpu.VMEM((nout // LANES, tm, LANES), F32)],
        compiler_params=_cparams(("parallel",)),
        name=f"norm_matmul_d{dilation}",
    )(x, g.reshape(1, d), w)


def _norm_matmul_t_kernel(x_ref, g_ref, wt_ref, o_ref):
    xn = _rmsnorm_f32(x_ref[...], g_ref[...]).astype(BF16)
    nt = (((1,), (1,)), ((), ()))
    o_ref[...] = lax.dot_general(wt_ref[...], xn, nt,
                                 preferred_element_type=F32).astype(o_ref.dtype)


def _norm_matmul_t(x, g, wt, *, tm):
    n, d = x.shape
    nout = wt.shape[0]
    return pl.pallas_call(
        _norm_matmul_t_kernel,
        grid=(n // tm,),
        in_specs=[pl.BlockSpec((tm, d), lambda i: (i, 0)),
                  pl.BlockSpec((1, d), lambda i: (0, 0)),
                  pl.BlockSpec((nout, d), lambda i: (0, 0))],
        out_specs=pl.BlockSpec((None, nout, tm), lambda i: (i, 0, 0)),
        out_shape=jax.ShapeDtypeStruct((n // tm, nout, tm), BF16),
        compiler_params=_cparams(("parallel",)),
        name="norm_matmul_t",
    )(x, g.reshape(1, d), wt)


def _dilated_kernel(q_ref, kp_ref, kc_ref, vp_ref, vc_ref, bias_ref, o_ref, lse_ref):
    first = (pl.program_id(2) == 0).astype(jnp.int32)
    lane = lax.broadcasted_iota(jnp.int32, (BLOCK, LANES), 1)
    lo = lane < HEAD_DIM
    lse_tile = jnp.zeros((BLOCK, LANES), F32)
    nt = (((1,), (1,)), ((), ()))
    for hp in range(N_HEADS_A // 2):
        sl = slice(hp * LANES, (hp + 1) * LANES)
        qf = q_ref[:, sl].astype(F32) * QK_SCALE
        k = jnp.concatenate([kp_ref[:, sl], kc_ref[:, sl]], axis=0)
        v = jnp.concatenate([vp_ref[:, sl], vc_ref[:, sl]], axis=0)
        outs = []
        for j in range(2):
            h = 2 * hp + j
            qm = jnp.where(lo if j == 0 else ~lo, qf, 0.0).astype(BF16)
            s = lax.dot_general(qm, k, nt, preferred_element_type=F32)
            s = s + bias_ref[first, h]
            m = jnp.max(s, axis=-1, keepdims=True)
            p = jnp.exp(s - m)
            l = jnp.sum(p, axis=-1, keepdims=True)
            pv = jnp.dot(p.astype(BF16), v, preferred_element_type=F32)
            outs.append(pv / l)
            lse_tile = jnp.where(lane == h, m + jnp.log(l), lse_tile)
        o_ref[:, sl] = jnp.where(lo, outs[0], outs[1]).astype(o_ref.dtype)
    lse_ref[...] = lse_tile


def _dilated_attention(qkv, bias, group, dilation, batch, seq):
    assert seq % (dilation * BLOCK) == 0
    sub_len = seq // dilation
    nb = sub_len // BLOCK
    qkv_v = qkv.reshape(batch, sub_len, dilation * 3 * D_MODEL)

    def col(c):
        return lambda b, r, n: (b, n, r * 3 + c)

    def col_prev(c):
        return lambda b, r, n: (b, jnp.maximum(n - 1, 0), r * 3 + c)

    blk = (None, BLOCK, D_MODEL)
    o, lse = pl.pallas_call(
        _dilated_kernel,
        grid=(batch, dilation, nb),
        in_specs=[pl.BlockSpec(blk, col(0)),
                  pl.BlockSpec(blk, col_prev(1)),
                  pl.BlockSpec(blk, col(1)),
                  pl.BlockSpec(blk, col_prev(2)),
                  pl.BlockSpec(blk, col(2)),
                  pl.BlockSpec((None, 2, N_BIAS_COLS, BLOCK, 2 * BLOCK),
                               lambda b, r, n: (group, 0, 0, 0, 0))],
        out_specs=[pl.BlockSpec(blk, lambda b, r, n: (b, n, r)),
                   pl.BlockSpec((None, BLOCK, LANES), lambda b, r, n: (b, n, r))],
        out_shape=[jax.ShapeDtypeStruct((batch, sub_len, dilation * D_MODEL), BF16),
                   jax.ShapeDtypeStruct((batch, sub_len, dilation * LANES), F32)],
        compiler_params=_cparams(("parallel", "parallel", "arbitrary")),
        name=f"dilated_attn_d{dilation}",
    )(qkv_v, qkv_v, qkv_v, qkv_v, qkv_v, bias)
    return o.reshape(batch * seq, D_MODEL), lse.reshape(batch * seq, LANES)


def _combine_proj_kernel(h_ref, o0_ref, o1_ref, o2_ref, l0_ref, l1_ref, l2_ref,
                         e_ref, w_ref, out_ref):
    lses = [l0_ref[...], l1_ref[...], l2_ref[...]]
    outs = [o0_ref, o1_ref, o2_ref]
    mx = jnp.maximum(jnp.maximum(lses[0], lses[1]), lses[2])
    ws = [jnp.exp(l - mx) for l in lses]
    tot = ws[0] + ws[1] + ws[2]
    e = e_ref[...]
    o = None
    for g in range(N_GROUPS):
        a = ws[g] / tot
        a_hi = a.astype(BF16)
        a_lo = (a - a_hi.astype(F32)).astype(BF16)
        ae = (jnp.dot(a_hi, e, preferred_element_type=F32)
              + jnp.dot(a_lo, e, preferred_element_type=F32))
        term = ae * outs[g][...].astype(F32)
        o = term if o is None else o + term
    out_ref[...] = h_ref[...] + jnp.dot(o.astype(BF16), w_ref[...], preferred_element_type=F32)


def _combine_proj(h, outs, lses, w, *, tm=512):
    n, d = h.shape
    head_of_col = jnp.arange(d, dtype=jnp.int32) // HEAD_DIM
    expand = (jnp.arange(LANES, dtype=jnp.int32)[:, None] == head_of_col[None, :]).astype(BF16)
    row = lambda i: (i, 0)
    full = lambda i: (0, 0)
    return pl.pallas_call(
        _combine_proj_kernel,
        grid=(n // tm,),
        in_specs=[pl.BlockSpec((tm, d), row)] + [pl.BlockSpec((tm, d), row)] * 3
                 + [pl.BlockSpec((tm, LANES), row)] * 3
                 + [pl.BlockSpec((LANES, d), full), pl.BlockSpec((d, d), full)],
        out_specs=pl.BlockSpec((tm, d), row),
        out_shape=jax.ShapeDtypeStruct((n, d), F32),
        compiler_params=_cparams(("parallel",)),
        name="combine_proj",
    )(h, *outs, *lses, expand, w)


def _proj_residual_kernel(h_ref, x_ref, w_ref, out_ref):
    out_ref[...] = h_ref[...] + jnp.dot(x_ref[...], w_ref[...], preferred_element_type=F32)


def _proj_residual(h, x, w, *, tm=1024):
    n, d = h.shape
    return pl.pallas_call(
        _proj_residual_kernel,
        grid=(n // tm,),
        in_specs=[pl.BlockSpec((tm, d), lambda i: (i, 0)),
                  pl.BlockSpec((tm, x.shape[1]), lambda i: (i, 0)),
                  pl.BlockSpec(w.shape, lambda i: (0, 0))],
        out_specs=pl.BlockSpec((tm, d), lambda i: (i, 0)),
        out_shape=jax.ShapeDtypeStruct((n, d), F32),
        compiler_params=_cparams(("parallel",)),
        name="proj_residual",
    )(h, x, w)


def _mlp_kernel(h_ref, g_ref, w1_ref, w2_ref, out_ref, xn_ref, acc_ref):
    f = pl.program_id(1)

    @pl.when(f == 0)
    def _():
        xn_ref[...] = _rmsnorm_f32(h_ref[...], g_ref[...]).astype(BF16)
        acc_ref[...] = jnp.zeros_like(acc_ref)

    a = jnp.dot(xn_ref[...], w1_ref[...], preferred_element_type=F32)
    a = jnp.maximum(a, 0.0)
    a = (a * a).astype(BF16)
    acc_ref[...] += jnp.dot(a, w2_ref[...], preferred_element_type=F32)

    @pl.when(f == pl.num_programs(1) - 1)
    def _():
        out_ref[...] = h_ref[...] + acc_ref[...]


def _mlp(h, g, w1, w2, *, tm=1024, tf=512):
    n, d = h.shape
    dff = w1.shape[1]
    return pl.pallas_call(
        _mlp_kernel,
        grid=(n // tm, dff // tf),
        in_specs=[pl.BlockSpec((tm, d), lambda i, f: (i, 0)),
                  pl.BlockSpec((1, d), lambda i, f: (0, 0)),
                  pl.BlockSpec((d, tf), lambda i, f: (0, f)),
                  pl.BlockSpec((tf, d), lambda i, f: (f, 0))],
        out_specs=pl.BlockSpec((tm, d), lambda i, f: (i, 0)),
        out_shape=jax.ShapeDtypeStruct((n, d), F32),
        scratch_shapes=[pltpu.VMEM((tm, d), BF16), pltpu.VMEM((tm, d), F32)],
        compiler_params=_cparams(("parallel", "arbitrary")),
        name="mlp",
    )(h, g.reshape(1, d), w1, w2)


def _ple_kernel(h_ref, p_ref, g_ref, wg_ref, wp_ref, fg_ref, out_ref, *, final_norm):
    x = h_ref[...]
    xn = _rmsnorm_f32(x, g_ref[...]).astype(BF16)
    gate = jax.nn.sigmoid(jnp.dot(xn, wg_ref[...], preferred_element_type=F32))
    proj = jnp.dot(p_ref[...].astype(BF16), wp_ref[...], preferred_element_type=F32)
    y = x + gate * proj
    if final_norm:
        y = _rmsnorm_f32(y, fg_ref[...])
    out_ref[...] = y


def _ple(h, p_all, layer, g, wg, wp, fg, *, final_norm, tm=512):
    n, d = h.shape
    pd = p_all.shape[-1]
    return pl.pallas_call(
        functools.partial(_ple_kernel, final_norm=final_norm),
        grid=(n // tm,),
        in_specs=[pl.BlockSpec((tm, d), lambda i: (i, 0)),
                  pl.BlockSpec((None, tm, pd), lambda i: (layer, i, 0)),
                  pl.BlockSpec((1, d), lambda i: (0, 0)),
                  pl.BlockSpec((d, d), lambda i: (0, 0)),
                  pl.BlockSpec((pd, d), lambda i: (0, 0)),
                  pl.BlockSpec((1, d), lambda i: (0, 0))],
        out_specs=pl.BlockSpec((tm, d), lambda i: (i, 0)),
        out_shape=jax.ShapeDtypeStruct((n, d), F32),
        compiler_params=_cparams(("parallel",)),
        name="ple",
    )(h, p_all, g.reshape(1, d), wg, wp, fg.reshape(1, d))


def _diff_attn_kernel(lq1_ref, lk1_ref, lq2_ref, lk2_ref, q_ref, k_ref, vt_ref, u_ref, sg_ref,
                      o_ref, m_sc, acc_sc, sa_sc, samax_sc, sb_sc, sbmax_sc, *, bq, bk, lambda_init):
    assert bq % (2 * bk) == 0
    q0 = pl.program_id(2) * bq
    nt = (((1,), (1,)), ((), ()))
    lane = lax.broadcasted_iota(jnp.int32, (bq, LANES), 1)
    qf = q_ref[...].astype(F32)
    q2 = jnp.concatenate([jnp.where(lane < HEAD_DIM, qf, 0.0),
                          jnp.where(lane >= HEAD_DIM, qf, 0.0)], axis=0).astype(BF16)

    m_sc[...] = jnp.full(m_sc.shape, NEG, F32)
    acc_sc[...] = jnp.zeros(acc_sc.shape, F32)

    qb = bq // LANES
    n_trips = (q0 + bq) // (2 * bk)
    ones = jnp.ones((SUM_ROWS, bk), BF16)

    def scores(c, s_ref, smax_ref):
        k0 = pl.multiple_of(c * bk, bk)
        s = lax.dot_general(k_ref[pl.ds(k0, bk), :], q2, nt, preferred_element_type=F32)
        rows = []
        for jb in range(bk // LANES):
            tiles = []
            for ib in range(2 * qb):
                col = ib // qb
                i0 = (ib % qb) * LANES
                t = lax.shift_right_arithmetic(q0 + i0 - k0 - jb * LANES, LOG2_LANES)
                t = jnp.where(t < 0, DIFF_MASK_TILE, jnp.minimum(t, DIFF_CONST_TILE))
                tiles.append(u_ref[t, col])
            rows.append(jnp.concatenate(tiles, axis=1))
        s = s + jnp.concatenate(rows, axis=0)
        s_ref[...] = s
        smax_ref[...] = jnp.max(s, axis=0, keepdims=True)

    def accumulate(c, s_ref, smax_ref):
        m_prev = m_sc[...]
        m_new = jnp.maximum(m_prev, smax_ref[...])
        m_sc[...] = m_new
        p = jnp.exp2(s_ref[...] - m_new).astype(BF16)
        v1 = jnp.concatenate([vt_ref[c], ones], axis=0)
        acc_sc[...] = (jnp.exp2(m_prev - m_new) * acc_sc[...]
                       + jnp.dot(v1, p, preferred_element_type=F32))

    def trip(i, with_next):
        scores(2 * i + 1, sb_sc, sbmax_sc)
        accumulate(2 * i, sa_sc, samax_sc)
        if with_next:
            scores(2 * i + 2, sa_sc, samax_sc)
        accumulate(2 * i + 1, sb_sc, sbmax_sc)

    scores(0, sa_sc, samax_sc)

    def body(i, carry):
        trip(i, True)
        return carry

    lax.fori_loop(0, n_trips - 1, body, 0)
    trip(n_trips - 1, False)

    lam = (jnp.exp(jnp.sum(lq1_ref[...] * lk1_ref[...], keepdims=True))
           - jnp.exp(jnp.sum(lq2_ref[...] * lk2_ref[...], keepdims=True)) + lambda_init)
    accl = acc_sc[...]
    acc = accl[:LANES] / accl[LANES:LANES + 1]
    o = (acc[:, :bq] - lam * acc[:, bq:]).T
    y = _rmsnorm_f32(o, sg_ref[...]) * (1.0 - lambda_init)
    o_ref[...] = y.astype(o_ref.dtype)


def _diff_attention(qk, vt, u, lq1, lk1, lq2, lk2, subln, lambda_init, batch, seq, *, bq, bk):
    nh = N_HEADS_B
    nkt = seq // bk
    vec = lambda a: a.reshape(1, -1)
    small = pl.BlockSpec((1, HEAD_DIM), lambda b, h, i: (0, 0))
    return pl.pallas_call(
        functools.partial(_diff_attn_kernel, bq=bq, bk=bk, lambda_init=lambda_init),
        grid=(batch, nh, seq // bq),
        in_specs=[small, small, small, small,
                  pl.BlockSpec((None, bq, LANES), lambda b, h, i: (b, i, h)),
                  pl.BlockSpec((None, seq, LANES), lambda b, h, i: (b, 0, nh + h)),
                  pl.BlockSpec((nkt, LANES, bk), lambda b, h, i: (b, h, 0)),
                  pl.BlockSpec((DIFF_N_TILES, 2, LANES, LANES), lambda b, h, i: (0, h, 0, 0)),
                  pl.BlockSpec((1, LANES), lambda b, h, i: (0, 0))],
        out_specs=pl.BlockSpec((None, bq, LANES), lambda b, h, i: (b, i, h)),
        out_shape=jax.ShapeDtypeStruct((batch, seq, D_MODEL), BF16),
        scratch_shapes=[pltpu.VMEM((1, 2 * bq), F32),
                        pltpu.VMEM((LANES + SUM_ROWS, 2 * bq), F32),
                        pltpu.VMEM((bk, 2 * bq), F32),
                        pltpu.VMEM((1, 2 * bq), F32),
                        pltpu.VMEM((bk, 2 * bq), F32),
                        pltpu.VMEM((1, 2 * bq), F32)],
        compiler_params=_cparams(("parallel", "parallel", "arbitrary")),
        name="diff_attn",
    )(vec(lq1), vec(lk1), vec(lq2), vec(lk2), qk, qk, vt, u, vec(subln))


def kernel(x, p, rel_bias, a_w_qkv, a_w_o, b_w_qkv, b_w_o, b_lambda_q1, b_lambda_k1, b_lambda_q2, b_lambda_k2, b_subln, norm_mix, norm_mlp, w_ff1, w_ff2, norm_ple, w_ple_gate, w_ple_proj, final_norm):
    batch, seq, d = x.shape
    depth = p.shape[0]
    n = batch * seq
    h = x.reshape(n, d)
    p_all = p.reshape(depth, n, p.shape[-1])
    n_mixers = 2

    for i in range(depth):
        j = i // n_mixers
        if i % n_mixers == 0:
            w_a = a_w_qkv[j].astype(BF16)
            bias = _build_dilated_bias(rel_bias)
            outs, lses = [], []
            for g, (_, dilation) in enumerate(DIL_CONFIGS):
                w_g = w_a[:, g * 3 * d:(g + 1) * 3 * d]
                if dilation == 1:
                    qkv = _norm_matmul(h, norm_mix[i], w_g)
                else:
                    qkv = _norm_matmul_dilated(h, norm_mix[i], w_g, dilation)
                o_g, lse_g = _dilated_attention(qkv, bias, g, dilation, batch, seq)
                outs.append(o_g)
                lses.append(lse_g)
            h = _combine_proj(h, outs, lses, a_w_o[j].astype(BF16))
        else:
            lambda_init = 0.8 - 0.6 * math.exp(-0.3 * i)
            bq, bk = 1024, 512
            w_b = b_w_qkv[j].astype(BF16)
            qk = _norm_matmul(h, norm_mix[i], w_b[:, :2 * d], tile_scales=(QK_SCALE * LOG2E, 1.0))
            vt = _norm_matmul_t(h, norm_mix[i], w_b[:, 2 * d:].T, tm=bk)
            u = _build_diff_bias(rel_bias)
            o = _diff_attention(qk.reshape(batch, seq, 2 * d), vt, u,
                                b_lambda_q1[j], b_lambda_k1[j], b_lambda_q2[j], b_lambda_k2[j],
                                b_subln[j], lambda_init, batch, seq, bq=bq, bk=bk)
            h = _proj_residual(h, o.reshape(n, d), b_w_o[j].astype(BF16))
        h = _mlp(h, norm_mlp[i], w_ff1[i].astype(BF16), w_ff2[i].astype(BF16))
        h = _ple(h, p_all, i, norm_ple[i], w_ple_gate[i].astype(BF16), w_ple_proj[i].astype(BF16),
                 final_norm, final_norm=(i == depth - 1))
    return h.reshape(batch, seq, d)
```

```python
import functools
import math

import jax
import jax.numpy as jnp
from jax import lax
from jax.experimental import pallas as pl
from jax.experimental.pallas import tpu as pltpu

F32 = jnp.float32
BF16 = jnp.bfloat16

D_MODEL = 1024
HEAD_DIM = 64
BLOCK = 128
DIL_QB = 4
DIL_CONFIGS = ((128, 1), (512, 4), (2048, 16))
N_GROUPS = len(DIL_CONFIGS)
N_HEADS_A = D_MODEL // HEAD_DIM
N_HEADS_B = D_MODEL // (2 * HEAD_DIM)
N_BUCKETS = 32
MAX_DISTANCE = 2048
N_BIAS_COLS = 16
EPS = 1e-6
NEG = -1e30
LANES = 128
LOG2_LANES = 7
QK_SCALE = HEAD_DIM ** -0.5
LOG2E = math.log2(math.e)
LN2 = math.log(2.0)

DIFF_CONST_TILE = (MAX_DISTANCE + LANES - 1) // LANES + 1
DIFF_MASK_TILE = DIFF_CONST_TILE + 1
DIFF_N_TILES = DIFF_MASK_TILE + 1
SUM_ROWS = 16

VMEM_LIMIT = 48 * 1024 * 1024


def _cparams(sem):
    return pltpu.CompilerParams(dimension_semantics=sem, vmem_limit_bytes=VMEM_LIMIT)


def _rmsnorm_f32(x, g):
    ms = jnp.mean(x * x, axis=-1, keepdims=True)
    return x * lax.rsqrt(ms + EPS) * g


def _rel_bucket(dist):
    n = jnp.maximum(dist, 0)
    max_exact = N_BUCKETS // 2
    nf = jnp.maximum(n, 1).astype(F32)
    large = max_exact + (jnp.log(nf / max_exact) / math.log(MAX_DISTANCE / max_exact)
                         * (N_BUCKETS - max_exact)).astype(jnp.int32)
    large = jnp.minimum(large, N_BUCKETS - 1)
    return jnp.where(n < max_exact, n, large)


def _table_lookup(bucket, tab_ref, col):
    acc = jnp.zeros(bucket.shape, F32)
    for b in range(N_BUCKETS):
        acc = jnp.where(bucket == b, tab_ref[b, col], acc)
    return acc


def _dilated_bias_kernel(tab_ref, o_ref):
    g = pl.program_id(0)
    first = pl.program_id(1)
    dilation = jnp.where(g == 0, DIL_CONFIGS[0][1],
                         jnp.where(g == 1, DIL_CONFIGS[1][1], DIL_CONFIGS[2][1]))
    kj = lax.broadcasted_iota(jnp.int32, (2 * BLOCK, BLOCK), 0)
    qi = lax.broadcasted_iota(jnp.int32, (2 * BLOCK, BLOCK), 1)
    sub = qi + BLOCK - kj
    valid = (sub >= 0) & (sub <= BLOCK) & ((first == 0) | (kj >= BLOCK))
    bucket = _rel_bucket(sub * dilation)
    for c in range(N_BIAS_COLS):
        o_ref[c // 2, :, (c % 2) * BLOCK:(c % 2 + 1) * BLOCK] = jnp.where(
            valid, _table_lookup(bucket, tab_ref, c) * LOG2E, NEG)


def _build_dilated_bias(rel_bias):
    return pl.pallas_call(
        _dilated_bias_kernel,
        grid=(N_GROUPS, 2),
        in_specs=[pl.BlockSpec(memory_space=pltpu.SMEM)],
        out_specs=pl.BlockSpec((None, None, N_HEADS_A // 2, 2 * BLOCK, 2 * BLOCK),
                               lambda g, f: (g, f, 0, 0, 0)),
        out_shape=jax.ShapeDtypeStruct((N_GROUPS, 2, N_HEADS_A // 2, 2 * BLOCK, 2 * BLOCK), F32),
        compiler_params=_cparams(("arbitrary", "arbitrary")),
        name="dilated_bias",
    )(rel_bias)


def _diff_bias_kernel(tab_ref, o_ref):
    t = pl.program_id(0)
    kj = lax.broadcasted_iota(jnp.int32, (LANES, LANES), 0)
    qi = lax.broadcasted_iota(jnp.int32, (LANES, LANES), 1)
    dist = t * LANES + qi - kj
    masked = (dist < 0) | (t == DIFF_MASK_TILE)
    bucket = _rel_bucket(dist)
    for c in range(N_BIAS_COLS):
        o_ref[c] = jnp.where(masked, NEG, _table_lookup(bucket, tab_ref, c) * LOG2E)


def _build_diff_bias(rel_bias):
    return pl.pallas_call(
        _diff_bias_kernel,
        grid=(DIFF_N_TILES,),
        in_specs=[pl.BlockSpec(memory_space=pltpu.SMEM)],
        out_specs=pl.BlockSpec((None, N_BIAS_COLS, LANES, LANES), lambda t: (t, 0, 0, 0)),
        out_shape=jax.ShapeDtypeStruct((DIFF_N_TILES, N_BIAS_COLS, LANES, LANES), F32),
        compiler_params=_cparams(("arbitrary",)),
        name="diff_bias",
    )(rel_bias)


def _norm_matmul_kernel(x_ref, g_ref, w_ref, o_ref, xn_ref, *, tile_scales):
    j = pl.program_id(1)

    @pl.when(j == 0)
    def _():
        xn_ref[...] = _rmsnorm_f32(x_ref[...], g_ref[...]).astype(BF16)

    y = jnp.dot(xn_ref[...], w_ref[...], preferred_element_type=F32)
    if tile_scales is not None:
        scale = jnp.float32(tile_scales[-1])
        for t in range(len(tile_scales) - 2, -1, -1):
            scale = jnp.where(j == t, jnp.float32(tile_scales[t]), scale)
        y = y * scale
    o_ref[...] = y.astype(o_ref.dtype)


def _norm_matmul(x, g, w, *, tm=1024, tn=1024, tile_scales=None):
    n, d = x.shape
    nout = w.shape[1]
    assert tile_scales is None or len(tile_scales) == nout // tn
    return pl.pallas_call(
        functools.partial(_norm_matmul_kernel, tile_scales=tile_scales),
        grid=(n // tm, nout // tn),
        in_specs=[pl.BlockSpec((tm, d), lambda i, j: (i, 0)),
                  pl.BlockSpec((1, d), lambda i, j: (0, 0)),
                  pl.BlockSpec((d, tn), lambda i, j: (0, j))],
        out_specs=pl.BlockSpec((tm, tn), lambda i, j: (i, j)),
        out_shape=jax.ShapeDtypeStruct((n, nout), BF16),
        scratch_shapes=[pltpu.VMEM((tm, d), BF16)],
        compiler_params=_cparams(("parallel", "arbitrary")),
        name="norm_matmul",
    )(x, g.reshape(1, d), w)


def _qkv_dilated_kernel(*refs, dilation, n_slabs):
    x_refs = refs[:n_slabs]
    g_ref, w_ref, o_ref, xn_sc = refs[n_slabs:]
    j = pl.program_id(1)
    tm, slab_w = x_refs[0].shape
    d_model = n_slabs * slab_w
    rows = tm // dilation

    @pl.when(j == 0)
    def _():
        for r in range(dilation):
            if dilation == 1:
                xs = [x_ref[...] for x_ref in x_refs]
            else:
                xs = [x_ref[pl.ds(r, rows, stride=dilation), :] for x_ref in x_refs]
            ssq = jnp.sum(xs[0] * xs[0], axis=-1, keepdims=True)
            for x in xs[1:]:
                ssq = ssq + jnp.sum(x * x, axis=-1, keepdims=True)
            scale = lax.rsqrt(ssq * (1.0 / d_model) + EPS)
            for s, x in enumerate(xs):
                cols = slice(s * slab_w, (s + 1) * slab_w)
                xn_sc[r * rows:(r + 1) * rows, cols] = (x * scale * g_ref[:, cols]).astype(BF16)

    y = jnp.dot(xn_sc[...], w_ref[...], preferred_element_type=F32)

    @pl.when(j < 2)
    def _():
        scale = jnp.where(j == 0, jnp.float32(QK_SCALE * LOG2E), jnp.float32(1.0))
        for r in range(dilation):
            o_ref[:, r * d_model:(r + 1) * d_model] = (
                y[r * rows:(r + 1) * rows] * scale).astype(o_ref.dtype)

    @pl.when(j == 2)
    def _():
        for r in range(dilation):
            for nb in range(rows // BLOCK):
                for hp in range(d_model // LANES):
                    tile = y[r * rows + nb * BLOCK:r * rows + (nb + 1) * BLOCK,
                             hp * LANES:(hp + 1) * LANES]
                    o_ref[nb * BLOCK:(nb + 1) * BLOCK,
                          r * d_model + hp * LANES:r * d_model + (hp + 1) * LANES] = (
                              tile.T.astype(o_ref.dtype))


def _qkv_dilated(x, g, w, dilation, *, tm):
    n, d = x.shape
    assert w.shape == (d, 3 * d)
    rows = tm // dilation
    assert tm % dilation == 0 and rows % BLOCK == 0
    n_slabs = 1 if dilation == 1 else d // LANES
    slab_w = d // n_slabs
    x_specs = [pl.BlockSpec((tm, slab_w), functools.partial(lambda i, j, s: (i, s), s=s))
               for s in range(n_slabs)]
    return pl.pallas_call(
        functools.partial(_qkv_dilated_kernel, dilation=dilation, n_slabs=n_slabs),
        grid=(n // tm, 3),
        in_specs=x_specs + [pl.BlockSpec((1, d), lambda i, j: (0, 0)),
                            pl.BlockSpec((d, d), lambda i, j: (0, j))],
        out_specs=pl.BlockSpec((rows, dilation * d), lambda i, j: (i, j)),
        out_shape=jax.ShapeDtypeStruct((n // dilation, 3 * dilation * d), BF16),
        scratch_shapes=[pltpu.VMEM((tm, d), BF16)],
        compiler_params=_cparams(("parallel", "arbitrary")),
        name=f"qkv_d{dilation}",
    )(*([x] * n_slabs), g.reshape(1, d), w)


def _norm_matmul_t_kernel(x_ref, g_ref, wt_ref, o_ref):
    xn = _rmsnorm_f32(x_ref[...], g_ref[...]).astype(BF16)
    nt = (((1,), (1,)), ((), ()))
    o_ref[...] = lax.dot_general(wt_ref[...], xn, nt,
                                 preferred_element_type=F32).astype(o_ref.dtype)


def _norm_matmul_t(x, g, wt, *, tm):
    n, d = x.shape
    nout = wt.shape[0]
    return pl.pallas_call(
        _norm_matmul_t_kernel,
        grid=(n // tm,),
        in_specs=[pl.BlockSpec((tm, d), lambda i: (i, 0)),
                  pl.BlockSpec((1, d), lambda i: (0, 0)),
                  pl.BlockSpec((nout, d), lambda i: (0, 0))],
        out_specs=pl.BlockSpec((None, nout, tm), lambda i: (i, 0, 0)),
        out_shape=jax.ShapeDtypeStruct((n // tm, nout, tm), BF16),
        compiler_params=_cparams(("parallel",)),
        name="norm_matmul_t",
    )(x, g.reshape(1, d), wt)


def _dilated_kernel(q_ref, kp_ref, kc_ref, vtp_ref, vtc_ref, bias_ref, o_ref, lse_ref,
                    s_sc, m_sc, pv_sc):
    first_step = (pl.program_id(2) == 0).astype(jnp.int32)
    lane = lax.broadcasted_iota(jnp.int32, (BLOCK, LANES), 1)
    row = lax.broadcasted_iota(jnp.int32, (BLOCK, LANES), 0)
    head_row = lax.broadcasted_iota(jnp.int32, (N_HEADS_A, BLOCK), 0)
    lo_row = row < HEAD_DIM
    mask_lo = jnp.where(lane < HEAD_DIM, 1.0, 0.0).astype(BF16)
    mask_hi = jnp.where(lane < HEAD_DIM, 0.0, 1.0).astype(BF16)
    nt = (((1,), (1,)), ((), ()))
    ones = jnp.ones((SUM_ROWS, 2 * BLOCK), BF16)
    n_pairs = N_HEADS_A // 2
    items = [(qb, hp) for qb in range(DIL_QB) for hp in range(n_pairs)]

    def rows(qb):
        return slice(qb * BLOCK, (qb + 1) * BLOCK)

    def prev_cur(prev_ref, cur_ref, qb, sl):
        prev = prev_ref[:, sl] if qb == 0 else cur_ref[rows(qb - 1), sl]
        return prev, cur_ref[rows(qb), sl]

    for it, (qb, hp) in enumerate(items):
        sl = slice(hp * LANES, (hp + 1) * LANES)
        q = q_ref[rows(qb), sl]
        q2 = jnp.concatenate([q * mask_lo, q * mask_hi], axis=0)
        k = jnp.concatenate(prev_cur(kp_ref, kc_ref, qb, sl), axis=0)
        s = lax.dot_general(k, q2, nt, preferred_element_type=F32)
        s = s + bias_ref[first_step if qb == 0 else 0, hp]
        s_sc[it] = s
        m_sc[it] = jnp.max(s, axis=0, keepdims=True)
    for it, (qb, hp) in enumerate(items):
        sl = slice(hp * LANES, (hp + 1) * LANES)
        p = jnp.exp2(s_sc[it] - m_sc[it]).astype(BF16)
        vt = jnp.concatenate(prev_cur(vtp_ref, vtc_ref, qb, sl), axis=1)
        pv_sc[it] = jnp.dot(jnp.concatenate([vt, ones], axis=0), p,
                            preferred_element_type=F32)
    for qb in range(DIL_QB):
        lse_t = jnp.zeros((N_HEADS_A, BLOCK), F32)
        for hp in range(n_pairs):
            it = qb * n_pairs + hp
            l = pv_sc[it, LANES:LANES + 1]
            acc = pv_sc[it, :LANES] / l
            o_t = jnp.where(lo_row, acc[:, :BLOCK], acc[:, BLOCK:])
            o_ref[rows(qb), hp * LANES:(hp + 1) * LANES] = o_t.T.astype(o_ref.dtype)
            lse = (m_sc[it] + jnp.log2(l)) * LN2
            lse_t = jnp.where(head_row == 2 * hp, lse[:, :BLOCK], lse_t)
            lse_t = jnp.where(head_row == 2 * hp + 1, lse[:, BLOCK:], lse_t)
        lse_full = jnp.concatenate([lse_t, jnp.zeros((BLOCK - N_HEADS_A, BLOCK), F32)], axis=0)
        lse_ref[rows(qb), :] = lse_full.T


def _dilated_attention(qkv, bias, group, dilation, batch, seq):
    assert seq % (dilation * BLOCK * DIL_QB) == 0
    sub_len = seq // dilation
    steps = sub_len // (BLOCK * DIL_QB)
    qkv_v = qkv.reshape(batch, sub_len, 3 * dilation * D_MODEL)

    def col(c):
        return lambda b, r, n: (b, n, c * dilation + r)

    def col_prev(c):
        return lambda b, r, n: (b, jnp.maximum(n * DIL_QB - 1, 0), c * dilation + r)

    blk = (None, DIL_QB * BLOCK, D_MODEL)
    blk_prev = (None, BLOCK, D_MODEL)
    n_items = DIL_QB * N_HEADS_A // 2
    o, lse = pl.pallas_call(
        _dilated_kernel,
        grid=(batch, dilation, steps),
        in_specs=[pl.BlockSpec(blk, col(0)),
                  pl.BlockSpec(blk_prev, col_prev(1)),
                  pl.BlockSpec(blk, col(1)),
                  pl.BlockSpec(blk_prev, col_prev(2)),
                  pl.BlockSpec(blk, col(2)),
                  pl.BlockSpec((None, 2, N_HEADS_A // 2, 2 * BLOCK, 2 * BLOCK),
                               lambda b, r, n: (group, 0, 0, 0, 0))],
        out_specs=[pl.BlockSpec(blk, lambda b, r, n: (b, n, r)),
                   pl.BlockSpec((None, DIL_QB * BLOCK, LANES), lambda b, r, n: (b, n, r))],
        out_shape=[jax.ShapeDtypeStruct((batch, sub_len, dilation * D_MODEL), BF16),
                   jax.ShapeDtypeStruct((batch, sub_len, dilation * LANES), F32)],
        scratch_shapes=[pltpu.VMEM((n_items, 2 * BLOCK, 2 * BLOCK), F32),
                        pltpu.VMEM((n_items, 1, 2 * BLOCK), F32),
                        pltpu.VMEM((n_items, LANES + SUM_ROWS, 2 * BLOCK), F32)],
        compiler_params=_cparams(("parallel", "parallel", "arbitrary")),
        name=f"dilated_attn_d{dilation}",
    )(qkv_v, qkv_v, qkv_v, qkv_v, qkv_v, bias)
    return (o.reshape(batch * sub_len, dilation * D_MODEL),
            lse.reshape(batch * sub_len, dilation * LANES))


def _combine_proj_kernel(h_ref, o0_ref, o1_ref, o2_ref, l0_ref, l1_ref, l2_ref,
                         e_ref, w_ref, out_ref, lse_sc, o_sc):
    o_refs = [o0_ref, o1_ref, o2_ref]
    l_refs = [l0_ref, l1_ref, l2_ref]
    tm, d = h_ref.shape
    n_slabs = d // LANES
    for g, (_, dilation) in enumerate(DIL_CONFIGS):
        rows = tm // dilation
        for r in range(dilation):
            dst = pl.ds(r, rows, stride=dilation) if dilation > 1 else slice(None)
            lse_sc[g, dst, :] = l_refs[g][:, r * LANES:(r + 1) * LANES]
            for c in range(n_slabs):
                o_sc[g, c, dst, :] = o_refs[g][:, r * d + c * LANES:r * d + (c + 1) * LANES].astype(F32)
    lses = [lse_sc[g] for g in range(N_GROUPS)]
    mx = jnp.maximum(jnp.maximum(lses[0], lses[1]), lses[2])
    ws = [jnp.exp(l - mx) for l in lses]
    tot = ws[0] + ws[1] + ws[2]
    e = e_ref[...]
    aes = []
    for g in range(N_GROUPS):
        a = ws[g] / tot
        a_hi = a.astype(BF16)
        a_lo = (a - a_hi.astype(F32)).astype(BF16)
        aes.append(jnp.dot(a_hi, e, preferred_element_type=F32)
                   + jnp.dot(a_lo, e, preferred_element_type=F32))
    slabs = []
    for c in range(n_slabs):
        cols = slice(c * LANES, (c + 1) * LANES)
        slabs.append(aes[0][:, cols] * o_sc[0, c] + aes[1][:, cols] * o_sc[1, c]
                     + aes[2][:, cols] * o_sc[2, c])
    o = jnp.concatenate(slabs, axis=1).astype(BF16)
    out_ref[...] = h_ref[...] + jnp.dot(o, w_ref[...], preferred_element_type=F32)


def _combine_proj(h, outs, lses, w, *, tm=512):
    n, d = h.shape
    head_of_col = jnp.arange(d, dtype=jnp.int32) // HEAD_DIM
    expand = (jnp.arange(LANES, dtype=jnp.int32)[:, None] == head_of_col[None, :]).astype(BF16)
    row = lambda i: (i, 0)
    full = lambda i: (0, 0)
    dils = [dilation for _, dilation in DIL_CONFIGS]
    assert all(tm % (16 * dilation) == 0 for dilation in dils)
    return pl.pallas_call(
        _combine_proj_kernel,
        grid=(n // tm,),
        in_specs=[pl.BlockSpec((tm, d), row)]
                 + [pl.BlockSpec((tm // dilation, dilation * d), row) for dilation in dils]
                 + [pl.BlockSpec((tm // dilation, dilation * LANES), row) for dilation in dils]
                 + [pl.BlockSpec((LANES, d), full), pl.BlockSpec((d, d), full)],
        out_specs=pl.BlockSpec((tm, d), row),
        out_shape=jax.ShapeDtypeStruct((n, d), F32),
        scratch_shapes=[pltpu.VMEM((N_GROUPS, tm, LANES), F32),
                        pltpu.VMEM((N_GROUPS, d // LANES, tm, LANES), F32)],
        compiler_params=_cparams(("parallel",)),
        name="combine_proj",
    )(h, *outs, *lses, expand, w)


def _proj_residual_kernel(h_ref, x_ref, w_ref, out_ref):
    out_ref[...] = h_ref[...] + jnp.dot(x_ref[...], w_ref[...], preferred_element_type=F32)


def _proj_residual(h, x, w, *, tm=1024):
    n, d = h.shape
    return pl.pallas_call(
        _proj_residual_kernel,
        grid=(n // tm,),
        in_specs=[pl.BlockSpec((tm, d), lambda i: (i, 0)),
                  pl.BlockSpec((tm, x.shape[1]), lambda i: (i, 0)),
                  pl.BlockSpec(w.shape, lambda i: (0, 0))],
        out_specs=pl.BlockSpec((tm, d), lambda i: (i, 0)),
        out_shape=jax.ShapeDtypeStruct((n, d), F32),
        compiler_params=_cparams(("parallel",)),
        name="proj_residual",
    )(h, x, w)


def _mlp_kernel(h_ref, g_ref, w1_ref, w2_ref, out_ref, xn_ref, acc_ref):
    f = pl.program_id(1)

    @pl.when(f == 0)
    def _():
        xn_ref[...] = _rmsnorm_f32(h_ref[...], g_ref[...]).astype(BF16)
        acc_ref[...] = jnp.zeros_like(acc_ref)

    a = jnp.dot(xn_ref[...], w1_ref[...], preferred_element_type=F32)
    a = jnp.maximum(a, 0.0)
    a = (a * a).astype(BF16)
    acc_ref[...] += jnp.dot(a, w2_ref[...], preferred_element_type=F32)

    @pl.when(f == pl.num_programs(1) - 1)
    def _():
        out_ref[...] = h_ref[...] + acc_ref[...]


def _mlp(h, g, w1, w2, *, tm=1024, tf=512):
    n, d = h.shape
    dff = w1.shape[1]
    return pl.pallas_call(
        _mlp_kernel,
        grid=(n // tm, dff // tf),
        in_specs=[pl.BlockSpec((tm, d), lambda i, f: (i, 0)),
                  pl.BlockSpec((1, d), lambda i, f: (0, 0)),
                  pl.BlockSpec((d, tf), lambda i, f: (0, f)),
                  pl.BlockSpec((tf, d), lambda i, f: (f, 0))],
        out_specs=pl.BlockSpec((tm, d), lambda i, f: (i, 0)),
        out_shape=jax.ShapeDtypeStruct((n, d), F32),
        scratch_shapes=[pltpu.VMEM((tm, d), BF16), pltpu.VMEM((tm, d), F32)],
        compiler_params=_cparams(("parallel", "arbitrary")),
        name="mlp",
    )(h, g.reshape(1, d), w1, w2)


def _ple_kernel(h_ref, p_ref, g_ref, wg_ref, wp_ref, fg_ref, out_ref, *, final_norm):
    x = h_ref[...]
    xn = _rmsnorm_f32(x, g_ref[...]).astype(BF16)
    gate = jax.nn.sigmoid(jnp.dot(xn, wg_ref[...], preferred_element_type=F32))
    proj = jnp.dot(p_ref[...].astype(BF16), wp_ref[...], preferred_element_type=F32)
    y = x + gate * proj
    if final_norm:
        y = _rmsnorm_f32(y, fg_ref[...])
    out_ref[...] = y


def _ple(h, p_all, layer, g, wg, wp, fg, *, final_norm, tm=512):
    n, d = h.shape
    pd = p_all.shape[-1]
    return pl.pallas_call(
        functools.partial(_ple_kernel, final_norm=final_norm),
        grid=(n // tm,),
        in_specs=[pl.BlockSpec((tm, d), lambda i: (i, 0)),
                  pl.BlockSpec((None, tm, pd), lambda i: (layer, i, 0)),
                  pl.BlockSpec((1, d), lambda i: (0, 0)),
                  pl.BlockSpec((d, d), lambda i: (0, 0)),
                  pl.BlockSpec((pd, d), lambda i: (0, 0)),
                  pl.BlockSpec((1, d), lambda i: (0, 0))],
        out_specs=pl.BlockSpec((tm, d), lambda i: (i, 0)),
        out_shape=jax.ShapeDtypeStruct((n, d), F32),
        compiler_params=_cparams(("parallel",)),
        name="ple",
    )(h, p_all, g.reshape(1, d), wg, wp, fg.reshape(1, d))


def _diff_attn_kernel(lq1_ref, lk1_ref, lq2_ref, lk2_ref, q_ref, k_ref, vt_ref, u_ref, sg_ref,
                      o_ref, m_sc, acc_sc, sa_sc, samax_sc, sb_sc, sbmax_sc, *, bq, bk, lambda_init):
    assert bq % (2 * bk) == 0
    q0 = pl.program_id(2) * bq
    nt = (((1,), (1,)), ((), ()))
    lane = lax.broadcasted_iota(jnp.int32, (bq, LANES), 1)
    qf = q_ref[...].astype(F32)
    q2 = jnp.concatenate([jnp.where(lane < HEAD_DIM, qf, 0.0),
                          jnp.where(lane >= HEAD_DIM, qf, 0.0)], axis=0).astype(BF16)

    m_sc[...] = jnp.full(m_sc.shape, NEG, F32)
    acc_sc[...] = jnp.zeros(acc_sc.shape, F32)

    qb = bq // LANES
    n_trips = (q0 + bq) // (2 * bk)
    ones = jnp.ones((SUM_ROWS, bk), BF16)

    def scores(c, s_ref, smax_ref):
        k0 = pl.multiple_of(c * bk, bk)
        s = lax.dot_general(k_ref[pl.ds(k0, bk), :], q2, nt, preferred_element_type=F32)
        rows = []
        for jb in range(bk // LANES):
            tiles = []
            for ib in range(2 * qb):
                col = ib // qb
                i0 = (ib % qb) * LANES
                t = lax.shift_right_arithmetic(q0 + i0 - k0 - jb * LANES, LOG2_LANES)
                t = jnp.where(t < 0, DIFF_MASK_TILE, jnp.minimum(t, DIFF_CONST_TILE))
                tiles.append(u_ref[t, col])
            rows.append(jnp.concatenate(tiles, axis=1))
        s = s + jnp.concatenate(rows, axis=0)
        s_ref[...] = s
        smax_ref[...] = jnp.max(s, axis=0, keepdims=True)

    def accumulate(c, s_ref, smax_ref):
        m_prev = m_sc[...]
        m_new = jnp.maximum(m_prev, smax_ref[...])
        m_sc[...] = m_new
        p = jnp.exp2(s_ref[...] - m_new).astype(BF16)
        v1 = jnp.concatenate([vt_ref[c], ones], axis=0)
        acc_sc[...] = (jnp.exp2(m_prev - m_new) * acc_sc[...]
                       + jnp.dot(v1, p, preferred_element_type=F32))

    def trip(i, with_next):
        scores(2 * i + 1, sb_sc, sbmax_sc)
        accumulate(2 * i, sa_sc, samax_sc)
        if with_next:
            scores(2 * i + 2, sa_sc, samax_sc)
        accumulate(2 * i + 1, sb_sc, sbmax_sc)

    scores(0, sa_sc, samax_sc)

    def body(i, carry):
        trip(i, True)
        return carry

    lax.fori_loop(0, n_trips - 1, body, 0)
    trip(n_trips - 1, False)

    lam = (jnp.exp(jnp.sum(lq1_ref[...] * lk1_ref[...], keepdims=True))
           - jnp.exp(jnp.sum(lq2_ref[...] * lk2_ref[...], keepdims=True)) + lambda_init)
    accl = acc_sc[...]
    acc = accl[:LANES] / accl[LANES:LANES + 1]
    o = (acc[:, :bq] - lam * acc[:, bq:]).T
    y = _rmsnorm_f32(o, sg_ref[...]) * (1.0 - lambda_init)
    o_ref[...] = y.astype(o_ref.dtype)


def _diff_attention(qk, vt, u, lq1, lk1, lq2, lk2, subln, lambda_init, batch, seq, *, bq, bk):
    nh = N_HEADS_B
    nkt = seq // bk
    vec = lambda a: a.reshape(1, -1)
    small = pl.BlockSpec((1, HEAD_DIM), lambda b, h, i: (0, 0))
    return pl.pallas_call(
        functools.partial(_diff_attn_kernel, bq=bq, bk=bk, lambda_init=lambda_init),
        grid=(batch, nh, seq // bq),
        in_specs=[small, small, small, small,
                  pl.BlockSpec((None, bq, LANES), lambda b, h, i: (b, i, h)),
                  pl.BlockSpec((None, seq, LANES), lambda b, h, i: (b, 0, nh + h)),
                  pl.BlockSpec((nkt, LANES, bk), lambda b, h, i: (b, h, 0)),
                  pl.BlockSpec((DIFF_N_TILES, 2, LANES, LANES), lambda b, h, i: (0, h, 0, 0)),
                  pl.BlockSpec((1, LANES), lambda b, h, i: (0, 0))],
        out_specs=pl.BlockSpec((None, bq, LANES), lambda b, h, i: (b, i, h)),
        out_shape=jax.ShapeDtypeStruct((batch, seq, D_MODEL), BF16),
        scratch_shapes=[pltpu.VMEM((1, 2 * bq), F32),
                        pltpu.VMEM((LANES + SUM_ROWS, 2 * bq), F32),
                        pltpu.VMEM((bk, 2 * bq), F32),
                        pltpu.VMEM((1, 2 * bq), F32),
                        pltpu.VMEM((bk, 2 * bq), F32),
                        pltpu.VMEM((1, 2 * bq), F32)],
        compiler_params=_cparams(("parallel", "parallel", "arbitrary")),
        name="diff_attn",
    )(vec(lq1), vec(lk1), vec(lq2), vec(lk2), qk, qk, vt, u, vec(subln))


def kernel(x, p, rel_bias, a_w_qkv, a_w_o, b_w_qkv, b_w_o, b_lambda_q1, b_lambda_k1, b_lambda_q2, b_lambda_k2, b_subln, norm_mix, norm_mlp, w_ff1, w_ff2, norm_ple, w_ple_gate, w_ple_proj, final_norm):
    batch, seq, d = x.shape
    depth = p.shape[0]
    n = batch * seq
    h = x.reshape(n, d)
    p_all = p.reshape(depth, n, p.shape[-1])
    n_mixers = 2

    for i in range(depth):
        j = i // n_mixers
        if i % n_mixers == 0:
            w_a = a_w_qkv[j].astype(BF16)
            bias = _build_dilated_bias(rel_bias)
            outs, lses = [], []
            for g, (_, dilation) in enumerate(DIL_CONFIGS):
                w_g = w_a[:, g * 3 * d:(g + 1) * 3 * d]
                qkv = _qkv_dilated(h, norm_mix[i], w_g, dilation, tm=max(512, BLOCK * dilation))
                o_g, lse_g = _dilated_attention(qkv, bias, g, dilation, batch, seq)
                outs.append(o_g)
                lses.append(lse_g)
            h = _combine_proj(h, outs, lses, a_w_o[j].astype(BF16))
        else:
            lambda_init = 0.8 - 0.6 * math.exp(-0.3 * i)
            bq, bk = 1024, 512
            w_b = b_w_qkv[j].astype(BF16)
            qk = _norm_matmul(h, norm_mix[i], w_b[:, :2 * d], tile_scales=(QK_SCALE * LOG2E, 1.0))
            vt = _norm_matmul_t(h, norm_mix[i], w_b[:, 2 * d:].T, tm=bk)
            u = _build_diff_bias(rel_bias)
            o = _diff_attention(qk.reshape(batch, seq, 2 * d), vt, u,
                                b_lambda_q1[j], b_lambda_k1[j], b_lambda_q2[j], b_lambda_k2[j],
                                b_subln[j], lambda_init, batch, seq, bq=bq, bk=bk)
            h = _proj_residual(h, o.reshape(n, d), b_w_o[j].astype(BF16))
        h = _mlp(h, norm_mlp[i], w_ff1[i].astype(BF16), w_ff2[i].astype(BF16))
        h = _ple(h, p_all, i, norm_ple[i], w_ple_gate[i].astype(BF16), w_ple_proj[i].astype(BF16),
                 final_norm, final_norm=(i == depth - 1))
    return h.reshape(batch, seq, d)
```

```python
import functools
import math

import jax
import jax.numpy as jnp
from jax import lax
from jax.experimental import pallas as pl
from jax.experimental.pallas import tpu as pltpu

F32 = jnp.float32
BF16 = jnp.bfloat16

D_MODEL = 1024
HEAD_DIM = 64
BLOCK = 128
DIL_QB = 4
DIL_CONFIGS = ((128, 1), (512, 4), (2048, 16))
N_GROUPS = len(DIL_CONFIGS)
N_HEADS_A = D_MODEL // HEAD_DIM
N_HEADS_B = D_MODEL // (2 * HEAD_DIM)
N_BUCKETS = 32
MAX_DISTANCE = 2048
N_BIAS_COLS = 16
EPS = 1e-6
NEG = -1e30
LANES = 128
SUBLANES = 8
LOG2_LANES = 7
QK_SCALE = HEAD_DIM ** -0.5
LOG2E = math.log2(math.e)
LN2 = math.log(2.0)

DIFF_CONST_TILE = (MAX_DISTANCE + LANES - 1) // LANES + 1
DIFF_MASK_TILE = DIFF_CONST_TILE + 1
DIFF_N_TILES = DIFF_MASK_TILE + 1
SUM_ROWS = 16

VMEM_LIMIT = 48 * 1024 * 1024


def _cparams(sem, flags=None):
    return pltpu.CompilerParams(dimension_semantics=sem, vmem_limit_bytes=VMEM_LIMIT, flags=flags)


def _rmsnorm_f32(x, g):
    ms = jnp.mean(x * x, axis=-1, keepdims=True)
    return x * lax.rsqrt(ms + EPS) * g


def _rel_bucket(dist):
    n = jnp.maximum(dist, 0)
    max_exact = N_BUCKETS // 2
    nf = jnp.maximum(n, 1).astype(F32)
    large = max_exact + (jnp.log(nf / max_exact) / math.log(MAX_DISTANCE / max_exact)
                         * (N_BUCKETS - max_exact)).astype(jnp.int32)
    large = jnp.minimum(large, N_BUCKETS - 1)
    return jnp.where(n < max_exact, n, large)


def _table_lookup(bucket, tab_ref, col):
    acc = jnp.zeros(bucket.shape, F32)
    for b in range(N_BUCKETS):
        acc = jnp.where(bucket == b, tab_ref[b, col], acc)
    return acc


def _dilated_bias_kernel(tab_ref, o_ref):
    g = pl.program_id(0)
    dilation = jnp.where(g == 0, DIL_CONFIGS[0][1],
                         jnp.where(g == 1, DIL_CONFIGS[1][1], DIL_CONFIGS[2][1]))
    kj = lax.broadcasted_iota(jnp.int32, (2 * BLOCK, BLOCK), 0)
    qi = lax.broadcasted_iota(jnp.int32, (2 * BLOCK, BLOCK), 1)
    sub = qi + BLOCK - kj
    band = (sub >= 0) & (sub <= BLOCK)
    band_first = band & (kj >= BLOCK)
    bucket = _rel_bucket(sub * dilation)
    for c in range(N_BIAS_COLS):
        lanes = slice((c % 2) * BLOCK, (c % 2 + 1) * BLOCK)
        bias = _table_lookup(bucket, tab_ref, c) * LOG2E
        o_ref[0, c // 2, :, lanes] = jnp.where(band, bias, NEG)
        o_ref[1, c // 2, :, lanes] = jnp.where(band_first, bias, NEG)


def _build_dilated_bias(rel_bias):
    return pl.pallas_call(
        _dilated_bias_kernel,
        grid=(N_GROUPS,),
        in_specs=[pl.BlockSpec(memory_space=pltpu.SMEM)],
        out_specs=pl.BlockSpec((None, 2, N_HEADS_A // 2, 2 * BLOCK, 2 * BLOCK),
                               lambda g: (g, 0, 0, 0, 0)),
        out_shape=jax.ShapeDtypeStruct((N_GROUPS, 2, N_HEADS_A // 2, 2 * BLOCK, 2 * BLOCK), F32),
        compiler_params=_cparams(("arbitrary",)),
        name="dilated_bias",
    )(rel_bias)


def _diff_bias_kernel(tab_ref, o_ref):
    t = pl.program_id(0)
    kj = lax.broadcasted_iota(jnp.int32, (LANES, LANES), 0)
    qi = lax.broadcasted_iota(jnp.int32, (LANES, LANES), 1)
    dist = t * LANES + qi - kj
    masked = (dist < 0) | (t == DIFF_MASK_TILE)
    bucket = _rel_bucket(dist)
    init = jnp.where(masked, NEG, 0.0)
    for c in range(N_BIAS_COLS):
        o_ref[c] = init
    live = jnp.where(masked, -1, bucket)

    def body(b, carry):
        hit = live == b
        for c in range(N_BIAS_COLS):
            o_ref[c] = jnp.where(hit, tab_ref[b, c] * LOG2E, o_ref[c])
        return carry

    lax.fori_loop(jnp.min(bucket), jnp.max(bucket) + 1, body, 0)


def _build_diff_bias(rel_bias):
    return pl.pallas_call(
        _diff_bias_kernel,
        grid=(DIFF_N_TILES,),
        in_specs=[pl.BlockSpec(memory_space=pltpu.SMEM)],
        out_specs=pl.BlockSpec((None, N_BIAS_COLS, LANES, LANES), lambda t: (t, 0, 0, 0)),
        out_shape=jax.ShapeDtypeStruct((DIFF_N_TILES, N_BIAS_COLS, LANES, LANES), F32),
        compiler_params=_cparams(("arbitrary",)),
        name="diff_bias",
    )(rel_bias)


def _qkv_dilated_kernel(*refs, dilation, n_slabs):
    n_x = len(refs) - 4
    x_refs = refs[:n_x]
    g_ref, w_ref, o_ref, xn_sc = refs[n_x:]
    j = pl.program_id(1)
    tm, d_model = xn_sc.shape
    rows = tm // dilation

    @pl.when(j == 0)
    def _():
        for r in range(dilation):
            if n_slabs is None:
                xs = [x_refs[0][...]]
            else:
                xs = [x_ref[pl.ds(r, rows, stride=dilation), :] for x_ref in x_refs]
            slab_w = d_model // len(xs)
            sq = xs[0] * xs[0]
            for x in xs[1:]:
                sq = sq + x * x
            ssq = jnp.sum(sq, axis=-1, keepdims=True)
            scale = lax.rsqrt(ssq * (1.0 / d_model) + EPS)
            for s, x in enumerate(xs):
                cols = slice(s * slab_w, (s + 1) * slab_w)
                xn_sc[r * rows:(r + 1) * rows, cols] = (x * scale * g_ref[:, cols]).astype(BF16)

    y = jnp.dot(xn_sc[...], w_ref[...], preferred_element_type=F32)

    @pl.when(j < 2)
    def _():
        scale = jnp.where(j == 0, jnp.float32(QK_SCALE * LOG2E), jnp.float32(1.0))
        for r in range(dilation):
            o_ref[:, r * d_model:(r + 1) * d_model] = (
                y[r * rows:(r + 1) * rows] * scale).astype(o_ref.dtype)

    @pl.when(j == 2)
    def _():
        for r in range(dilation):
            for nb in range(rows // BLOCK):
                for hp in range(d_model // LANES):
                    tile = y[r * rows + nb * BLOCK:r * rows + (nb + 1) * BLOCK,
                             hp * LANES:(hp + 1) * LANES]
                    o_ref[nb * BLOCK:(nb + 1) * BLOCK,
                          r * d_model + hp * LANES:r * d_model + (hp + 1) * LANES] = (
                              tile.T.astype(o_ref.dtype))


def _qkv_dilated(x, g, w, dilation, *, tm):
    n, d = x.shape
    assert w.shape == (d, 3 * d)
    rows = tm // dilation
    assert tm % dilation == 0 and rows % BLOCK == 0
    if dilation == 1:
        n_slabs = None
        x_in = [x]
        x_specs = [pl.BlockSpec((tm, d), lambda i, j: (i, 0))]
    else:
        n_slabs = d // LANES
        x_in = [x] * n_slabs
        x_specs = [pl.BlockSpec((tm, LANES), functools.partial(lambda i, j, s: (i, s), s=s))
                   for s in range(n_slabs)]
    return pl.pallas_call(
        functools.partial(_qkv_dilated_kernel, dilation=dilation, n_slabs=n_slabs),
        grid=(n // tm, 3),
        in_specs=x_specs + [pl.BlockSpec((1, d), lambda i, j: (0, 0)),
                            pl.BlockSpec((d, d), lambda i, j: (0, j))],
        out_specs=pl.BlockSpec((rows, dilation * d), lambda i, j: (i, j)),
        out_shape=jax.ShapeDtypeStruct((n // dilation, 3 * dilation * d), BF16),
        scratch_shapes=[pltpu.VMEM((tm, d), BF16)],
        compiler_params=_cparams(("parallel", "arbitrary")),
        name=f"qkv_d{dilation}",
    )(*x_in, g.reshape(1, d), w)


def _qkv_diff_kernel(x_ref, g_ref, w_ref, qk_ref, vt_ref, xn_sc, *, bk):
    j = pl.program_id(1)
    tm = x_ref.shape[0]

    @pl.when(j == 0)
    def _():
        xn_sc[...] = _rmsnorm_f32(x_ref[...], g_ref[...]).astype(BF16)

    @pl.when(j < 2)
    def _():
        y = jnp.dot(xn_sc[...], w_ref[...], preferred_element_type=F32)
        scale = jnp.where(j == 0, jnp.float32(QK_SCALE * LOG2E), jnp.float32(1.0))
        qk_ref[...] = (y * scale).astype(qk_ref.dtype)

    @pl.when(j == 2)
    def _():
        nt = (((1,), (1,)), ((), ()))
        for c in range(tm // bk):
            vt_ref[c] = lax.dot_general(w_ref[...], xn_sc[c * bk:(c + 1) * bk], nt,
                                        preferred_element_type=F32).astype(vt_ref.dtype)


def _qkv_diff(x, g, w_qkv, *, bk, tm=1024):
    n, d = x.shape
    w3 = jnp.stack([w_qkv[:, :d], w_qkv[:, d:2 * d], w_qkv[:, 2 * d:].T])
    return pl.pallas_call(
        functools.partial(_qkv_diff_kernel, bk=bk),
        grid=(n // tm, 3),
        in_specs=[pl.BlockSpec((tm, d), lambda i, j: (i, 0)),
                  pl.BlockSpec((1, d), lambda i, j: (0, 0)),
                  pl.BlockSpec((None, d, d), lambda i, j: (j, 0, 0))],
        out_specs=[pl.BlockSpec((tm, d), lambda i, j: (i, jnp.minimum(j, 1))),
                   pl.BlockSpec((tm // bk, d, bk), lambda i, j: (i, 0, 0))],
        out_shape=[jax.ShapeDtypeStruct((n, 2 * d), BF16),
                   jax.ShapeDtypeStruct((n // bk, d, bk), BF16)],
        scratch_shapes=[pltpu.VMEM((tm, d), BF16)],
        compiler_params=_cparams(("parallel", "arbitrary")),
        name="qkv_diff",
    )(x, g.reshape(1, d), w3)


def _dilated_kernel(q_ref, kp_ref, kc_ref, vtp_ref, vtc_ref, bias_ref, o_ref, lse_ref,
                    s_sc, m_sc, pv_sc):
    first_step = (pl.program_id(2) == 0).astype(jnp.int32)
    lane = lax.broadcasted_iota(jnp.int32, (BLOCK, LANES), 1)
    row = lax.broadcasted_iota(jnp.int32, (BLOCK, LANES), 0)
    head_row = lax.broadcasted_iota(jnp.int32, (N_HEADS_A, BLOCK), 0)
    lo_row = row < HEAD_DIM
    mask_lo = jnp.where(lane < HEAD_DIM, 1.0, 0.0).astype(BF16)
    mask_hi = jnp.where(lane < HEAD_DIM, 0.0, 1.0).astype(BF16)
    nt = (((1,), (1,)), ((), ()))
    ones = jnp.ones((SUM_ROWS, 2 * BLOCK), BF16)
    n_pairs = N_HEADS_A // 2
    items = [(qb, hp) for qb in range(DIL_QB) for hp in range(n_pairs)]

    def rows(qb):
        return slice(qb * BLOCK, (qb + 1) * BLOCK)

    def prev_cur(prev_ref, cur_ref, qb, sl):
        prev = prev_ref[:, sl] if qb == 0 else cur_ref[rows(qb - 1), sl]
        return prev, cur_ref[rows(qb), sl]

    for it, (qb, hp) in enumerate(items):
        sl = slice(hp * LANES, (hp + 1) * LANES)
        q = q_ref[rows(qb), sl]
        q2 = jnp.concatenate([q * mask_lo, q * mask_hi], axis=0)
        k = jnp.concatenate(prev_cur(kp_ref, kc_ref, qb, sl), axis=0)
        s = lax.dot_general(k, q2, nt, preferred_element_type=F32)
        s = s + bias_ref[first_step if qb == 0 else 0, hp]
        s_sc[it] = s
        m_sc[it] = jnp.max(s, axis=0, keepdims=True)
    for it, (qb, hp) in enumerate(items):
        sl = slice(hp * LANES, (hp + 1) * LANES)
        p = jnp.exp2(s_sc[it] - m_sc[it]).astype(BF16)
        vt = jnp.concatenate(prev_cur(vtp_ref, vtc_ref, qb, sl), axis=1)
        pv_sc[it] = jnp.dot(jnp.concatenate([vt, ones], axis=0), p,
                            preferred_element_type=F32)
    for qb in range(DIL_QB):
        lse_t = jnp.zeros((N_HEADS_A, BLOCK), F32)
        for hp in range(n_pairs):
            it = qb * n_pairs + hp
            l = pv_sc[it, LANES:LANES + 1]
            acc = pv_sc[it, :LANES] / l
            o_t = jnp.where(lo_row, acc[:, :BLOCK], acc[:, BLOCK:])
            o_ref[rows(qb), hp * LANES:(hp + 1) * LANES] = o_t.T.astype(o_ref.dtype)
            lse = (m_sc[it] + jnp.log2(l)) * LN2
            lse_t = jnp.where(head_row == 2 * hp, lse[:, :BLOCK], lse_t)
            lse_t = jnp.where(head_row == 2 * hp + 1, lse[:, BLOCK:], lse_t)
        lse_full = jnp.concatenate([lse_t, jnp.zeros((BLOCK - N_HEADS_A, BLOCK), F32)], axis=0)
        lse_ref[rows(qb), :] = lse_full.T


def _dilated_attention(qkv, bias, group, dilation, batch, seq):
    assert seq % (dilation * BLOCK * DIL_QB) == 0
    sub_len = seq // dilation
    steps = sub_len // (BLOCK * DIL_QB)
    qkv_v = qkv.reshape(batch, sub_len, 3 * dilation * D_MODEL)

    def col(c):
        return lambda b, r, n: (b, n, c * dilation + r)

    def col_prev(c):
        return lambda b, r, n: (b, jnp.maximum(n * DIL_QB - 1, 0), c * dilation + r)

    blk = (None, DIL_QB * BLOCK, D_MODEL)
    blk_prev = (None, BLOCK, D_MODEL)
    n_items = DIL_QB * N_HEADS_A // 2
    o, lse = pl.pallas_call(
        _dilated_kernel,
        grid=(batch, dilation, steps),
        in_specs=[pl.BlockSpec(blk, col(0)),
                  pl.BlockSpec(blk_prev, col_prev(1)),
                  pl.BlockSpec(blk, col(1)),
                  pl.BlockSpec(blk_prev, col_prev(2)),
                  pl.BlockSpec(blk, col(2)),
                  pl.BlockSpec((None, 2, N_HEADS_A // 2, 2 * BLOCK, 2 * BLOCK),
                               lambda b, r, n: (group, 0, 0, 0, 0))],
        out_specs=[pl.BlockSpec(blk, lambda b, r, n: (b, n, r)),
                   pl.BlockSpec((None, DIL_QB * BLOCK, LANES), lambda b, r, n: (b, n, r))],
        out_shape=[jax.ShapeDtypeStruct((batch, sub_len, dilation * D_MODEL), BF16),
                   jax.ShapeDtypeStruct((batch, sub_len, dilation * LANES), F32)],
        scratch_shapes=[pltpu.VMEM((n_items, 2 * BLOCK, 2 * BLOCK), F32),
                        pltpu.VMEM((n_items, 1, 2 * BLOCK), F32),
                        pltpu.VMEM((n_items, LANES + SUM_ROWS, 2 * BLOCK), F32)],
        compiler_params=_cparams(("parallel", "parallel", "arbitrary")),
        name=f"dilated_attn_d{dilation}",
    )(qkv_v, qkv_v, qkv_v, qkv_v, qkv_v, bias)
    return (o.reshape(batch * sub_len, dilation * D_MODEL),
            lse.reshape(batch * sub_len, dilation * LANES))


def _combine_proj_kernel(h_ref, o0_ref, o1_ref, o2_ref, l0_ref, l1_ref, l2_ref,
                         e_ref, w_ref, out_ref, lse_sc, o_sc):
    o_refs = [o0_ref, o1_ref, o2_ref]
    l_refs = [l0_ref, l1_ref, l2_ref]
    tm, d = h_ref.shape
    n_slabs = d // LANES
    for g, (_, dilation) in enumerate(DIL_CONFIGS):
        rows = tm // dilation
        for r in range(dilation):
            dst = pl.ds(r, rows, stride=dilation) if dilation > 1 else slice(None)
            lse_sc[g, dst, :] = l_refs[g][:, r * LANES:(r + 1) * LANES]
            for c in range(n_slabs):
                o_sc[g, c, dst, :] = o_refs[g][:, r * d + c * LANES:r * d + (c + 1) * LANES].astype(F32)
    lses = [lse_sc[g] for g in range(N_GROUPS)]
    mx = jnp.maximum(jnp.maximum(lses[0], lses[1]), lses[2])
    ws = [jnp.exp(l - mx) for l in lses]
    tot = ws[0] + ws[1] + ws[2]
    e = e_ref[...]
    aes = []
    for g in range(N_GROUPS):
        a = ws[g] / tot
        a_hi = a.astype(BF16)
        a_lo = (a - a_hi.astype(F32)).astype(BF16)
        aes.append(jnp.dot(a_hi, e, preferred_element_type=F32)
                   + jnp.dot(a_lo, e, preferred_element_type=F32))
    slabs = []
    for c in range(n_slabs):
        cols = slice(c * LANES, (c + 1) * LANES)
        slabs.append(aes[0][:, cols] * o_sc[0, c] + aes[1][:, cols] * o_sc[1, c]
                     + aes[2][:, cols] * o_sc[2, c])
    o = jnp.concatenate(slabs, axis=1).astype(BF16)
    out_ref[...] = h_ref[...] + jnp.dot(o, w_ref[...], preferred_element_type=F32)


def _combine_proj(h, outs, lses, w, *, tm=512):
    n, d = h.shape
    head_of_col = jnp.arange(d, dtype=jnp.int32) // HEAD_DIM
    expand = (jnp.arange(LANES, dtype=jnp.int32)[:, None] == head_of_col[None, :]).astype(BF16)
    row = lambda i: (i, 0)
    full = lambda i: (0, 0)
    dils = [dilation for _, dilation in DIL_CONFIGS]
    assert all(tm % (16 * dilation) == 0 for dilation in dils)
    return pl.pallas_call(
        _combine_proj_kernel,
        grid=(n // tm,),
        in_specs=[pl.BlockSpec((tm, d), row)]
                 + [pl.BlockSpec((tm // dilation, dilation * d), row) for dilation in dils]
                 + [pl.BlockSpec((tm // dilation, dilation * LANES), row) for dilation in dils]
                 + [pl.BlockSpec((LANES, d), full), pl.BlockSpec((d, d), full)],
        out_specs=pl.BlockSpec((tm, d), row),
        out_shape=jax.ShapeDtypeStruct((n, d), F32),
        scratch_shapes=[pltpu.VMEM((N_GROUPS, tm, LANES), F32),
                        pltpu.VMEM((N_GROUPS, d // LANES, tm, LANES), F32)],
        compiler_params=_cparams(("parallel",)),
        name="combine_proj",
    )(h, *outs, *lses, expand, w)


def _proj_residual_kernel(h_ref, x_ref, w_ref, out_ref):
    out_ref[...] = h_ref[...] + jnp.dot(x_ref[...], w_ref[...], preferred_element_type=F32)


def _proj_residual(h, x, w, *, tm=1024):
    n, d = h.shape
    return pl.pallas_call(
        _proj_residual_kernel,
        grid=(n // tm,),
        in_specs=[pl.BlockSpec((tm, d), lambda i: (i, 0)),
                  pl.BlockSpec((tm, x.shape[1]), lambda i: (i, 0)),
                  pl.BlockSpec(w.shape, lambda i: (0, 0))],
        out_specs=pl.BlockSpec((tm, d), lambda i: (i, 0)),
        out_shape=jax.ShapeDtypeStruct((n, d), F32),
        compiler_params=_cparams(("parallel",)),
        name="proj_residual",
    )(h, x, w)


def _mlp_ple_kernel(h_ref, p_ref, g_ref, w1_ref, w2_ref, gp_ref, wg_ref, wp_ref, fg_ref,
                    out_ref, xn_ref, acc_ref, *, final_norm):
    f = pl.program_id(1)

    @pl.when(f == 0)
    def _():
        xn_ref[...] = _rmsnorm_f32(h_ref[...], g_ref[...]).astype(BF16)
        acc_ref[...] = jnp.zeros_like(acc_ref)

    a = jnp.dot(xn_ref[...], w1_ref[...], preferred_element_type=F32)
    a = jnp.maximum(a, 0.0)
    a = (a * a).astype(BF16)
    acc_ref[...] += jnp.dot(a, w2_ref[...], preferred_element_type=F32)

    @pl.when(f == pl.num_programs(1) - 1)
    def _():
        x = h_ref[...] + acc_ref[...]
        xn = _rmsnorm_f32(x, gp_ref[...]).astype(BF16)
        gate = jax.nn.sigmoid(jnp.dot(xn, wg_ref[...], preferred_element_type=F32))
        proj = jnp.dot(p_ref[...].astype(BF16), wp_ref[...], preferred_element_type=F32)
        y = x + gate * proj
        if final_norm:
            y = _rmsnorm_f32(y, fg_ref[...])
        out_ref[...] = y


def _mlp_ple(h, p_all, layer, g, w1, w2, gp, wg, wp, fg, *, final_norm, tm=1024, tf=1024):
    n, d = h.shape
    dff = w1.shape[1]
    pd = p_all.shape[-1]
    vec = lambda i, f: (0, 0)
    return pl.pallas_call(
        functools.partial(_mlp_ple_kernel, final_norm=final_norm),
        grid=(n // tm, dff // tf),
        in_specs=[pl.BlockSpec((tm, d), lambda i, f: (i, 0)),
                  pl.BlockSpec((None, tm, pd), lambda i, f: (layer, i, 0)),
                  pl.BlockSpec((1, d), vec),
                  pl.BlockSpec((d, tf), lambda i, f: (0, f)),
                  pl.BlockSpec((tf, d), lambda i, f: (f, 0)),
                  pl.BlockSpec((1, d), vec),
                  pl.BlockSpec((d, d), vec),
                  pl.BlockSpec((pd, d), vec),
                  pl.BlockSpec((1, d), vec)],
        out_specs=pl.BlockSpec((tm, d), lambda i, f: (i, 0)),
        out_shape=jax.ShapeDtypeStruct((n, d), F32),
        scratch_shapes=[pltpu.VMEM((tm, d), BF16), pltpu.VMEM((tm, d), F32)],
        compiler_params=_cparams(("parallel", "arbitrary")),
        name="mlp_ple",
    )(h, p_all, g.reshape(1, d), w1, w2, gp.reshape(1, d), wg, wp, fg.reshape(1, d))


def _diff_attn_kernel(lq1_ref, lk1_ref, lq2_ref, lk2_ref, q_ref, qn_ref, k_ref, vt_ref, u_ref, sg_ref,
                      o_ref, m_sc, acc_sc, sa_sc, samax_sc, sb_sc, sbmax_sc, *, bq, bk, lambda_init):
    assert bq == 2 * bk
    half = bq // 2
    i_q = pl.program_id(2)
    q0 = i_q * bq
    nt = (((1,), (1,)), ((), ()))
    lane = lax.broadcasted_iota(jnp.int32, (bq, LANES), 1)

    def query_columns(ref):
        qf = ref[...].astype(F32)
        a = jnp.where(lane < HEAD_DIM, qf, 0.0)
        b = jnp.where(lane >= HEAD_DIM, qf, 0.0)
        return jnp.concatenate([a[:half], b[:half], a[half:], b[half:]], axis=0).astype(BF16)

    q2 = query_columns(q_ref)
    q2_next = query_columns(qn_ref)

    m_sc[...] = jnp.full(m_sc.shape, NEG, F32)
    acc_sc[...] = jnp.zeros(acc_sc.shape, F32)

    n_blocks = 2 * bq // LANES
    per_part = half // LANES
    n_trips = (q0 + bq) // (2 * bk)
    ones = jnp.ones((SUM_ROWS, bk), BF16)

    def scores(c, q2x, q0x, s_ref, smax_ref, first_block=0):
        k0 = pl.multiple_of(c * bk, bk)
        lanes = slice(first_block * LANES, n_blocks * LANES)
        s = lax.dot_general(k_ref[pl.ds(k0, bk), :], q2x[lanes], nt,
                            preferred_element_type=F32)
        rows = []
        for jb in range(bk // LANES):
            tiles = []
            for ib in range(first_block, n_blocks):
                part = ib // per_part
                i0 = (part // 2) * half + (ib % per_part) * LANES
                t = lax.shift_right_arithmetic(q0x + i0 - k0 - jb * LANES, LOG2_LANES)
                t = jnp.where(t < 0, DIFF_MASK_TILE, jnp.minimum(t, DIFF_CONST_TILE))
                tiles.append(u_ref[t, part % 2])
            rows.append(jnp.concatenate(tiles, axis=1))
        s = s + jnp.concatenate(rows, axis=0)
        s_ref[:, lanes] = s
        smax_ref[:, lanes] = jnp.max(s, axis=0, keepdims=True)

    def accumulate(c, s_ref, smax_ref, first_block=0):
        lanes = slice(first_block * LANES, n_blocks * LANES)
        m_prev = m_sc[:, lanes]
        m_new = jnp.maximum(m_prev, smax_ref[:, lanes])
        m_sc[:, lanes] = m_new
        p = jnp.exp2(s_ref[:, lanes] - m_new).astype(BF16)
        v1 = jnp.concatenate([vt_ref[c], ones], axis=0)
        acc_sc[:, lanes] = (jnp.exp2(m_prev - m_new) * acc_sc[:, lanes]
                            + jnp.dot(v1, p, preferred_element_type=F32))

    @pl.when(i_q == 0)
    def _():
        scores(0, q2, q0, sa_sc, samax_sc)

    def body(i, carry):
        scores(2 * i + 1, q2, q0, sb_sc, sbmax_sc)
        accumulate(2 * i, sa_sc, samax_sc)
        scores(2 * i + 2, q2, q0, sa_sc, samax_sc)
        accumulate(2 * i + 1, sb_sc, sbmax_sc)
        return carry

    lax.fori_loop(0, n_trips - 1, body, 0)

    last = n_trips - 1
    hi = n_blocks // 2
    scores(2 * last + 1, q2, q0, sb_sc, sbmax_sc, first_block=hi)
    accumulate(2 * last, sa_sc, samax_sc)
    scores(0, q2_next, q0 + bq, sa_sc, samax_sc)
    accumulate(2 * last + 1, sb_sc, sbmax_sc, first_block=hi)

    lam = (jnp.exp(jnp.sum(lq1_ref[...] * lk1_ref[...], keepdims=True))
           - jnp.exp(jnp.sum(lq2_ref[...] * lk2_ref[...], keepdims=True)) + lambda_init)
    accl = acc_sc[...]
    acc = accl[:LANES] / accl[LANES:LANES + 1]
    map0 = jnp.concatenate([acc[:, :half], acc[:, bq:bq + half]], axis=1)
    map1 = jnp.concatenate([acc[:, half:bq], acc[:, bq + half:]], axis=1)
    o = (map0 - lam * map1).T
    y = _rmsnorm_f32(o, sg_ref[...]) * (1.0 - lambda_init)
    o_ref[...] = y.astype(o_ref.dtype)


def _diff_attention(qk, vt, u, lq1, lk1, lq2, lk2, subln, lambda_init, batch, seq, *, bq, bk):
    nh = N_HEADS_B
    nkt = seq // bk
    nq = seq // bq
    vec = lambda a: a.reshape(1, -1)
    small = pl.BlockSpec((1, HEAD_DIM), lambda b, h, i: (0, 0))
    return pl.pallas_call(
        functools.partial(_diff_attn_kernel, bq=bq, bk=bk, lambda_init=lambda_init),
        grid=(batch, nh, seq // bq),
        in_specs=[small, small, small, small,
                  pl.BlockSpec((None, bq, LANES), lambda b, h, i: (b, i, h)),
                  pl.BlockSpec((None, bq, LANES), lambda b, h, i: (b, jnp.minimum(i + 1, nq - 1), h)),
                  pl.BlockSpec((None, seq, LANES), lambda b, h, i: (b, 0, nh + h)),
                  pl.BlockSpec((nkt, LANES, bk), lambda b, h, i: (b, h, 0)),
                  pl.BlockSpec((DIFF_N_TILES, 2, LANES, LANES), lambda b, h, i: (0, h, 0, 0)),
                  pl.BlockSpec((1, LANES), lambda b, h, i: (0, 0))],
        out_specs=pl.BlockSpec((None, bq, LANES), lambda b, h, i: (b, i, h)),
        out_shape=jax.ShapeDtypeStruct((batch, seq, D_MODEL), BF16),
        scratch_shapes=[pltpu.VMEM((1, 2 * bq), F32),
                        pltpu.VMEM((LANES + SUM_ROWS, 2 * bq), F32),
                        pltpu.VMEM((bk, 2 * bq), F32),
                        pltpu.VMEM((1, 2 * bq), F32),
                        pltpu.VMEM((bk, 2 * bq), F32),
                        pltpu.VMEM((1, 2 * bq), F32)],
        compiler_params=_cparams(("parallel", "parallel", "arbitrary")),
        name="diff_attn",
    )(vec(lq1), vec(lk1), vec(lq2), vec(lk2), qk, qk, qk, vt, u, vec(subln))


def kernel(x, p, rel_bias, a_w_qkv, a_w_o, b_w_qkv, b_w_o, b_lambda_q1, b_lambda_k1, b_lambda_q2, b_lambda_k2, b_subln, norm_mix, norm_mlp, w_ff1, w_ff2, norm_ple, w_ple_gate, w_ple_proj, final_norm):
    batch, seq, d = x.shape
    depth = p.shape[0]
    n = batch * seq
    h = x.reshape(n, d)
    p_all = p.reshape(depth, n, p.shape[-1])
    n_mixers = 2

    for i in range(depth):
        j = i // n_mixers
        if i % n_mixers == 0:
            w_a = a_w_qkv[j].astype(BF16)
            bias = _build_dilated_bias(rel_bias)
            outs, lses = [], []
            for g, (_, dilation) in enumerate(DIL_CONFIGS):
                w_g = w_a[:, g * 3 * d:(g + 1) * 3 * d]
                qkv = _qkv_dilated(h, norm_mix[i], w_g, dilation, tm=max(1024, BLOCK * dilation))
                o_g, lse_g = _dilated_attention(qkv, bias, g, dilation, batch, seq)
                outs.append(o_g)
                lses.append(lse_g)
            h = _combine_proj(h, outs, lses, a_w_o[j].astype(BF16))
        else:
            lambda_init = 0.8 - 0.6 * math.exp(-0.3 * i)
            bq, bk = 1024, 512
            w_b = b_w_qkv[j].astype(BF16)
            qk, vt = _qkv_diff(h, norm_mix[i], w_b, bk=bk)
            u = _build_diff_bias(rel_bias)
            o = _diff_attention(qk.reshape(batch, seq, 2 * d), vt, u,
                                b_lambda_q1[j], b_lambda_k1[j], b_lambda_q2[j], b_lambda_k2[j],
                                b_subln[j], lambda_init, batch, seq, bq=bq, bk=bk)
            h = _proj_residual(h, o.reshape(n, d), b_w_o[j].astype(BF16))
        h = _mlp_ple(h, p_all, i, norm_mlp[i], w_ff1[i].astype(BF16), w_ff2[i].astype(BF16),
                     norm_ple[i], w_ple_gate[i].astype(BF16), w_ple_proj[i].astype(BF16),
                     final_norm, final_norm=(i == depth - 1))
    return h.reshape(batch, seq, d)
```

```python
import functools
import math

import jax
import jax.numpy as jnp
from jax import lax
from jax.experimental import pallas as pl
from jax.experimental.pallas import tpu as pltpu

F32 = jnp.float32
BF16 = jnp.bfloat16

D_MODEL = 1024
HEAD_DIM = 64
BLOCK = 128
DIL_QB = 4
DIL_CONFIGS = ((128, 1), (512, 4), (2048, 16))
N_GROUPS = len(DIL_CONFIGS)
N_HEADS_A = D_MODEL // HEAD_DIM
N_HEADS_B = D_MODEL // (2 * HEAD_DIM)
N_BUCKETS = 32
MAX_DISTANCE = 2048
N_BIAS_COLS = 16
EPS = 1e-6
NEG = -1e30
LANES = 128
SUBLANES = 8
LOG2_LANES = 7
QK_SCALE = HEAD_DIM ** -0.5
LOG2E = math.log2(math.e)
LN2 = math.log(2.0)

DIFF_CONST_TILE = (MAX_DISTANCE + LANES - 1) // LANES + 1
DIFF_MASK_TILE = DIFF_CONST_TILE + 1
DIFF_N_TILES = DIFF_MASK_TILE + 1
SUM_ROWS = 16

VMEM_LIMIT = 48 * 1024 * 1024
VMEM_LIMIT_PROJ = 56 * 1024 * 1024


def _cparams(sem, vmem_limit=VMEM_LIMIT):
    return pltpu.CompilerParams(dimension_semantics=sem, vmem_limit_bytes=vmem_limit)


def _rmsnorm_f32(x, g):
    ms = jnp.mean(x * x, axis=-1, keepdims=True)
    return x * lax.rsqrt(ms + EPS) * g


def _rel_bucket(dist):
    n = jnp.maximum(dist, 0)
    max_exact = N_BUCKETS // 2
    nf = jnp.maximum(n, 1).astype(F32)
    large = max_exact + (jnp.log(nf / max_exact) / math.log(MAX_DISTANCE / max_exact)
                         * (N_BUCKETS - max_exact)).astype(jnp.int32)
    large = jnp.minimum(large, N_BUCKETS - 1)
    return jnp.where(n < max_exact, n, large)


def _table_lookup(bucket, tab_ref, col):
    acc = jnp.zeros(bucket.shape, F32)
    for b in range(N_BUCKETS):
        acc = jnp.where(bucket == b, tab_ref[b, col], acc)
    return acc


def _dilated_bias_kernel(tab_ref, o_ref):
    g = pl.program_id(0)
    dilation = jnp.where(g == 0, DIL_CONFIGS[0][1],
                         jnp.where(g == 1, DIL_CONFIGS[1][1], DIL_CONFIGS[2][1]))
    kj = lax.broadcasted_iota(jnp.int32, (2 * BLOCK, BLOCK), 0)
    qi = lax.broadcasted_iota(jnp.int32, (2 * BLOCK, BLOCK), 1)
    sub = qi + BLOCK - kj
    band = (sub >= 0) & (sub <= BLOCK)
    band_first = band & (kj >= BLOCK)
    bucket = _rel_bucket(sub * dilation)
    for c in range(N_BIAS_COLS):
        lanes = slice((c % 2) * BLOCK, (c % 2 + 1) * BLOCK)
        bias = _table_lookup(bucket, tab_ref, c) * LOG2E
        o_ref[0, c // 2, :, lanes] = jnp.where(band, bias, NEG)
        o_ref[1, c // 2, :, lanes] = jnp.where(band_first, bias, NEG)


def _build_dilated_bias(rel_bias):
    return pl.pallas_call(
        _dilated_bias_kernel,
        grid=(N_GROUPS,),
        in_specs=[pl.BlockSpec(memory_space=pltpu.SMEM)],
        out_specs=pl.BlockSpec((None, 2, N_HEADS_A // 2, 2 * BLOCK, 2 * BLOCK),
                               lambda g: (g, 0, 0, 0, 0)),
        out_shape=jax.ShapeDtypeStruct((N_GROUPS, 2, N_HEADS_A // 2, 2 * BLOCK, 2 * BLOCK), F32),
        compiler_params=_cparams(("arbitrary",)),
        name="dilated_bias",
    )(rel_bias)


def _diff_bias_kernel(tab_ref, o_ref):
    t = pl.program_id(0)
    kj = lax.broadcasted_iota(jnp.int32, (LANES, LANES), 0)
    qi = lax.broadcasted_iota(jnp.int32, (LANES, LANES), 1)
    dist = t * LANES + qi - kj
    masked = (dist < 0) | (t == DIFF_MASK_TILE)
    bucket = _rel_bucket(dist)
    init = jnp.where(masked, NEG, 0.0)
    for c in range(N_BIAS_COLS):
        o_ref[c] = init
    live = jnp.where(masked, -1, bucket)

    def body(b, carry):
        hit = live == b
        for c in range(N_BIAS_COLS):
            o_ref[c] = jnp.where(hit, tab_ref[b, c] * LOG2E, o_ref[c])
        return carry

    lax.fori_loop(jnp.min(bucket), jnp.max(bucket) + 1, body, 0)


def _build_diff_bias(rel_bias):
    return pl.pallas_call(
        _diff_bias_kernel,
        grid=(DIFF_N_TILES,),
        in_specs=[pl.BlockSpec(memory_space=pltpu.SMEM)],
        out_specs=pl.BlockSpec((None, N_BIAS_COLS, LANES, LANES), lambda t: (t, 0, 0, 0)),
        out_shape=jax.ShapeDtypeStruct((DIFF_N_TILES, N_BIAS_COLS, LANES, LANES), F32),
        compiler_params=_cparams(("arbitrary",)),
        name="diff_bias",
    )(rel_bias)


def _qkv_dilated_kernel(*refs, dilation, n_slabs):
    n_x = len(refs) - 4
    x_refs = refs[:n_x]
    g_ref, w_ref, o_ref, xn_sc = refs[n_x:]
    j = pl.program_id(1)
    tm, d_model = xn_sc.shape
    rows = tm // dilation

    @pl.when(j == 0)
    def _():
        for r in range(dilation):
            if n_slabs is None:
                xs = [x_refs[0][...]]
            else:
                xs = [x_ref[pl.ds(r, rows, stride=dilation), :] for x_ref in x_refs]
            slab_w = d_model // len(xs)
            sq = xs[0] * xs[0]
            for x in xs[1:]:
                sq = sq + x * x
            ssq = jnp.sum(sq, axis=-1, keepdims=True)
            scale = lax.rsqrt(ssq * (1.0 / d_model) + EPS)
            for s, x in enumerate(xs):
                cols = slice(s * slab_w, (s + 1) * slab_w)
                xn_sc[r * rows:(r + 1) * rows, cols] = (x * scale * g_ref[:, cols]).astype(BF16)

    y = jnp.dot(xn_sc[...], w_ref[...], preferred_element_type=F32)

    @pl.when(j < 2)
    def _():
        scale = jnp.where(j == 0, jnp.float32(QK_SCALE * LOG2E), jnp.float32(1.0))
        for r in range(dilation):
            o_ref[:, r * d_model:(r + 1) * d_model] = (
                y[r * rows:(r + 1) * rows] * scale).astype(o_ref.dtype)

    @pl.when(j == 2)
    def _():
        for r in range(dilation):
            for nb in range(rows // BLOCK):
                for hp in range(d_model // LANES):
                    tile = y[r * rows + nb * BLOCK:r * rows + (nb + 1) * BLOCK,
                             hp * LANES:(hp + 1) * LANES]
                    o_ref[nb * BLOCK:(nb + 1) * BLOCK,
                          r * d_model + hp * LANES:r * d_model + (hp + 1) * LANES] = (
                              tile.T.astype(o_ref.dtype))


def _qkv_dilated(x, g, w, dilation, *, tm):
    n, d = x.shape
    assert w.shape == (d, 3 * d)
    rows = tm // dilation
    assert tm % dilation == 0 and rows % BLOCK == 0
    if dilation == 1:
        n_slabs = None
        x_in = [x]
        x_specs = [pl.BlockSpec((tm, d), lambda i, j: (i, 0))]
    else:
        n_slabs = d // LANES
        x_in = [x] * n_slabs
        x_specs = [pl.BlockSpec((tm, LANES), functools.partial(lambda i, j, s: (i, s), s=s))
                   for s in range(n_slabs)]
    return pl.pallas_call(
        functools.partial(_qkv_dilated_kernel, dilation=dilation, n_slabs=n_slabs),
        grid=(n // tm, 3),
        in_specs=x_specs + [pl.BlockSpec((1, d), lambda i, j: (0, 0)),
                            pl.BlockSpec((d, d), lambda i, j: (0, j))],
        out_specs=pl.BlockSpec((rows, dilation * d), lambda i, j: (i, j)),
        out_shape=jax.ShapeDtypeStruct((n // dilation, 3 * dilation * d), BF16),
        scratch_shapes=[pltpu.VMEM((tm, d), BF16)],
        compiler_params=_cparams(("parallel", "arbitrary")),
        name=f"qkv_d{dilation}",
    )(*x_in, g.reshape(1, d), w)


def _qkv_diff_kernel(x_ref, g_ref, w_ref, qk_ref, vt_ref, xn_sc, *, bk):
    j = pl.program_id(1)
    tm = x_ref.shape[0]

    @pl.when(j == 0)
    def _():
        xn_sc[...] = _rmsnorm_f32(x_ref[...], g_ref[...]).astype(BF16)

    @pl.when(j < 2)
    def _():
        y = jnp.dot(xn_sc[...], w_ref[...], preferred_element_type=F32)
        scale = jnp.where(j == 0, jnp.float32(QK_SCALE * LOG2E), jnp.float32(1.0))
        qk_ref[...] = (y * scale).astype(qk_ref.dtype)

    @pl.when(j == 2)
    def _():
        nt = (((1,), (1,)), ((), ()))
        for c in range(tm // bk):
            vt_ref[c] = lax.dot_general(w_ref[...], xn_sc[c * bk:(c + 1) * bk], nt,
                                        preferred_element_type=F32).astype(vt_ref.dtype)


def _qkv_diff(x, g, w_qkv, *, bk, tm=1024):
    n, d = x.shape
    w3 = jnp.stack([w_qkv[:, :d], w_qkv[:, d:2 * d], w_qkv[:, 2 * d:].T])
    return pl.pallas_call(
        functools.partial(_qkv_diff_kernel, bk=bk),
        grid=(n // tm, 3),
        in_specs=[pl.BlockSpec((tm, d), lambda i, j: (i, 0)),
                  pl.BlockSpec((1, d), lambda i, j: (0, 0)),
                  pl.BlockSpec((None, d, d), lambda i, j: (j, 0, 0))],
        out_specs=[pl.BlockSpec((tm, d), lambda i, j: (i, jnp.minimum(j, 1))),
                   pl.BlockSpec((tm // bk, d, bk), lambda i, j: (i, 0, 0))],
        out_shape=[jax.ShapeDtypeStruct((n, 2 * d), BF16),
                   jax.ShapeDtypeStruct((n // bk, d, bk), BF16)],
        scratch_shapes=[pltpu.VMEM((tm, d), BF16)],
        compiler_params=_cparams(("parallel", "arbitrary")),
        name="qkv_diff",
    )(x, g.reshape(1, d), w3)


def _dilated_kernel(q_ref, kp_ref, kc_ref, vtp_ref, vtc_ref, bias_ref, o_ref, lse_ref,
                    s_sc, m_sc, pv_sc):
    first_step = (pl.program_id(2) == 0).astype(jnp.int32)
    lane = lax.broadcasted_iota(jnp.int32, (BLOCK, LANES), 1)
    row = lax.broadcasted_iota(jnp.int32, (BLOCK, LANES), 0)
    head_row = lax.broadcasted_iota(jnp.int32, (N_HEADS_A, BLOCK), 0)
    lo_row = row < HEAD_DIM
    mask_lo = jnp.where(lane < HEAD_DIM, 1.0, 0.0).astype(BF16)
    mask_hi = jnp.where(lane < HEAD_DIM, 0.0, 1.0).astype(BF16)
    nt = (((1,), (1,)), ((), ()))
    ones = jnp.ones((SUM_ROWS, 2 * BLOCK), BF16)
    n_pairs = N_HEADS_A // 2
    items = [(qb, hp) for qb in range(DIL_QB) for hp in range(n_pairs)]

    def rows(qb):
        return slice(qb * BLOCK, (qb + 1) * BLOCK)

    def prev_cur(prev_ref, cur_ref, qb, sl):
        prev = prev_ref[:, sl] if qb == 0 else cur_ref[rows(qb - 1), sl]
        return prev, cur_ref[rows(qb), sl]

    for it, (qb, hp) in enumerate(items):
        sl = slice(hp * LANES, (hp + 1) * LANES)
        q = q_ref[rows(qb), sl]
        q2 = jnp.concatenate([q * mask_lo, q * mask_hi], axis=0)
        k = jnp.concatenate(prev_cur(kp_ref, kc_ref, qb, sl), axis=0)
        s = lax.dot_general(k, q2, nt, preferred_element_type=F32)
        s = s + bias_ref[first_step if qb == 0 else 0, hp]
        s_sc[it] = s
        m_sc[it] = jnp.max(s, axis=0, keepdims=True)
    for it, (qb, hp) in enumerate(items):
        sl = slice(hp * LANES, (hp + 1) * LANES)
        p = jnp.exp2(s_sc[it] - m_sc[it]).astype(BF16)
        vt = jnp.concatenate(prev_cur(vtp_ref, vtc_ref, qb, sl), axis=1)
        pv_sc[it] = jnp.dot(jnp.concatenate([vt, ones], axis=0), p,
                            preferred_element_type=F32)
    for qb in range(DIL_QB):
        lse_t = jnp.zeros((N_HEADS_A, BLOCK), F32)
        for hp in range(n_pairs):
            it = qb * n_pairs + hp
            l = pv_sc[it, LANES:LANES + 1]
            acc = pv_sc[it, :LANES] / l
            o_t = jnp.where(lo_row, acc[:, :BLOCK], acc[:, BLOCK:])
            o_ref[rows(qb), hp * LANES:(hp + 1) * LANES] = o_t.T.astype(o_ref.dtype)
            lse = (m_sc[it] + jnp.log2(l)) * LN2
            lse_t = jnp.where(head_row == 2 * hp, lse[:, :BLOCK], lse_t)
            lse_t = jnp.where(head_row == 2 * hp + 1, lse[:, BLOCK:], lse_t)
        lse_full = jnp.concatenate([lse_t, jnp.zeros((BLOCK - N_HEADS_A, BLOCK), F32)], axis=0)
        lse_ref[rows(qb), :] = lse_full.T


def _dilated_attention(qkv, bias, group, dilation, batch, seq):
    assert seq % (dilation * BLOCK * DIL_QB) == 0
    sub_len = seq // dilation
    steps = sub_len // (BLOCK * DIL_QB)
    qkv_v = qkv.reshape(batch, sub_len, 3 * dilation * D_MODEL)

    def col(c):
        return lambda b, r, n: (b, n, c * dilation + r)

    def col_prev(c):
        return lambda b, r, n: (b, jnp.maximum(n * DIL_QB - 1, 0), c * dilation + r)

    blk = (None, DIL_QB * BLOCK, D_MODEL)
    blk_prev = (None, BLOCK, D_MODEL)
    n_items = DIL_QB * N_HEADS_A // 2
    o, lse = pl.pallas_call(
        _dilated_kernel,
        grid=(batch, dilation, steps),
        in_specs=[pl.BlockSpec(blk, col(0)),
                  pl.BlockSpec(blk_prev, col_prev(1)),
                  pl.BlockSpec(blk, col(1)),
                  pl.BlockSpec(blk_prev, col_prev(2)),
                  pl.BlockSpec(blk, col(2)),
                  pl.BlockSpec((None, 2, N_HEADS_A // 2, 2 * BLOCK, 2 * BLOCK),
                               lambda b, r, n: (group, 0, 0, 0, 0))],
        out_specs=[pl.BlockSpec(blk, lambda b, r, n: (b, n, r)),
                   pl.BlockSpec((None, DIL_QB * BLOCK, LANES), lambda b, r, n: (b, n, r))],
        out_shape=[jax.ShapeDtypeStruct((batch, sub_len, dilation * D_MODEL), BF16),
                   jax.ShapeDtypeStruct((batch, sub_len, dilation * LANES), F32)],
        scratch_shapes=[pltpu.VMEM((n_items, 2 * BLOCK, 2 * BLOCK), F32),
                        pltpu.VMEM((n_items, 1, 2 * BLOCK), F32),
                        pltpu.VMEM((n_items, LANES + SUM_ROWS, 2 * BLOCK), F32)],
        compiler_params=_cparams(("parallel", "parallel", "arbitrary")),
        name=f"dilated_attn_d{dilation}",
    )(qkv_v, qkv_v, qkv_v, qkv_v, qkv_v, bias)
    return (o.reshape(batch * sub_len, dilation * D_MODEL),
            lse.reshape(batch * sub_len, dilation * LANES))


def _combine_proj_kernel(h_ref, o0_ref, o1_ref, o2_ref, l0_ref, l1_ref, l2_ref,
                         e_ref, w_ref, out_ref, lse_sc, o_sc):
    o_refs = [o0_ref, o1_ref, o2_ref]
    l_refs = [l0_ref, l1_ref, l2_ref]
    tm, d = h_ref.shape
    n_slabs = d // LANES
    for g, (_, dilation) in enumerate(DIL_CONFIGS):
        rows = tm // dilation
        for r in range(dilation):
            dst = pl.ds(r, rows, stride=dilation) if dilation > 1 else slice(None)
            lse_sc[g, dst, :] = l_refs[g][:, r * LANES:(r + 1) * LANES]
            for c in range(n_slabs):
                o_sc[g, c, dst, :] = o_refs[g][:, r * d + c * LANES:r * d + (c + 1) * LANES].astype(F32)
    lses = [lse_sc[g] for g in range(N_GROUPS)]
    mx = jnp.maximum(jnp.maximum(lses[0], lses[1]), lses[2])
    ws = [jnp.exp(l - mx) for l in lses]
    tot = ws[0] + ws[1] + ws[2]
    head_lane = lax.broadcasted_iota(jnp.int32, (tm, LANES), 1) < N_HEADS_A
    packed = None
    for g in range(N_GROUPS):
        a = jnp.where(head_lane, ws[g] / tot, 0.0)
        a_hi = a.astype(BF16).astype(F32)
        for part, piece in enumerate((a_hi, a - a_hi)):
            shift = (part * N_GROUPS + g) * N_HEADS_A
            moved = piece if shift == 0 else pltpu.roll(piece, shift, axis=1)
            packed = moved if packed is None else packed + moved
    ae_all = jnp.dot(packed.astype(BF16), e_ref[...], preferred_element_type=F32)
    aes = [ae_all[:, g * d:(g + 1) * d] for g in range(N_GROUPS)]
    slabs = []
    for c in range(n_slabs):
        cols = slice(c * LANES, (c + 1) * LANES)
        slabs.append(aes[0][:, cols] * o_sc[0, c] + aes[1][:, cols] * o_sc[1, c]
                     + aes[2][:, cols] * o_sc[2, c])
    o = jnp.concatenate(slabs, axis=1).astype(BF16)
    out_ref[...] = h_ref[...] + jnp.dot(o, w_ref[...], preferred_element_type=F32)


def _combine_proj(h, outs, lses, w, *, tm=512):
    n, d = h.shape
    rows = jnp.arange(LANES, dtype=jnp.int32)
    cols = jnp.arange(N_GROUPS * d, dtype=jnp.int32)
    row_group = (rows // N_HEADS_A) % N_GROUPS
    row_head = rows % N_HEADS_A
    row_used = rows < 2 * N_GROUPS * N_HEADS_A
    expand = (row_used[:, None] & (row_group[:, None] == (cols // d)[None, :])
              & (row_head[:, None] == ((cols % d) // HEAD_DIM)[None, :])).astype(BF16)
    row = lambda i: (i, 0)
    full = lambda i: (0, 0)
    dils = [dilation for _, dilation in DIL_CONFIGS]
    assert all(tm % (16 * dilation) == 0 for dilation in dils)
    return pl.pallas_call(
        _combine_proj_kernel,
        grid=(n // tm,),
        in_specs=[pl.BlockSpec((tm, d), row)]
                 + [pl.BlockSpec((tm // dilation, dilation * d), row) for dilation in dils]
                 + [pl.BlockSpec((tm // dilation, dilation * LANES), row) for dilation in dils]
                 + [pl.BlockSpec((LANES, N_GROUPS * d), full), pl.BlockSpec((d, d), full)],
        out_specs=pl.BlockSpec((tm, d), row),
        out_shape=jax.ShapeDtypeStruct((n, d), F32),
        scratch_shapes=[pltpu.VMEM((N_GROUPS, tm, LANES), F32),
                        pltpu.VMEM((N_GROUPS, d // LANES, tm, LANES), F32)],
        compiler_params=_cparams(("parallel",)),
        name="combine_proj",
    )(h, *outs, *lses, expand, w)


def _mlp_ple_kernel(*refs, final_norm, with_proj):
    if with_proj:
        h_ref, o_ref, wo_ref = refs[:3]
        refs = refs[3:]
    else:
        h_ref = refs[0]
        refs = refs[1:]
    p_ref, g_ref, w1_ref, w2_ref, gp_ref, wg_ref, wp_ref, fg_ref, out_ref, xn_ref, acc_ref = refs
    f = pl.program_id(1)

    @pl.when(f == 0)
    def _():
        if with_proj:
            h0 = h_ref[...] + jnp.dot(o_ref[...], wo_ref[...], preferred_element_type=F32)
            out_ref[...] = h0
        else:
            h0 = h_ref[...]
        xn_ref[...] = _rmsnorm_f32(h0, g_ref[...]).astype(BF16)
        acc_ref[...] = jnp.zeros_like(acc_ref)

    a = jnp.dot(xn_ref[...], w1_ref[...], preferred_element_type=F32)
    a = jnp.maximum(a, 0.0)
    a = (a * a).astype(BF16)
    acc_ref[...] += jnp.dot(a, w2_ref[...], preferred_element_type=F32)

    @pl.when(f == pl.num_programs(1) - 1)
    def _():
        x = (out_ref[...] if with_proj else h_ref[...]) + acc_ref[...]
        xn = _rmsnorm_f32(x, gp_ref[...]).astype(BF16)
        gate = jax.nn.sigmoid(jnp.dot(xn, wg_ref[...], preferred_element_type=F32))
        proj = jnp.dot(p_ref[...].astype(BF16), wp_ref[...], preferred_element_type=F32)
        y = x + gate * proj
        if final_norm:
            y = _rmsnorm_f32(y, fg_ref[...])
        out_ref[...] = y


def _mlp_ple(h, p_all, layer, g, w1, w2, gp, wg, wp, fg, *, final_norm, proj=None, tm=1024, tf=1024):
    n, d = h.shape
    dff = w1.shape[1]
    pd = p_all.shape[-1]
    vec = lambda i, f: (0, 0)
    row = lambda i, f: (i, 0)
    lead_specs = [pl.BlockSpec((tm, d), row)]
    lead_args = [h]
    if proj is not None:
        o, wo = proj
        lead_specs += [pl.BlockSpec((tm, o.shape[1]), row), pl.BlockSpec(wo.shape, vec)]
        lead_args += [o, wo]
    return pl.pallas_call(
        functools.partial(_mlp_ple_kernel, final_norm=final_norm, with_proj=proj is not None),
        grid=(n // tm, dff // tf),
        in_specs=lead_specs + [
                  pl.BlockSpec((None, tm, pd), lambda i, f: (layer, i, 0)),
                  pl.BlockSpec((1, d), vec),
                  pl.BlockSpec((d, tf), lambda i, f: (0, f)),
                  pl.BlockSpec((tf, d), lambda i, f: (f, 0)),
                  pl.BlockSpec((1, d), vec),
                  pl.BlockSpec((d, d), vec),
                  pl.BlockSpec((pd, d), vec),
                  pl.BlockSpec((1, d), vec)],
        out_specs=pl.BlockSpec((tm, d), lambda i, f: (i, 0)),
        out_shape=jax.ShapeDtypeStruct((n, d), F32),
        scratch_shapes=[pltpu.VMEM((tm, d), BF16), pltpu.VMEM((tm, d), F32)],
        compiler_params=_cparams(("parallel", "arbitrary"),
                                 vmem_limit=VMEM_LIMIT_PROJ if proj is not None else VMEM_LIMIT),
        name="mlp_ple",
    )(*lead_args, p_all, g.reshape(1, d), w1, w2, gp.reshape(1, d), wg, wp, fg.reshape(1, d))


def _diff_attn_kernel(lq1_ref, lk1_ref, lq2_ref, lk2_ref, q_ref, qn_ref, k_ref, vt_ref, u_ref, sg_ref,
                      o_ref, m_sc, acc_sc, sa_sc, samax_sc, sb_sc, sbmax_sc, *, bq, bk, lambda_init):
    n_parts = bq // bk
    assert bq == n_parts * bk and n_parts % 2 == 0
    i_q = pl.program_id(2)
    q0 = i_q * bq
    nt = (((1,), (1,)), ((), ()))
    lane = lax.broadcasted_iota(jnp.int32, (bq, LANES), 1)

    def query_columns(ref):
        qf = ref[...].astype(F32)
        a = jnp.where(lane < HEAD_DIM, qf, 0.0)
        b = jnp.where(lane >= HEAD_DIM, qf, 0.0)
        pieces = []
        for part in range(n_parts):
            pieces += [a[part * bk:(part + 1) * bk], b[part * bk:(part + 1) * bk]]
        return jnp.concatenate(pieces, axis=0).astype(BF16)

    q2 = query_columns(q_ref)
    q2_next = query_columns(qn_ref)

    m_sc[...] = jnp.full(m_sc.shape, NEG, F32)
    acc_sc[...] = jnp.zeros(acc_sc.shape, F32)

    per_map = bk // LANES
    part_blocks = 2 * per_map
    n_blocks = n_parts * part_blocks
    first_diag = q0 // bk
    ones = jnp.ones((SUM_ROWS, bk), BF16)

    def scores(c, q2x, q0x, s_ref, smax_ref, first_block=0):
        k0 = pl.multiple_of(c * bk, bk)
        lanes = slice(first_block * LANES, n_blocks * LANES)
        s = lax.dot_general(k_ref[pl.ds(k0, bk), :], q2x[lanes], nt,
                            preferred_element_type=F32)
        rows = []
        for jb in range(bk // LANES):
            tiles = []
            for ib in range(first_block, n_blocks):
                part, within = divmod(ib, part_blocks)
                col, blk = divmod(within, per_map)
                i0 = part * bk + blk * LANES
                t = lax.shift_right_arithmetic(q0x + i0 - k0 - jb * LANES, LOG2_LANES)
                t = jnp.where(t < 0, DIFF_MASK_TILE, jnp.minimum(t, DIFF_CONST_TILE))
                tiles.append(u_ref[t, col])
            rows.append(jnp.concatenate(tiles, axis=1))
        s = s + jnp.concatenate(rows, axis=0)
        s_ref[:, lanes] = s
        smax_ref[:, lanes] = jnp.max(s, axis=0, keepdims=True)

    def accumulate(c, s_ref, smax_ref, first_block=0):
        lanes = slice(first_block * LANES, n_blocks * LANES)
        m_prev = m_sc[:, lanes]
        m_new = jnp.maximum(m_prev, smax_ref[:, lanes])
        m_sc[:, lanes] = m_new
        p = jnp.exp2(s_ref[:, lanes] - m_new).astype(BF16)
        v1 = jnp.concatenate([vt_ref[c], ones], axis=0)
        acc_sc[:, lanes] = (jnp.exp2(m_prev - m_new) * acc_sc[:, lanes]
                            + jnp.dot(v1, p, preferred_element_type=F32))

    @pl.when(i_q == 0)
    def _():
        scores(0, q2, q0, sa_sc, samax_sc)

    def body(i, carry):
        scores(2 * i + 1, q2, q0, sb_sc, sbmax_sc)
        accumulate(2 * i, sa_sc, samax_sc)
        scores(2 * i + 2, q2, q0, sa_sc, samax_sc)
        accumulate(2 * i + 1, sb_sc, sbmax_sc)
        return carry

    lax.fori_loop(0, first_diag // 2, body, 0)

    for dt in range(n_parts // 2):
        ja, jb = 2 * dt, 2 * dt + 1
        scores(first_diag + jb, q2, q0, sb_sc, sbmax_sc, first_block=jb * part_blocks)
        accumulate(first_diag + ja, sa_sc, samax_sc, first_block=ja * part_blocks)
        if jb + 1 < n_parts:
            scores(first_diag + jb + 1, q2, q0, sa_sc, samax_sc, first_block=(jb + 1) * part_blocks)
        else:
            scores(0, q2_next, q0 + bq, sa_sc, samax_sc)
        accumulate(first_diag + jb, sb_sc, sbmax_sc, first_block=jb * part_blocks)

    lam = (jnp.exp(jnp.sum(lq1_ref[...] * lk1_ref[...], keepdims=True))
           - jnp.exp(jnp.sum(lq2_ref[...] * lk2_ref[...], keepdims=True)) + lambda_init)
    accl = acc_sc[...]
    acc = accl[:LANES] / accl[LANES:LANES + 1]
    map0 = jnp.concatenate([acc[:, 2 * part * bk:(2 * part + 1) * bk] for part in range(n_parts)], axis=1)
    map1 = jnp.concatenate([acc[:, (2 * part + 1) * bk:(2 * part + 2) * bk] for part in range(n_parts)], axis=1)
    o = (map0 - lam * map1).T
    y = _rmsnorm_f32(o, sg_ref[...]) * (1.0 - lambda_init)
    o_ref[...] = y.astype(o_ref.dtype)


def _diff_attention(qk, vt, u, lq1, lk1, lq2, lk2, subln, lambda_init, batch, seq, *, bq, bk):
    nh = N_HEADS_B
    nkt = seq // bk
    nq = seq // bq
    vec = lambda a: a.reshape(1, -1)
    small = pl.BlockSpec((1, HEAD_DIM), lambda b, h, i: (0, 0))
    return pl.pallas_call(
        functools.partial(_diff_attn_kernel, bq=bq, bk=bk, lambda_init=lambda_init),
        grid=(batch, nh, seq // bq),
        in_specs=[small, small, small, small,
                  pl.BlockSpec((None, bq, LANES), lambda b, h, i: (b, i, h)),
                  pl.BlockSpec((None, bq, LANES), lambda b, h, i: (b, jnp.minimum(i + 1, nq - 1), h)),
                  pl.BlockSpec((None, seq, LANES), lambda b, h, i: (b, 0, nh + h)),
                  pl.BlockSpec((nkt, LANES, bk), lambda b, h, i: (b, h, 0)),
                  pl.BlockSpec((DIFF_N_TILES, 2, LANES, LANES), lambda b, h, i: (0, h, 0, 0)),
                  pl.BlockSpec((1, LANES), lambda b, h, i: (0, 0))],
        out_specs=pl.BlockSpec((None, bq, LANES), lambda b, h, i: (b, i, h)),
        out_shape=jax.ShapeDtypeStruct((batch, seq, D_MODEL), BF16),
        scratch_shapes=[pltpu.VMEM((1, 2 * bq), F32),
                        pltpu.VMEM((LANES + SUM_ROWS, 2 * bq), F32),
                        pltpu.VMEM((bk, 2 * bq), F32),
                        pltpu.VMEM((1, 2 * bq), F32),
                        pltpu.VMEM((bk, 2 * bq), F32),
                        pltpu.VMEM((1, 2 * bq), F32)],
        compiler_params=_cparams(("parallel", "parallel", "arbitrary")),
        name="diff_attn",
    )(vec(lq1), vec(lk1), vec(lq2), vec(lk2), qk, qk, qk, vt, u, vec(subln))


def kernel(x, p, rel_bias, a_w_qkv, a_w_o, b_w_qkv, b_w_o, b_lambda_q1, b_lambda_k1, b_lambda_q2, b_lambda_k2, b_subln, norm_mix, norm_mlp, w_ff1, w_ff2, norm_ple, w_ple_gate, w_ple_proj, final_norm):
    batch, seq, d = x.shape
    depth = p.shape[0]
    n = batch * seq
    h = x.reshape(n, d)
    p_all = p.reshape(depth, n, p.shape[-1])
    n_mixers = 2

    for i in range(depth):
        j = i // n_mixers
        if i % n_mixers == 0:
            w_a = a_w_qkv[j].astype(BF16)
            bias = _build_dilated_bias(rel_bias)
            outs, lses = [], []
            for g, (_, dilation) in enumerate(DIL_CONFIGS):
                w_g = w_a[:, g * 3 * d:(g + 1) * 3 * d]
                qkv = _qkv_dilated(h, norm_mix[i], w_g, dilation, tm=max(1024, BLOCK * dilation))
                o_g, lse_g = _dilated_attention(qkv, bias, g, dilation, batch, seq)
                outs.append(o_g)
                lses.append(lse_g)
            h = _combine_proj(h, outs, lses, a_w_o[j].astype(BF16))
            proj = None
        else:
            lambda_init = 0.8 - 0.6 * math.exp(-0.3 * i)
            bq, bk = 2048, 512
            w_b = b_w_qkv[j].astype(BF16)
            qk, vt = _qkv_diff(h, norm_mix[i], w_b, bk=bk)
            u = _build_diff_bias(rel_bias)
            o = _diff_attention(qk.reshape(batch, seq, 2 * d), vt, u,
                                b_lambda_q1[j], b_lambda_k1[j], b_lambda_q2[j], b_lambda_k2[j],
                                b_subln[j], lambda_init, batch, seq, bq=bq, bk=bk)
            proj = (o.reshape(n, d), b_w_o[j].astype(BF16))
        h = _mlp_ple(h, p_all, i, norm_mlp[i], w_ff1[i].astype(BF16), w_ff2[i].astype(BF16),
                     norm_ple[i], w_ple_gate[i].astype(BF16), w_ple_proj[i].astype(BF16),
                     final_norm, final_norm=(i == depth - 1), proj=proj)
    return h.reshape(batch, seq, d)
```

```python
import functools
import math

import jax
import jax.numpy as jnp
from jax import lax
from jax.experimental import pallas as pl
from jax.experimental.pallas import tpu as pltpu

F32 = jnp.float32
BF16 = jnp.bfloat16

D_MODEL = 1024
HEAD_DIM = 64
BLOCK = 128
QKV_CHUNK = 512
DIL_QB = 4
DIL_CONFIGS = ((128, 1), (512, 4), (2048, 16))
N_GROUPS = len(DIL_CONFIGS)
N_HEADS_A = D_MODEL // HEAD_DIM
N_HEADS_B = D_MODEL // (2 * HEAD_DIM)
N_BUCKETS = 32
MAX_DISTANCE = 2048
N_BIAS_COLS = 16
EPS = 1e-6
NEG = -1e30
LANES = 128
SUBLANES = 8
LOG2_LANES = 7
QK_SCALE = HEAD_DIM ** -0.5
LOG2E = math.log2(math.e)
LN2 = math.log(2.0)

DIFF_CONST_TILE = (MAX_DISTANCE + LANES - 1) // LANES + 1
DIFF_MASK_TILE = DIFF_CONST_TILE + 1
DIFF_N_TILES = DIFF_MASK_TILE + 1
SUM_ROWS = 16

VMEM_LIMIT = 48 * 1024 * 1024
VMEM_LIMIT_PROJ = 56 * 1024 * 1024


def _cparams(sem, vmem_limit=VMEM_LIMIT):
    return pltpu.CompilerParams(dimension_semantics=sem, vmem_limit_bytes=vmem_limit)


def _rmsnorm_f32(x, g):
    ms = jnp.mean(x * x, axis=-1, keepdims=True)
    return x * lax.rsqrt(ms + EPS) * g


def _rel_bucket(dist):
    n = jnp.maximum(dist, 0)
    max_exact = N_BUCKETS // 2
    nf = jnp.maximum(n, 1).astype(F32)
    large = max_exact + (jnp.log(nf / max_exact) / math.log(MAX_DISTANCE / max_exact)
                         * (N_BUCKETS - max_exact)).astype(jnp.int32)
    large = jnp.minimum(large, N_BUCKETS - 1)
    return jnp.where(n < max_exact, n, large)


def _table_lookup(bucket, tab_ref, col):
    acc = jnp.zeros(bucket.shape, F32)
    for b in range(N_BUCKETS):
        acc = jnp.where(bucket == b, tab_ref[b, col], acc)
    return acc


def _dilated_bias_kernel(tab_ref, o_ref):
    g = pl.program_id(0)
    dilation = jnp.where(g == 0, DIL_CONFIGS[0][1],
                         jnp.where(g == 1, DIL_CONFIGS[1][1], DIL_CONFIGS[2][1]))
    kj = lax.broadcasted_iota(jnp.int32, (2 * BLOCK, BLOCK), 0)
    qi = lax.broadcasted_iota(jnp.int32, (2 * BLOCK, BLOCK), 1)
    sub = qi + BLOCK - kj
    band = (sub >= 0) & (sub <= BLOCK)
    band_first = band & (kj >= BLOCK)
    bucket = _rel_bucket(sub * dilation)
    for c in range(N_BIAS_COLS):
        lanes = slice((c % 2) * BLOCK, (c % 2 + 1) * BLOCK)
        bias = _table_lookup(bucket, tab_ref, c) * LOG2E
        o_ref[0, c // 2, :, lanes] = jnp.where(band, bias, NEG)
        o_ref[1, c // 2, :, lanes] = jnp.where(band_first, bias, NEG)


def _build_dilated_bias(rel_bias):
    return pl.pallas_call(
        _dilated_bias_kernel,
        grid=(N_GROUPS,),
        in_specs=[pl.BlockSpec(memory_space=pltpu.SMEM)],
        out_specs=pl.BlockSpec((None, 2, N_HEADS_A // 2, 2 * BLOCK, 2 * BLOCK),
                               lambda g: (g, 0, 0, 0, 0)),
        out_shape=jax.ShapeDtypeStruct((N_GROUPS, 2, N_HEADS_A // 2, 2 * BLOCK, 2 * BLOCK), F32),
        compiler_params=_cparams(("arbitrary",)),
        name="dilated_bias",
    )(rel_bias)


def _diff_bias_kernel(tab_ref, o_ref):
    t = pl.program_id(0)
    kj = lax.broadcasted_iota(jnp.int32, (LANES, LANES), 0)
    qi = lax.broadcasted_iota(jnp.int32, (LANES, LANES), 1)
    dist = t * LANES + qi - kj
    masked = (dist < 0) | (t == DIFF_MASK_TILE)
    bucket = _rel_bucket(dist)
    init = jnp.where(masked, NEG, 0.0)
    for c in range(N_BIAS_COLS):
        o_ref[c] = init
    live = jnp.where(masked, -1, bucket)

    def body(b, carry):
        hit = live == b
        for c in range(N_BIAS_COLS):
            o_ref[c] = jnp.where(hit, tab_ref[b, c] * LOG2E, o_ref[c])
        return carry

    lax.fori_loop(jnp.min(bucket), jnp.max(bucket) + 1, body, 0)


def _build_diff_bias(rel_bias):
    return pl.pallas_call(
        _diff_bias_kernel,
        grid=(DIFF_N_TILES,),
        in_specs=[pl.BlockSpec(memory_space=pltpu.SMEM)],
        out_specs=pl.BlockSpec((None, N_BIAS_COLS, LANES, LANES), lambda t: (t, 0, 0, 0)),
        out_shape=jax.ShapeDtypeStruct((DIFF_N_TILES, N_BIAS_COLS, LANES, LANES), F32),
        compiler_params=_cparams(("arbitrary",)),
        name="diff_bias",
    )(rel_bias)


def _qkv_dilated_kernel(*refs, dilation, n_slabs, chunk, whole):
    n_x = len(refs) - 4
    x_refs = refs[:n_x]
    g_ref, w_ref, o_ref, xn_sc = refs[n_x:]
    tm, d_model = xn_sc.shape
    rows = tm // dilation

    def normalise(r, a, b):
        if n_slabs is None:
            xs = [x_refs[0][a:b, :]]
        else:
            xs = [x_ref[pl.ds(r + (a - r * rows) * dilation, b - a, stride=dilation), :]
                  for x_ref in x_refs]
        slab_w = d_model // len(xs)
        sq = xs[0] * xs[0]
        for x in xs[1:]:
            sq = sq + x * x
        scale = lax.rsqrt(jnp.sum(sq, axis=-1, keepdims=True) * (1.0 / d_model) + EPS)
        for s, x in enumerate(xs):
            cols = slice(s * slab_w, (s + 1) * slab_w)
            xn_sc[a:b, cols] = (x * scale * g_ref[:, cols]).astype(BF16)

    def store(c, y, lo, r, a, b, col0):
        if c < 2:
            val = y[a - lo:b - lo]
            if c == 0:
                val = val * (QK_SCALE * LOG2E)
            o_ref[a - r * rows:b - r * rows, col0:col0 + d_model] = val.astype(o_ref.dtype)
            return
        for blk in range((b - a) // BLOCK):
            y0 = a - lo + blk * BLOCK
            s0 = a - r * rows + blk * BLOCK
            for hp in range(d_model // LANES):
                tile = y[y0:y0 + BLOCK, hp * LANES:(hp + 1) * LANES]
                o_ref[s0:s0 + BLOCK, col0 + hp * LANES:col0 + (hp + 1) * LANES] = (
                    tile.T.astype(o_ref.dtype))

    def run(comps, with_norm):
        for ch in range(tm // chunk):
            lo, hi = ch * chunk, (ch + 1) * chunk
            pieces = [(r, max(lo, r * rows), min(hi, (r + 1) * rows)) for r in range(dilation)
                      if max(lo, r * rows) < min(hi, (r + 1) * rows)]
            if with_norm:
                for r, a, b in pieces:
                    normalise(r, a, b)
            xn = xn_sc[lo:hi, :]
            for c in comps:
                w = w_ref[:, c * d_model:(c + 1) * d_model] if whole else w_ref[...]
                y = jnp.dot(xn, w, preferred_element_type=F32)
                for r, a, b in pieces:
                    store(c, y, lo, r, a, b, ((c * dilation if whole else 0) + r) * d_model)

    if whole:
        run((0, 1, 2), True)
    else:
        j = pl.program_id(1)
        for c in range(3):
            pl.when(j == c)(functools.partial(run, (c,), c == 0))


def _qkv_dilated(x, g, w, dilation, *, tm):
    n, d = x.shape
    assert w.shape == (d, 3 * d)
    rows = tm // dilation
    assert tm % dilation == 0 and rows % BLOCK == 0
    if dilation == 1:
        n_slabs = None
        x_in = [x]
        x_specs = [pl.BlockSpec((tm, d), lambda i, j: (i, 0))]
    else:
        n_slabs = d // LANES
        x_in = [x] * n_slabs
        x_specs = [pl.BlockSpec((tm, LANES), functools.partial(lambda i, j, s: (i, s), s=s))
                   for s in range(n_slabs)]
    out_shape = jax.ShapeDtypeStruct((n // dilation, 3 * dilation * d), BF16)
    whole_bytes = 2 * (tm * 3 * d * 2) + 2 * (d * 3 * d * 2) + 2 * (tm * d * 4) + 4 * QKV_CHUNK * d * 4
    assert tm % QKV_CHUNK == 0 and (QKV_CHUNK % rows == 0 or rows % QKV_CHUNK == 0)
    if whole_bytes <= VMEM_LIMIT:
        return pl.pallas_call(
            functools.partial(_qkv_dilated_kernel, dilation=dilation, n_slabs=n_slabs,
                              chunk=QKV_CHUNK, whole=True),
            grid=(n // tm, 1),
            in_specs=x_specs + [pl.BlockSpec((1, d), lambda i, j: (0, 0)),
                                pl.BlockSpec((d, 3 * d), lambda i, j: (0, 0))],
            out_specs=pl.BlockSpec((rows, 3 * dilation * d), lambda i, j: (i, 0)),
            out_shape=out_shape,
            scratch_shapes=[pltpu.VMEM((tm, d), BF16)],
            compiler_params=_cparams(("parallel", "arbitrary")),
            name=f"qkv_d{dilation}",
        )(*x_in, g.reshape(1, d), w)
    return pl.pallas_call(
        functools.partial(_qkv_dilated_kernel, dilation=dilation, n_slabs=n_slabs,
                          chunk=QKV_CHUNK, whole=False),
        grid=(n // tm, 3),
        in_specs=x_specs + [pl.BlockSpec((1, d), lambda i, j: (0, 0)),
                            pl.BlockSpec((d, d), lambda i, j: (0, j))],
        out_specs=pl.BlockSpec((rows, dilation * d), lambda i, j: (i, j)),
        out_shape=out_shape,
        scratch_shapes=[pltpu.VMEM((tm, d), BF16)],
        compiler_params=_cparams(("parallel", "arbitrary")),
        name=f"qkv_d{dilation}",
    )(*x_in, g.reshape(1, d), w)


def _qkv_diff_kernel(x_ref, g_ref, w_ref, qk_ref, vt_ref, *, bk):
    tm, d = x_ref.shape
    nt = (((1,), (1,)), ((), ()))
    for c in range(tm // bk):
        rows = slice(c * bk, (c + 1) * bk)
        xn = _rmsnorm_f32(x_ref[rows, :], g_ref[...]).astype(BF16)
        q = jnp.dot(xn, w_ref[0], preferred_element_type=F32) * (QK_SCALE * LOG2E)
        qk_ref[rows, :d] = q.astype(qk_ref.dtype)
        qk_ref[rows, d:] = jnp.dot(xn, w_ref[1], preferred_element_type=F32).astype(qk_ref.dtype)
        vt_ref[c] = lax.dot_general(w_ref[2], xn, nt,
                                    preferred_element_type=F32).astype(vt_ref.dtype)


def _qkv_diff(x, g, w_qkv, *, bk, tm=1024):
    n, d = x.shape
    w3 = jnp.stack([w_qkv[:, :d], w_qkv[:, d:2 * d], w_qkv[:, 2 * d:].T])
    return pl.pallas_call(
        functools.partial(_qkv_diff_kernel, bk=bk),
        grid=(n // tm,),
        in_specs=[pl.BlockSpec((tm, d), lambda i: (i, 0)),
                  pl.BlockSpec((1, d), lambda i: (0, 0)),
                  pl.BlockSpec((3, d, d), lambda i: (0, 0, 0))],
        out_specs=[pl.BlockSpec((tm, 2 * d), lambda i: (i, 0)),
                   pl.BlockSpec((tm // bk, d, bk), lambda i: (i, 0, 0))],
        out_shape=[jax.ShapeDtypeStruct((n, 2 * d), BF16),
                   jax.ShapeDtypeStruct((n // bk, d, bk), BF16)],
        compiler_params=_cparams(("parallel",)),
        name="qkv_diff",
    )(x, g.reshape(1, d), w3)


def _dilated_kernel(q_ref, kp_ref, kc_ref, vtp_ref, vtc_ref, bias_ref, o_ref, lse_ref,
                    s_sc, m_sc, pv_sc):
    first_step = (pl.program_id(2) == 0).astype(jnp.int32)
    lane = lax.broadcasted_iota(jnp.int32, (BLOCK, LANES), 1)
    row = lax.broadcasted_iota(jnp.int32, (BLOCK, LANES), 0)
    head_row = lax.broadcasted_iota(jnp.int32, (N_HEADS_A, BLOCK), 0)
    lo_row = row < HEAD_DIM
    mask_lo = jnp.where(lane < HEAD_DIM, 1.0, 0.0).astype(BF16)
    mask_hi = jnp.where(lane < HEAD_DIM, 0.0, 1.0).astype(BF16)
    nt = (((1,), (1,)), ((), ()))
    ones = jnp.ones((SUM_ROWS, 2 * BLOCK), BF16)
    n_pairs = N_HEADS_A // 2
    items = [(qb, hp) for qb in range(DIL_QB) for hp in range(n_pairs)]

    def rows(qb):
        return slice(qb * BLOCK, (qb + 1) * BLOCK)

    def prev_cur(prev_ref, cur_ref, qb, sl):
        prev = prev_ref[:, sl] if qb == 0 else cur_ref[rows(qb - 1), sl]
        return prev, cur_ref[rows(qb), sl]

    for it, (qb, hp) in enumerate(items):
        sl = slice(hp * LANES, (hp + 1) * LANES)
        q = q_ref[rows(qb), sl]
        q2 = jnp.concatenate([q * mask_lo, q * mask_hi], axis=0)
        k = jnp.concatenate(prev_cur(kp_ref, kc_ref, qb, sl), axis=0)
        s = lax.dot_general(k, q2, nt, preferred_element_type=F32)
        s = s + bias_ref[first_step if qb == 0 else 0, hp]
        s_sc[it] = s
        m_sc[it] = jnp.max(s, axis=0, keepdims=True)
    for it, (qb, hp) in enumerate(items):
        sl = slice(hp * LANES, (hp + 1) * LANES)
        p = jnp.exp2(s_sc[it] - m_sc[it]).astype(BF16)
        vt = jnp.concatenate(prev_cur(vtp_ref, vtc_ref, qb, sl), axis=1)
        pv_sc[it] = jnp.dot(jnp.concatenate([vt, ones], axis=0), p,
                            preferred_element_type=F32)
    for qb in range(DIL_QB):
        lse_t = jnp.zeros((N_HEADS_A, BLOCK), F32)
        for hp in range(n_pairs):
            it = qb * n_pairs + hp
            l = pv_sc[it, LANES:LANES + 1]
            acc = pv_sc[it, :LANES] / l
            o_t = jnp.where(lo_row, acc[:, :BLOCK], acc[:, BLOCK:])
            o_ref[rows(qb), hp * LANES:(hp + 1) * LANES] = o_t.T.astype(o_ref.dtype)
            lse = (m_sc[it] + jnp.log2(l)) * LN2
            lse_t = jnp.where(head_row == 2 * hp, lse[:, :BLOCK], lse_t)
            lse_t = jnp.where(head_row == 2 * hp + 1, lse[:, BLOCK:], lse_t)
        lse_full = jnp.concatenate([lse_t, jnp.zeros((BLOCK - N_HEADS_A, BLOCK), F32)], axis=0)
        lse_ref[rows(qb), :] = lse_full.T


def _dilated_attention(qkv, bias, group, dilation, batch, seq):
    assert seq % (dilation * BLOCK * DIL_QB) == 0
    sub_len = seq // dilation
    steps = sub_len // (BLOCK * DIL_QB)
    qkv_v = qkv.reshape(batch, sub_len, 3 * dilation * D_MODEL)

    def col(c):
        return lambda b, r, n: (b, n, c * dilation + r)

    def col_prev(c):
        return lambda b, r, n: (b, jnp.maximum(n * DIL_QB - 1, 0), c * dilation + r)

    blk = (None, DIL_QB * BLOCK, D_MODEL)
    blk_prev = (None, BLOCK, D_MODEL)
    n_items = DIL_QB * N_HEADS_A // 2
    o, lse = pl.pallas_call(
        _dilated_kernel,
        grid=(batch, dilation, steps),
        in_specs=[pl.BlockSpec(blk, col(0)),
                  pl.BlockSpec(blk_prev, col_prev(1)),
                  pl.BlockSpec(blk, col(1)),
                  pl.BlockSpec(blk_prev, col_prev(2)),
                  pl.BlockSpec(blk, col(2)),
                  pl.BlockSpec((None, 2, N_HEADS_A // 2, 2 * BLOCK, 2 * BLOCK),
                               lambda b, r, n: (group, 0, 0, 0, 0))],
        out_specs=[pl.BlockSpec(blk, lambda b, r, n: (b, n, r)),
                   pl.BlockSpec((None, DIL_QB * BLOCK, LANES), lambda b, r, n: (b, n, r))],
        out_shape=[jax.ShapeDtypeStruct((batch, sub_len, dilation * D_MODEL), BF16),
                   jax.ShapeDtypeStruct((batch, sub_len, dilation * LANES), F32)],
        scratch_shapes=[pltpu.VMEM((n_items, 2 * BLOCK, 2 * BLOCK), F32),
                        pltpu.VMEM((n_items, 1, 2 * BLOCK), F32),
                        pltpu.VMEM((n_items, LANES + SUM_ROWS, 2 * BLOCK), F32)],
        compiler_params=_cparams(("parallel", "parallel", "arbitrary")),
        name=f"dilated_attn_d{dilation}",
    )(qkv_v, qkv_v, qkv_v, qkv_v, qkv_v, bias)
    return (o.reshape(batch * sub_len, dilation * D_MODEL),
            lse.reshape(batch * sub_len, dilation * LANES))


def _combine_proj_kernel(h_ref, o0_ref, o1_ref, o2_ref, l0_ref, l1_ref, l2_ref,
                         e_ref, w_ref, out_ref, lse_sc, o_sc):
    o_refs = [o0_ref, o1_ref, o2_ref]
    l_refs = [l0_ref, l1_ref, l2_ref]
    tm, d = h_ref.shape
    n_slabs = d // LANES
    for g, (_, dilation) in enumerate(DIL_CONFIGS):
        rows = tm // dilation
        for r in range(dilation):
            dst = pl.ds(r, rows, stride=dilation) if dilation > 1 else slice(None)
            lse_sc[g, dst, :] = l_refs[g][:, r * LANES:(r + 1) * LANES]
            for c in range(n_slabs):
                o_sc[g, c, dst, :] = o_refs[g][:, r * d + c * LANES:r * d + (c + 1) * LANES].astype(F32)
    lses = [lse_sc[g] for g in range(N_GROUPS)]
    mx = jnp.maximum(jnp.maximum(lses[0], lses[1]), lses[2])
    ws = [jnp.exp(l - mx) for l in lses]
    tot = ws[0] + ws[1] + ws[2]
    head_lane = lax.broadcasted_iota(jnp.int32, (tm, LANES), 1) < N_HEADS_A
    packed = None
    for g in range(N_GROUPS):
        a = jnp.where(head_lane, ws[g] / tot, 0.0)
        a_hi = a.astype(BF16).astype(F32)
        for part, piece in enumerate((a_hi, a - a_hi)):
            shift = (part * N_GROUPS + g) * N_HEADS_A
            moved = piece if shift == 0 else pltpu.roll(piece, shift, axis=1)
            packed = moved if packed is None else packed + moved
    ae_all = jnp.dot(packed.astype(BF16), e_ref[...], preferred_element_type=F32)
    aes = [ae_all[:, g * d:(g + 1) * d] for g in range(N_GROUPS)]
    slabs = []
    for c in range(n_slabs):
        cols = slice(c * LANES, (c + 1) * LANES)
        slabs.append(aes[0][:, cols] * o_sc[0, c] + aes[1][:, cols] * o_sc[1, c]
                     + aes[2][:, cols] * o_sc[2, c])
    o = jnp.concatenate(slabs, axis=1).astype(BF16)
    out_ref[...] = h_ref[...] + jnp.dot(o, w_ref[...], preferred_element_type=F32)


def _combine_proj(h, outs, lses, w, *, tm=512):
    n, d = h.shape
    rows = jnp.arange(LANES, dtype=jnp.int32)
    cols = jnp.arange(N_GROUPS * d, dtype=jnp.int32)
    row_group = (rows // N_HEADS_A) % N_GROUPS
    row_head = rows % N_HEADS_A
    row_used = rows < 2 * N_GROUPS * N_HEADS_A
    expand = (row_used[:, None] & (row_group[:, None] == (cols // d)[None, :])
              & (row_head[:, None] == ((cols % d) // HEAD_DIM)[None, :])).astype(BF16)
    row = lambda i: (i, 0)
    full = lambda i: (0, 0)
    dils = [dilation for _, dilation in DIL_CONFIGS]
    assert all(tm % (16 * dilation) == 0 for dilation in dils)
    return pl.pallas_call(
        _combine_proj_kernel,
        grid=(n // tm,),
        in_specs=[pl.BlockSpec((tm, d), row)]
                 + [pl.BlockSpec((tm // dilation, dilation * d), row) for dilation in dils]
                 + [pl.BlockSpec((tm // dilation, dilation * LANES), row) for dilation in dils]
                 + [pl.BlockSpec((LANES, N_GROUPS * d), full), pl.BlockSpec((d, d), full)],
        out_specs=pl.BlockSpec((tm, d), row),
        out_shape=jax.ShapeDtypeStruct((n, d), F32),
        scratch_shapes=[pltpu.VMEM((N_GROUPS, tm, LANES), F32),
                        pltpu.VMEM((N_GROUPS, d // LANES, tm, LANES), F32)],
        compiler_params=_cparams(("parallel",)),
        name="combine_proj",
    )(h, *outs, *lses, expand, w)


def _mlp_ple_kernel(*refs, final_norm, with_proj):
    if with_proj:
        h_ref, o_ref, wo_ref = refs[:3]
        refs = refs[3:]
    else:
        h_ref = refs[0]
        refs = refs[1:]
    p_ref, g_ref, w1_ref, w2_ref, gp_ref, wg_ref, wp_ref, fg_ref, out_ref, xn_ref, acc_ref = refs
    f = pl.program_id(1)

    @pl.when(f == 0)
    def _():
        if with_proj:
            h0 = h_ref[...] + jnp.dot(o_ref[...], wo_ref[...], preferred_element_type=F32)
            out_ref[...] = h0
        else:
            h0 = h_ref[...]
        xn_ref[...] = _rmsnorm_f32(h0, g_ref[...]).astype(BF16)
        acc_ref[...] = jnp.zeros_like(acc_ref)

    a = jnp.dot(xn_ref[...], w1_ref[...], preferred_element_type=F32)
    a = jnp.maximum(a, 0.0)
    a = (a * a).astype(BF16)
    acc_ref[...] += jnp.dot(a, w2_ref[...], preferred_element_type=F32)

    @pl.when(f == pl.num_programs(1) - 1)
    def _():
        x = (out_ref[...] if with_proj else h_ref[...]) + acc_ref[...]
        xn = _rmsnorm_f32(x, gp_ref[...]).astype(BF16)
        gate = jax.nn.sigmoid(jnp.dot(xn, wg_ref[...], preferred_element_type=F32))
        proj = jnp.dot(p_ref[...].astype(BF16), wp_ref[...], preferred_element_type=F32)
        y = x + gate * proj
        if final_norm:
            y = _rmsnorm_f32(y, fg_ref[...])
        out_ref[...] = y


def _mlp_ple(h, p_all, layer, g, w1, w2, gp, wg, wp, fg, *, final_norm, proj=None, tm=1024, tf=1024):
    n, d = h.shape
    dff = w1.shape[1]
    pd = p_all.shape[-1]
    vec = lambda i, f: (0, 0)
    row = lambda i, f: (i, 0)
    lead_specs = [pl.BlockSpec((tm, d), row)]
    lead_args = [h]
    if proj is not None:
        o, wo = proj
        lead_specs += [pl.BlockSpec((tm, o.shape[1]), row), pl.BlockSpec(wo.shape, vec)]
        lead_args += [o, wo]
    return pl.pallas_call(
        functools.partial(_mlp_ple_kernel, final_norm=final_norm, with_proj=proj is not None),
        grid=(n // tm, dff // tf),
        in_specs=lead_specs + [
                  pl.BlockSpec((None, tm, pd), lambda i, f: (layer, i, 0)),
                  pl.BlockSpec((1, d), vec),
                  pl.BlockSpec((d, tf), lambda i, f: (0, f)),
                  pl.BlockSpec((tf, d), lambda i, f: (f, 0)),
                  pl.BlockSpec((1, d), vec),
                  pl.BlockSpec((d, d), vec),
                  pl.BlockSpec((pd, d), vec),
                  pl.BlockSpec((1, d), vec)],
        out_specs=pl.BlockSpec((tm, d), lambda i, f: (i, 0)),
        out_shape=jax.ShapeDtypeStruct((n, d), F32),
        scratch_shapes=[pltpu.VMEM((tm, d), BF16), pltpu.VMEM((tm, d), F32)],
        compiler_params=_cparams(("parallel", "arbitrary"),
                                 vmem_limit=VMEM_LIMIT_PROJ if proj is not None else VMEM_LIMIT),
        name="mlp_ple",
    )(*lead_args, p_all, g.reshape(1, d), w1, w2, gp.reshape(1, d), wg, wp, fg.reshape(1, d))


def _diff_attn_kernel(lq1_ref, lk1_ref, lq2_ref, lk2_ref, q_ref, qn_ref, k_ref, vt_ref, u_ref, sg_ref,
                      o_ref, m_sc, acc_sc, sa_sc, samax_sc, sb_sc, sbmax_sc, *, bq, bk, lambda_init):
    n_parts = bq // bk
    assert bq == n_parts * bk and n_parts % 2 == 0
    i_q = pl.program_id(2)
    q0 = i_q * bq
    nt = (((1,), (1,)), ((), ()))
    lane = lax.broadcasted_iota(jnp.int32, (bq, LANES), 1)

    def query_columns(ref):
        qf = ref[...].astype(F32)
        a = jnp.where(lane < HEAD_DIM, qf, 0.0)
        b = jnp.where(lane >= HEAD_DIM, qf, 0.0)
        pieces = []
        for part in range(n_parts):
            pieces += [a[part * bk:(part + 1) * bk], b[part * bk:(part + 1) * bk]]
        return jnp.concatenate(pieces, axis=0).astype(BF16)

    q2 = query_columns(q_ref)
    q2_next = query_columns(qn_ref)

    m_sc[...] = jnp.full(m_sc.shape, NEG, F32)
    acc_sc[...] = jnp.zeros(acc_sc.shape, F32)

    per_map = bk // LANES
    part_blocks = 2 * per_map
    n_blocks = n_parts * part_blocks
    first_diag = q0 // bk
    ones = jnp.ones((SUM_ROWS, bk), BF16)

    def scores(c, q2x, q0x, s_ref, smax_ref, first_block=0):
        k0 = pl.multiple_of(c * bk, bk)
        lanes = slice(first_block * LANES, n_blocks * LANES)
        s = lax.dot_general(k_ref[pl.ds(k0, bk), :], q2x[lanes], nt,
                            preferred_element_type=F32)
        rows = []
        for jb in range(bk // LANES):
            tiles = []
            for ib in range(first_block, n_blocks):
                part, within = divmod(ib, part_blocks)
                col, blk = divmod(within, per_map)
                i0 = part * bk + blk * LANES
                t = lax.shift_right_arithmetic(q0x + i0 - k0 - jb * LANES, LOG2_LANES)
                t = jnp.where(t < 0, DIFF_MASK_TILE, jnp.minimum(t, DIFF_CONST_TILE))
                tiles.append(u_ref[t, col])
            rows.append(jnp.concatenate(tiles, axis=1))
        s = s + jnp.concatenate(rows, axis=0)
        s_ref[:, lanes] = s
        smax_ref[:, lanes] = jnp.max(s, axis=0, keepdims=True)

    def accumulate(c, s_ref, smax_ref, first_block=0):
        lanes = slice(first_block * LANES, n_blocks * LANES)
        m_prev = m_sc[:, lanes]
        m_new = jnp.maximum(m_prev, smax_ref[:, lanes])
        m_sc[:, lanes] = m_new
        p = jnp.exp2(s_ref[:, lanes] - m_new).astype(BF16)
        v1 = jnp.concatenate([vt_ref[c], ones], axis=0)
        acc_sc[:, lanes] = (jnp.exp2(m_prev - m_new) * acc_sc[:, lanes]
                            + jnp.dot(v1, p, preferred_element_type=F32))

    @pl.when(i_q == 0)
    def _():
        scores(0, q2, q0, sa_sc, samax_sc)

    def body(i, carry):
        scores(2 * i + 1, q2, q0, sb_sc, sbmax_sc)
        accumulate(2 * i, sa_sc, samax_sc)
        scores(2 * i + 2, q2, q0, sa_sc, samax_sc)
        accumulate(2 * i + 1, sb_sc, sbmax_sc)
        return carry

    lax.fori_loop(0, first_diag // 2, body, 0)

    for dt in range(n_parts // 2):
        ja, jb = 2 * dt, 2 * dt + 1
        scores(first_diag + jb, q2, q0, sb_sc, sbmax_sc, first_block=jb * part_blocks)
        accumulate(first_diag + ja, sa_sc, samax_sc, first_block=ja * part_blocks)
        if jb + 1 < n_parts:
            scores(first_diag + jb + 1, q2, q0, sa_sc, samax_sc, first_block=(jb + 1) * part_blocks)
        else:
            scores(0, q2_next, q0 + bq, sa_sc, samax_sc)
        accumulate(first_diag + jb, sb_sc, sbmax_sc, first_block=jb * part_blocks)

    lam = (jnp.exp(jnp.sum(lq1_ref[...] * lk1_ref[...], keepdims=True))
           - jnp.exp(jnp.sum(lq2_ref[...] * lk2_ref[...], keepdims=True)) + lambda_init)
    accl = acc_sc[...]
    acc = accl[:LANES] / accl[LANES:LANES + 1]
    map0 = jnp.concatenate([acc[:, 2 * part * bk:(2 * part + 1) * bk] for part in range(n_parts)], axis=1)
    map1 = jnp.concatenate([acc[:, (2 * part + 1) * bk:(2 * part + 2) * bk] for part in range(n_parts)], axis=1)
    o = (map0 - lam * map1).T
    y = _rmsnorm_f32(o, sg_ref[...]) * (1.0 - lambda_init)
    o_ref[...] = y.astype(o_ref.dtype)


def _diff_attention(qk, vt, u, lq1, lk1, lq2, lk2, subln, lambda_init, batch, seq, *, bq, bk):
    nh = N_HEADS_B
    nkt = seq // bk
    nq = seq // bq
    vec = lambda a: a.reshape(1, -1)
    small = pl.BlockSpec((1, HEAD_DIM), lambda b, h, i: (0, 0))
    return pl.pallas_call(
        functools.partial(_diff_attn_kernel, bq=bq, bk=bk, lambda_init=lambda_init),
        grid=(batch, nh, seq // bq),
        in_specs=[small, small, small, small,
                  pl.BlockSpec((None, bq, LANES), lambda b, h, i: (b, i, h)),
                  pl.BlockSpec((None, bq, LANES), lambda b, h, i: (b, jnp.minimum(i + 1, nq - 1), h)),
                  pl.BlockSpec((None, seq, LANES), lambda b, h, i: (b, 0, nh + h)),
                  pl.BlockSpec((nkt, LANES, bk), lambda b, h, i: (b, h, 0)),
                  pl.BlockSpec((DIFF_N_TILES, 2, LANES, LANES), lambda b, h, i: (0, h, 0, 0)),
                  pl.BlockSpec((1, LANES), lambda b, h, i: (0, 0))],
        out_specs=pl.BlockSpec((None, bq, LANES), lambda b, h, i: (b, i, h)),
        out_shape=jax.ShapeDtypeStruct((batch, seq, D_MODEL), BF16),
        scratch_shapes=[pltpu.VMEM((1, 2 * bq), F32),
                        pltpu.VMEM((LANES + SUM_ROWS, 2 * bq), F32),
                        pltpu.VMEM((bk, 2 * bq), F32),
                        pltpu.VMEM((1, 2 * bq), F32),
                        pltpu.VMEM((bk, 2 * bq), F32),
                        pltpu.VMEM((1, 2 * bq), F32)],
        compiler_params=_cparams(("parallel", "parallel", "arbitrary")),
        name="diff_attn",
    )(vec(lq1), vec(lk1), vec(lq2), vec(lk2), qk, qk, qk, vt, u, vec(subln))


def kernel(x, p, rel_bias, a_w_qkv, a_w_o, b_w_qkv, b_w_o, b_lambda_q1, b_lambda_k1, b_lambda_q2, b_lambda_k2, b_subln, norm_mix, norm_mlp, w_ff1, w_ff2, norm_ple, w_ple_gate, w_ple_proj, final_norm):
    batch, seq, d = x.shape
    depth = p.shape[0]
    n = batch * seq
    h = x.reshape(n, d)
    p_all = p.reshape(depth, n, p.shape[-1])
    n_mixers = 2

    for i in range(depth):
        j = i // n_mixers
        if i % n_mixers == 0:
            w_a = a_w_qkv[j].astype(BF16)
            bias = _build_dilated_bias(rel_bias)
            outs, lses = [], []
            for g, (_, dilation) in enumerate(DIL_CONFIGS):
                w_g = w_a[:, g * 3 * d:(g + 1) * 3 * d]
                qkv = _qkv_dilated(h, norm_mix[i], w_g, dilation, tm=max(1024, BLOCK * dilation))
                o_g, lse_g = _dilated_attention(qkv, bias, g, dilation, batch, seq)
                outs.append(o_g)
                lses.append(lse_g)
            h = _combine_proj(h, outs, lses, a_w_o[j].astype(BF16))
            proj = None
        else:
            lambda_init = 0.8 - 0.6 * math.exp(-0.3 * i)
            bq, bk = 2048, 512
            w_b = b_w_qkv[j].astype(BF16)
            qk, vt = _qkv_diff(h, norm_mix[i], w_b, bk=bk)
            u = _build_diff_bias(rel_bias)
            o = _diff_attention(qk.reshape(batch, seq, 2 * d), vt, u,
                                b_lambda_q1[j], b_lambda_k1[j], b_lambda_q2[j], b_lambda_k2[j],
                                b_subln[j], lambda_init, batch, seq, bq=bq, bk=bk)
            proj = (o.reshape(n, d), b_w_o[j].astype(BF16))
        h = _mlp_ple(h, p_all, i, norm_mlp[i], w_ff1[i].astype(BF16), w_ff2[i].astype(BF16),
                     norm_ple[i], w_ple_gate[i].astype(BF16), w_ple_proj[i].astype(BF16),
                     final_norm, final_norm=(i == depth - 1), proj=proj)
    return h.reshape(batch, seq, d)
```

```python
import functools
import math

import jax
import jax.numpy as jnp
from jax import lax
from jax.experimental import pallas as pl
from jax.experimental.pallas import tpu as pltpu

F32 = jnp.float32
BF16 = jnp.bfloat16

D_MODEL = 1024
HEAD_DIM = 64
BLOCK = 128
QKV_CHUNK = 512
MLP_CHUNK = 512
DIL_QB = 4
DIL_CONFIGS = ((128, 1), (512, 4), (2048, 16))
N_GROUPS = len(DIL_CONFIGS)
N_HEADS_A = D_MODEL // HEAD_DIM
N_HEADS_B = D_MODEL // (2 * HEAD_DIM)
N_BUCKETS = 32
MAX_DISTANCE = 2048
N_BIAS_COLS = 16
EPS = 1e-6
NEG = -1e30
LANES = 128
SUBLANES = 8
LOG2_LANES = 7
QK_SCALE = HEAD_DIM ** -0.5
LOG2E = math.log2(math.e)
LN2 = math.log(2.0)

DIFF_CONST_TILE = (MAX_DISTANCE + LANES - 1) // LANES + 1
DIFF_MASK_TILE = DIFF_CONST_TILE + 1
DIFF_N_TILES = DIFF_MASK_TILE + 1
SUM_ROWS = 16

VMEM_LIMIT = 48 * 1024 * 1024
VMEM_LIMIT_PROJ = 56 * 1024 * 1024


def _cparams(sem, vmem_limit=VMEM_LIMIT):
    return pltpu.CompilerParams(dimension_semantics=sem, vmem_limit_bytes=vmem_limit)


def _rmsnorm_f32(x, g):
    ms = jnp.mean(x * x, axis=-1, keepdims=True)
    return x * lax.rsqrt(ms + EPS) * g


def _rel_bucket(dist):
    n = jnp.maximum(dist, 0)
    max_exact = N_BUCKETS // 2
    nf = jnp.maximum(n, 1).astype(F32)
    large = max_exact + (jnp.log(nf / max_exact) / math.log(MAX_DISTANCE / max_exact)
                         * (N_BUCKETS - max_exact)).astype(jnp.int32)
    large = jnp.minimum(large, N_BUCKETS - 1)
    return jnp.where(n < max_exact, n, large)


def _table_lookup(bucket, tab_ref, col):
    acc = jnp.zeros(bucket.shape, F32)
    for b in range(N_BUCKETS):
        acc = jnp.where(bucket == b, tab_ref[b, col], acc)
    return acc


def _dilated_bias_kernel(tab_ref, o_ref):
    g = pl.program_id(0)
    dilation = jnp.where(g == 0, DIL_CONFIGS[0][1],
                         jnp.where(g == 1, DIL_CONFIGS[1][1], DIL_CONFIGS[2][1]))
    kj = lax.broadcasted_iota(jnp.int32, (2 * BLOCK, BLOCK), 0)
    qi = lax.broadcasted_iota(jnp.int32, (2 * BLOCK, BLOCK), 1)
    sub = qi + BLOCK - kj
    band = (sub >= 0) & (sub <= BLOCK)
    band_first = band & (kj >= BLOCK)
    bucket = _rel_bucket(sub * dilation)
    for c in range(N_BIAS_COLS):
        lanes = slice((c % 2) * BLOCK, (c % 2 + 1) * BLOCK)
        bias = _table_lookup(bucket, tab_ref, c) * LOG2E
        o_ref[0, c // 2, :, lanes] = jnp.where(band, bias, NEG)
        o_ref[1, c // 2, :, lanes] = jnp.where(band_first, bias, NEG)


def _build_dilated_bias(rel_bias):
    return pl.pallas_call(
        _dilated_bias_kernel,
        grid=(N_GROUPS,),
        in_specs=[pl.BlockSpec(memory_space=pltpu.SMEM)],
        out_specs=pl.BlockSpec((None, 2, N_HEADS_A // 2, 2 * BLOCK, 2 * BLOCK),
                               lambda g: (g, 0, 0, 0, 0)),
        out_shape=jax.ShapeDtypeStruct((N_GROUPS, 2, N_HEADS_A // 2, 2 * BLOCK, 2 * BLOCK), F32),
        compiler_params=_cparams(("arbitrary",)),
        name="dilated_bias",
    )(rel_bias)


def _diff_bias_kernel(tab_ref, o_ref):
    t = pl.program_id(0)
    kj = lax.broadcasted_iota(jnp.int32, (LANES, LANES), 0)
    qi = lax.broadcasted_iota(jnp.int32, (LANES, LANES), 1)
    dist = t * LANES + qi - kj
    masked = (dist < 0) | (t == DIFF_MASK_TILE)
    bucket = _rel_bucket(dist)
    init = jnp.where(masked, NEG, 0.0)
    for c in range(N_BIAS_COLS):
        o_ref[c] = init
    live = jnp.where(masked, -1, bucket)

    def body(b, carry):
        hit = live == b
        for c in range(N_BIAS_COLS):
            o_ref[c] = jnp.where(hit, tab_ref[b, c] * LOG2E, o_ref[c])
        return carry

    lax.fori_loop(jnp.min(bucket), jnp.max(bucket) + 1, body, 0)


def _build_diff_bias(rel_bias):
    return pl.pallas_call(
        _diff_bias_kernel,
        grid=(DIFF_N_TILES,),
        in_specs=[pl.BlockSpec(memory_space=pltpu.SMEM)],
        out_specs=pl.BlockSpec((None, N_BIAS_COLS, LANES, LANES), lambda t: (t, 0, 0, 0)),
        out_shape=jax.ShapeDtypeStruct((DIFF_N_TILES, N_BIAS_COLS, LANES, LANES), F32),
        compiler_params=_cparams(("arbitrary",)),
        name="diff_bias",
    )(rel_bias)


def _qkv_dilated_kernel(*refs, dilation, n_slabs, chunk, whole):
    n_x = len(refs) - 4
    x_refs = refs[:n_x]
    g_ref, w_ref, o_ref, xn_sc = refs[n_x:]
    tm, d_model = xn_sc.shape
    rows = tm // dilation

    def normalise(r, a, b):
        if n_slabs is None:
            xs = [x_refs[0][a:b, :]]
        else:
            xs = [x_ref[pl.ds(r + (a - r * rows) * dilation, b - a, stride=dilation), :]
                  for x_ref in x_refs]
        slab_w = d_model // len(xs)
        sq = xs[0] * xs[0]
        for x in xs[1:]:
            sq = sq + x * x
        scale = lax.rsqrt(jnp.sum(sq, axis=-1, keepdims=True) * (1.0 / d_model) + EPS)
        for s, x in enumerate(xs):
            cols = slice(s * slab_w, (s + 1) * slab_w)
            xn_sc[a:b, cols] = (x * scale * g_ref[:, cols]).astype(BF16)

    def store(c, y, lo, r, a, b, col0):
        if c < 2:
            val = y[a - lo:b - lo]
            if c == 0:
                val = val * (QK_SCALE * LOG2E)
            o_ref[a - r * rows:b - r * rows, col0:col0 + d_model] = val.astype(o_ref.dtype)
            return
        for blk in range((b - a) // BLOCK):
            y0 = a - lo + blk * BLOCK
            s0 = a - r * rows + blk * BLOCK
            for hp in range(d_model // LANES):
                tile = y[y0:y0 + BLOCK, hp * LANES:(hp + 1) * LANES]
                o_ref[s0:s0 + BLOCK, col0 + hp * LANES:col0 + (hp + 1) * LANES] = (
                    tile.T.astype(o_ref.dtype))

    def run(comps, with_norm):
        for ch in range(tm // chunk):
            lo, hi = ch * chunk, (ch + 1) * chunk
            pieces = [(r, max(lo, r * rows), min(hi, (r + 1) * rows)) for r in range(dilation)
                      if max(lo, r * rows) < min(hi, (r + 1) * rows)]
            if with_norm:
                for r, a, b in pieces:
                    normalise(r, a, b)
            xn = xn_sc[lo:hi, :]
            for c in comps:
                w = w_ref[:, c * d_model:(c + 1) * d_model] if whole else w_ref[...]
                y = jnp.dot(xn, w, preferred_element_type=F32)
                for r, a, b in pieces:
                    store(c, y, lo, r, a, b, ((c * dilation if whole else 0) + r) * d_model)

    if whole:
        run((0, 1, 2), True)
    else:
        j = pl.program_id(1)
        for c in range(3):
            pl.when(j == c)(functools.partial(run, (c,), c == 0))


def _qkv_dilated(x, g, w, dilation, *, tm):
    n, d = x.shape
    assert w.shape == (d, 3 * d)
    rows = tm // dilation
    assert tm % dilation == 0 and rows % BLOCK == 0
    if dilation == 1:
        n_slabs = None
        x_in = [x]
        x_specs = [pl.BlockSpec((tm, d), lambda i, j: (i, 0))]
    else:
        n_slabs = d // LANES
        x_in = [x] * n_slabs
        x_specs = [pl.BlockSpec((tm, LANES), functools.partial(lambda i, j, s: (i, s), s=s))
                   for s in range(n_slabs)]
    out_shape = jax.ShapeDtypeStruct((n // dilation, 3 * dilation * d), BF16)
    whole_bytes = 2 * (tm * 3 * d * 2) + 2 * (d * 3 * d * 2) + 2 * (tm * d * 4) + 4 * QKV_CHUNK * d * 4
    assert tm % QKV_CHUNK == 0 and (QKV_CHUNK % rows == 0 or rows % QKV_CHUNK == 0)
    if whole_bytes <= VMEM_LIMIT:
        return pl.pallas_call(
            functools.partial(_qkv_dilated_kernel, dilation=dilation, n_slabs=n_slabs,
                              chunk=QKV_CHUNK, whole=True),
            grid=(n // tm, 1),
            in_specs=x_specs + [pl.BlockSpec((1, d), lambda i, j: (0, 0)),
                                pl.BlockSpec((d, 3 * d), lambda i, j: (0, 0))],
            out_specs=pl.BlockSpec((rows, 3 * dilation * d), lambda i, j: (i, 0)),
            out_shape=out_shape,
            scratch_shapes=[pltpu.VMEM((tm, d), BF16)],
            compiler_params=_cparams(("parallel", "arbitrary")),
            name=f"qkv_d{dilation}",
        )(*x_in, g.reshape(1, d), w)
    return pl.pallas_call(
        functools.partial(_qkv_dilated_kernel, dilation=dilation, n_slabs=n_slabs,
                          chunk=QKV_CHUNK, whole=False),
        grid=(n // tm, 3),
        in_specs=x_specs + [pl.BlockSpec((1, d), lambda i, j: (0, 0)),
                            pl.BlockSpec((d, d), lambda i, j: (0, j))],
        out_specs=pl.BlockSpec((rows, dilation * d), lambda i, j: (i, j)),
        out_shape=out_shape,
        scratch_shapes=[pltpu.VMEM((tm, d), BF16)],
        compiler_params=_cparams(("parallel", "arbitrary")),
        name=f"qkv_d{dilation}",
    )(*x_in, g.reshape(1, d), w)


def _qkv_diff_kernel(x_ref, g_ref, w_ref, qk_ref, vt_ref, *, bk):
    tm, d = x_ref.shape
    nt = (((1,), (1,)), ((), ()))
    for c in range(tm // bk):
        rows = slice(c * bk, (c + 1) * bk)
        xn = _rmsnorm_f32(x_ref[rows, :], g_ref[...]).astype(BF16)
        q = jnp.dot(xn, w_ref[0], preferred_element_type=F32) * (QK_SCALE * LOG2E)
        qk_ref[rows, :d] = q.astype(qk_ref.dtype)
        qk_ref[rows, d:] = jnp.dot(xn, w_ref[1], preferred_element_type=F32).astype(qk_ref.dtype)
        vt_ref[c] = lax.dot_general(w_ref[2], xn, nt,
                                    preferred_element_type=F32).astype(vt_ref.dtype)


def _qkv_diff(x, g, w_qkv, *, bk, tm=1024):
    n, d = x.shape
    w3 = jnp.stack([w_qkv[:, :d], w_qkv[:, d:2 * d], w_qkv[:, 2 * d:].T])
    return pl.pallas_call(
        functools.partial(_qkv_diff_kernel, bk=bk),
        grid=(n // tm,),
        in_specs=[pl.BlockSpec((tm, d), lambda i: (i, 0)),
                  pl.BlockSpec((1, d), lambda i: (0, 0)),
                  pl.BlockSpec((3, d, d), lambda i: (0, 0, 0))],
        out_specs=[pl.BlockSpec((tm, 2 * d), lambda i: (i, 0)),
                   pl.BlockSpec((tm // bk, d, bk), lambda i: (i, 0, 0))],
        out_shape=[jax.ShapeDtypeStruct((n, 2 * d), BF16),
                   jax.ShapeDtypeStruct((n // bk, d, bk), BF16)],
        compiler_params=_cparams(("parallel",)),
        name="qkv_diff",
    )(x, g.reshape(1, d), w3)


def _dilated_kernel(q_ref, kp_ref, kc_ref, vtp_ref, vtc_ref, bias_ref, o_ref, lse_ref,
                    s_sc, m_sc, pv_sc):
    first_step = (pl.program_id(2) == 0).astype(jnp.int32)
    lane = lax.broadcasted_iota(jnp.int32, (BLOCK, LANES), 1)
    row = lax.broadcasted_iota(jnp.int32, (BLOCK, LANES), 0)
    head_row = lax.broadcasted_iota(jnp.int32, (N_HEADS_A, BLOCK), 0)
    lo_row = row < HEAD_DIM
    mask_lo = jnp.where(lane < HEAD_DIM, 1.0, 0.0).astype(BF16)
    mask_hi = jnp.where(lane < HEAD_DIM, 0.0, 1.0).astype(BF16)
    nt = (((1,), (1,)), ((), ()))
    ones = jnp.ones((SUM_ROWS, 2 * BLOCK), BF16)
    n_pairs = N_HEADS_A // 2
    items = [(qb, hp) for qb in range(DIL_QB) for hp in range(n_pairs)]

    def rows(qb):
        return slice(qb * BLOCK, (qb + 1) * BLOCK)

    def prev_cur(prev_ref, cur_ref, qb, sl):
        prev = prev_ref[:, sl] if qb == 0 else cur_ref[rows(qb - 1), sl]
        return prev, cur_ref[rows(qb), sl]

    for it, (qb, hp) in enumerate(items):
        sl = slice(hp * LANES, (hp + 1) * LANES)
        q = q_ref[rows(qb), sl]
        q2 = jnp.concatenate([q * mask_lo, q * mask_hi], axis=0)
        k = jnp.concatenate(prev_cur(kp_ref, kc_ref, qb, sl), axis=0)
        s = lax.dot_general(k, q2, nt, preferred_element_type=F32)
        s = s + bias_ref[first_step if qb == 0 else 0, hp]
        s_sc[it] = s
        m_sc[it] = jnp.max(s, axis=0, keepdims=True)
    for it, (qb, hp) in enumerate(items):
        sl = slice(hp * LANES, (hp + 1) * LANES)
        p = jnp.exp2(s_sc[it] - m_sc[it]).astype(BF16)
        vt = jnp.concatenate(prev_cur(vtp_ref, vtc_ref, qb, sl), axis=1)
        pv_sc[it] = jnp.dot(jnp.concatenate([vt, ones], axis=0), p,
                            preferred_element_type=F32)
    for qb in range(DIL_QB):
        lse_t = jnp.zeros((N_HEADS_A, BLOCK), F32)
        for hp in range(n_pairs):
            it = qb * n_pairs + hp
            l = pv_sc[it, LANES:LANES + 1]
            acc = pv_sc[it, :LANES] / l
            o_t = jnp.where(lo_row, acc[:, :BLOCK], acc[:, BLOCK:])
            o_ref[rows(qb), hp * LANES:(hp + 1) * LANES] = o_t.T.astype(o_ref.dtype)
            lse = (m_sc[it] + jnp.log2(l)) * LN2
            lse_t = jnp.where(head_row == 2 * hp, lse[:, :BLOCK], lse_t)
            lse_t = jnp.where(head_row == 2 * hp + 1, lse[:, BLOCK:], lse_t)
        lse_full = jnp.concatenate([lse_t, jnp.zeros((BLOCK - N_HEADS_A, BLOCK), F32)], axis=0)
        lse_ref[rows(qb), :] = lse_full.T


def _dilated_attention(qkv, bias, group, dilation, batch, seq):
    assert seq % (dilation * BLOCK * DIL_QB) == 0
    sub_len = seq // dilation
    steps = sub_len // (BLOCK * DIL_QB)
    qkv_v = qkv.reshape(batch, sub_len, 3 * dilation * D_MODEL)

    def col(c):
        return lambda b, r, n: (b, n, c * dilation + r)

    def col_prev(c):
        return lambda b, r, n: (b, jnp.maximum(n * DIL_QB - 1, 0), c * dilation + r)

    blk = (None, DIL_QB * BLOCK, D_MODEL)
    blk_prev = (None, BLOCK, D_MODEL)
    n_items = DIL_QB * N_HEADS_A // 2
    o, lse = pl.pallas_call(
        _dilated_kernel,
        grid=(batch, dilation, steps),
        in_specs=[pl.BlockSpec(blk, col(0)),
                  pl.BlockSpec(blk_prev, col_prev(1)),
                  pl.BlockSpec(blk, col(1)),
                  pl.BlockSpec(blk_prev, col_prev(2)),
                  pl.BlockSpec(blk, col(2)),
                  pl.BlockSpec((None, 2, N_HEADS_A // 2, 2 * BLOCK, 2 * BLOCK),
                               lambda b, r, n: (group, 0, 0, 0, 0))],
        out_specs=[pl.BlockSpec(blk, lambda b, r, n: (b, n, r)),
                   pl.BlockSpec((None, DIL_QB * BLOCK, LANES), lambda b, r, n: (b, n, r))],
        out_shape=[jax.ShapeDtypeStruct((batch, sub_len, dilation * D_MODEL), BF16),
                   jax.ShapeDtypeStruct((batch, sub_len, dilation * LANES), F32)],
        scratch_shapes=[pltpu.VMEM((n_items, 2 * BLOCK, 2 * BLOCK), F32),
                        pltpu.VMEM((n_items, 1, 2 * BLOCK), F32),
                        pltpu.VMEM((n_items, LANES + SUM_ROWS, 2 * BLOCK), F32)],
        compiler_params=_cparams(("parallel", "parallel", "arbitrary")),
        name=f"dilated_attn_d{dilation}",
    )(qkv_v, qkv_v, qkv_v, qkv_v, qkv_v, bias)
    return (o.reshape(batch * sub_len, dilation * D_MODEL),
            lse.reshape(batch * sub_len, dilation * LANES))


def _combine_proj_kernel(h_ref, o0_ref, o1_ref, o2_ref, l0_ref, l1_ref, l2_ref,
                         e_ref, w_ref, out_ref, lse_sc, o_sc):
    o_refs = [o0_ref, o1_ref, o2_ref]
    l_refs = [l0_ref, l1_ref, l2_ref]
    tm, d = h_ref.shape
    n_slabs = d // LANES
    for g, (_, dilation) in enumerate(DIL_CONFIGS):
        rows = tm // dilation
        for r in range(dilation):
            dst = pl.ds(r, rows, stride=dilation) if dilation > 1 else slice(None)
            lse_sc[g, dst, :] = l_refs[g][:, r * LANES:(r + 1) * LANES]
            for c in range(n_slabs):
                o_sc[g, c, dst, :] = o_refs[g][:, r * d + c * LANES:r * d + (c + 1) * LANES].astype(F32)
    lses = [lse_sc[g] for g in range(N_GROUPS)]
    mx = jnp.maximum(jnp.maximum(lses[0], lses[1]), lses[2])
    ws = [jnp.exp(l - mx) for l in lses]
    tot = ws[0] + ws[1] + ws[2]
    head_lane = lax.broadcasted_iota(jnp.int32, (tm, LANES), 1) < N_HEADS_A
    packed = None
    for g in range(N_GROUPS):
        a = jnp.where(head_lane, ws[g] / tot, 0.0)
        a_hi = a.astype(BF16).astype(F32)
        for part, piece in enumerate((a_hi, a - a_hi)):
            shift = (part * N_GROUPS + g) * N_HEADS_A
            moved = piece if shift == 0 else pltpu.roll(piece, shift, axis=1)
            packed = moved if packed is None else packed + moved
    ae_all = jnp.dot(packed.astype(BF16), e_ref[...], preferred_element_type=F32)
    aes = [ae_all[:, g * d:(g + 1) * d] for g in range(N_GROUPS)]
    slabs = []
    for c in range(n_slabs):
        cols = slice(c * LANES, (c + 1) * LANES)
        slabs.append(aes[0][:, cols] * o_sc[0, c] + aes[1][:, cols] * o_sc[1, c]
                     + aes[2][:, cols] * o_sc[2, c])
    o = jnp.concatenate(slabs, axis=1).astype(BF16)
    out_ref[...] = h_ref[...] + jnp.dot(o, w_ref[...], preferred_element_type=F32)


def _combine_proj(h, outs, lses, w, *, tm=512):
    n, d = h.shape
    rows = jnp.arange(LANES, dtype=jnp.int32)
    cols = jnp.arange(N_GROUPS * d, dtype=jnp.int32)
    row_group = (rows // N_HEADS_A) % N_GROUPS
    row_head = rows % N_HEADS_A
    row_used = rows < 2 * N_GROUPS * N_HEADS_A
    expand = (row_used[:, None] & (row_group[:, None] == (cols // d)[None, :])
              & (row_head[:, None] == ((cols % d) // HEAD_DIM)[None, :])).astype(BF16)
    row = lambda i: (i, 0)
    full = lambda i: (0, 0)
    dils = [dilation for _, dilation in DIL_CONFIGS]
    assert all(tm % (16 * dilation) == 0 for dilation in dils)
    return pl.pallas_call(
        _combine_proj_kernel,
        grid=(n // tm,),
        in_specs=[pl.BlockSpec((tm, d), row)]
                 + [pl.BlockSpec((tm // dilation, dilation * d), row) for dilation in dils]
                 + [pl.BlockSpec((tm // dilation, dilation * LANES), row) for dilation in dils]
                 + [pl.BlockSpec((LANES, N_GROUPS * d), full), pl.BlockSpec((d, d), full)],
        out_specs=pl.BlockSpec((tm, d), row),
        out_shape=jax.ShapeDtypeStruct((n, d), F32),
        scratch_shapes=[pltpu.VMEM((N_GROUPS, tm, LANES), F32),
                        pltpu.VMEM((N_GROUPS, d // LANES, tm, LANES), F32)],
        compiler_params=_cparams(("parallel",)),
        name="combine_proj",
    )(h, *outs, *lses, expand, w)


def _mlp_ple_kernel(*refs, final_norm, with_proj, n_steps, chunk):
    if with_proj:
        h_ref, o_ref, wo_ref = refs[:3]
        refs = refs[3:]
    else:
        h_ref = refs[0]
        refs = refs[1:]
    p_ref, g_ref, w1_ref, w2_ref, gp_ref, wg_ref, wp_ref, fg_ref, out_ref, xn_ref, acc_ref = refs
    f = pl.program_id(1)
    tm = h_ref.shape[0]

    def step(first, last):
        for ch in range(tm // chunk):
            rows = slice(ch * chunk, (ch + 1) * chunk)
            if first:
                if with_proj:
                    h0 = h_ref[rows, :] + jnp.dot(o_ref[rows, :], wo_ref[...],
                                                  preferred_element_type=F32)
                    out_ref[rows, :] = h0
                else:
                    h0 = h_ref[rows, :]
                xn_ref[rows, :] = _rmsnorm_f32(h0, g_ref[...]).astype(BF16)
            a = jnp.dot(xn_ref[rows, :], w1_ref[...], preferred_element_type=F32)
            a = jnp.maximum(a, 0.0)
            a = (a * a).astype(BF16)
            acc = jnp.dot(a, w2_ref[...], preferred_element_type=F32)
            if not first:
                acc = acc_ref[rows, :] + acc
            if not last:
                acc_ref[rows, :] = acc
                continue
            x = (out_ref[rows, :] if with_proj else h_ref[rows, :]) + acc
            xn = _rmsnorm_f32(x, gp_ref[...]).astype(BF16)
            gate = jax.nn.sigmoid(jnp.dot(xn, wg_ref[...], preferred_element_type=F32))
            proj = jnp.dot(p_ref[rows, :].astype(BF16), wp_ref[...], preferred_element_type=F32)
            y = x + gate * proj
            if final_norm:
                y = _rmsnorm_f32(y, fg_ref[...])
            out_ref[rows, :] = y

    pl.when(f == 0)(functools.partial(step, True, n_steps == 1))
    if n_steps > 2:
        pl.when((f > 0) & (f < n_steps - 1))(functools.partial(step, False, False))
    if n_steps > 1:
        pl.when(f == n_steps - 1)(functools.partial(step, False, True))


def _mlp_ple(h, p_all, layer, g, w1, w2, gp, wg, wp, fg, *, final_norm, proj=None, tm=1024, tf=1024):
    n, d = h.shape
    dff = w1.shape[1]
    pd = p_all.shape[-1]
    vec = lambda i, f: (0, 0)
    row = lambda i, f: (i, 0)
    lead_specs = [pl.BlockSpec((tm, d), row)]
    lead_args = [h]
    if proj is not None:
        o, wo = proj
        lead_specs += [pl.BlockSpec((tm, o.shape[1]), row), pl.BlockSpec(wo.shape, vec)]
        lead_args += [o, wo]
    return pl.pallas_call(
        functools.partial(_mlp_ple_kernel, final_norm=final_norm, with_proj=proj is not None,
                          n_steps=dff // tf, chunk=MLP_CHUNK),
        grid=(n // tm, dff // tf),
        in_specs=lead_specs + [
                  pl.BlockSpec((None, tm, pd), lambda i, f: (layer, i, 0)),
                  pl.BlockSpec((1, d), vec),
                  pl.BlockSpec((d, tf), lambda i, f: (0, f)),
                  pl.BlockSpec((tf, d), lambda i, f: (f, 0)),
                  pl.BlockSpec((1, d), vec),
                  pl.BlockSpec((d, d), vec),
                  pl.BlockSpec((pd, d), vec),
                  pl.BlockSpec((1, d), vec)],
        out_specs=pl.BlockSpec((tm, d), lambda i, f: (i, 0)),
        out_shape=jax.ShapeDtypeStruct((n, d), F32),
        scratch_shapes=[pltpu.VMEM((tm, d), BF16), pltpu.VMEM((tm, d), F32)],
        compiler_params=_cparams(("parallel", "arbitrary"),
                                 vmem_limit=VMEM_LIMIT_PROJ if proj is not None else VMEM_LIMIT),
        name="mlp_ple",
    )(*lead_args, p_all, g.reshape(1, d), w1, w2, gp.reshape(1, d), wg, wp, fg.reshape(1, d))


def _diff_attn_kernel(lq1_ref, lk1_ref, lq2_ref, lk2_ref, q_ref, qn_ref, k_ref, vt_ref, u_ref, sg_ref,
                      o_ref, m_sc, acc_sc, sa_sc, samax_sc, sb_sc, sbmax_sc, *, bq, bk, lambda_init):
    n_parts = bq // bk
    assert bq == n_parts * bk and n_parts % 2 == 0
    i_q = pl.program_id(2)
    q0 = i_q * bq
    nt = (((1,), (1,)), ((), ()))
    lane = lax.broadcasted_iota(jnp.int32, (bq, LANES), 1)

    def query_columns(ref):
        qf = ref[...].astype(F32)
        a = jnp.where(lane < HEAD_DIM, qf, 0.0)
        b = jnp.where(lane >= HEAD_DIM, qf, 0.0)
        pieces = []
        for part in range(n_parts):
            pieces += [a[part * bk:(part + 1) * bk], b[part * bk:(part + 1) * bk]]
        return jnp.concatenate(pieces, axis=0).astype(BF16)

    q2 = query_columns(q_ref)
    q2_next = query_columns(qn_ref)

    m_sc[...] = jnp.full(m_sc.shape, NEG, F32)
    acc_sc[...] = jnp.zeros(acc_sc.shape, F32)

    per_map = bk // LANES
    part_blocks = 2 * per_map
    n_blocks = n_parts * part_blocks
    first_diag = q0 // bk
    ones = jnp.ones((SUM_ROWS, bk), BF16)

    def scores(c, q2x, q0x, s_ref, smax_ref, first_block=0):
        k0 = pl.multiple_of(c * bk, bk)
        lanes = slice(first_block * LANES, n_blocks * LANES)
        s = lax.dot_general(k_ref[pl.ds(k0, bk), :], q2x[lanes], nt,
                            preferred_element_type=F32)
        rows = []
        for jb in range(bk // LANES):
            tiles = []
            for ib in range(first_block, n_blocks):
                part, within = divmod(ib, part_blocks)
                col, blk = divmod(within, per_map)
                i0 = part * bk + blk * LANES
                t = lax.shift_right_arithmetic(q0x + i0 - k0 - jb * LANES, LOG2_LANES)
                t = jnp.where(t < 0, DIFF_MASK_TILE, jnp.minimum(t, DIFF_CONST_TILE))
                tiles.append(u_ref[t, col])
            rows.append(jnp.concatenate(tiles, axis=1))
        s = s + jnp.concatenate(rows, axis=0)
        s_ref[:, lanes] = s
        smax_ref[:, lanes] = jnp.max(s, axis=0, keepdims=True)

    def accumulate(c, s_ref, smax_ref, first_block=0):
        lanes = slice(first_block * LANES, n_blocks * LANES)
        m_prev = m_sc[:, lanes]
        m_new = jnp.maximum(m_prev, smax_ref[:, lanes])
        m_sc[:, lanes] = m_new
        p = jnp.exp2(s_ref[:, lanes] - m_new).astype(BF16)
        v1 = jnp.concatenate([vt_ref[c], ones], axis=0)
        acc_sc[:, lanes] = (jnp.exp2(m_prev - m_new) * acc_sc[:, lanes]
                            + jnp.dot(v1, p, preferred_element_type=F32))

    @pl.when(i_q == 0)
    def _():
        scores(0, q2, q0, sa_sc, samax_sc)

    def body(i, carry):
        scores(2 * i + 1, q2, q0, sb_sc, sbmax_sc)
        accumulate(2 * i, sa_sc, samax_sc)
        scores(2 * i + 2, q2, q0, sa_sc, samax_sc)
        accumulate(2 * i + 1, sb_sc, sbmax_sc)
        return carry

    lax.fori_loop(0, first_diag // 2, body, 0)

    for dt in range(n_parts // 2):
        ja, jb = 2 * dt, 2 * dt + 1
        scores(first_diag + jb, q2, q0, sb_sc, sbmax_sc, first_block=jb * part_blocks)
        accumulate(first_diag + ja, sa_sc, samax_sc, first_block=ja * part_blocks)
        if jb + 1 < n_parts:
            scores(first_diag + jb + 1, q2, q0, sa_sc, samax_sc, first_block=(jb + 1) * part_blocks)
        else:
            scores(0, q2_next, q0 + bq, sa_sc, samax_sc)
        accumulate(first_diag + jb, sb_sc, sbmax_sc, first_block=jb * part_blocks)

    lam = (jnp.exp(jnp.sum(lq1_ref[...] * lk1_ref[...], keepdims=True))
           - jnp.exp(jnp.sum(lq2_ref[...] * lk2_ref[...], keepdims=True)) + lambda_init)
    accl = acc_sc[...]
    acc = accl[:LANES] / accl[LANES:LANES + 1]
    map0 = jnp.concatenate([acc[:, 2 * part * bk:(2 * part + 1) * bk] for part in range(n_parts)], axis=1)
    map1 = jnp.concatenate([acc[:, (2 * part + 1) * bk:(2 * part + 2) * bk] for part in range(n_parts)], axis=1)
    o = (map0 - lam * map1).T
    y = _rmsnorm_f32(o, sg_ref[...]) * (1.0 - lambda_init)
    o_ref[...] = y.astype(o_ref.dtype)


def _diff_attention(qk, vt, u, lq1, lk1, lq2, lk2, subln, lambda_init, batch, seq, *, bq, bk):
    nh = N_HEADS_B
    nkt = seq // bk
    nq = seq // bq
    vec = lambda a: a.reshape(1, -1)
    small = pl.BlockSpec((1, HEAD_DIM), lambda b, h, i: (0, 0))
    return pl.pallas_call(
        functools.partial(_diff_attn_kernel, bq=bq, bk=bk, lambda_init=lambda_init),
        grid=(batch, nh, seq // bq),
        in_specs=[small, small, small, small,
                  pl.BlockSpec((None, bq, LANES), lambda b, h, i: (b, i, h)),
                  pl.BlockSpec((None, bq, LANES), lambda b, h, i: (b, jnp.minimum(i + 1, nq - 1), h)),
                  pl.BlockSpec((None, seq, LANES), lambda b, h, i: (b, 0, nh + h)),
                  pl.BlockSpec((nkt, LANES, bk), lambda b, h, i: (b, h, 0)),
                  pl.BlockSpec((DIFF_N_TILES, 2, LANES, LANES), lambda b, h, i: (0, h, 0, 0)),
                  pl.BlockSpec((1, LANES), lambda b, h, i: (0, 0))],
        out_specs=pl.BlockSpec((None, bq, LANES), lambda b, h, i: (b, i, h)),
        out_shape=jax.ShapeDtypeStruct((batch, seq, D_MODEL), BF16),
        scratch_shapes=[pltpu.VMEM((1, 2 * bq), F32),
                        pltpu.VMEM((LANES + SUM_ROWS, 2 * bq), F32),
                        pltpu.VMEM((bk, 2 * bq), F32),
                        pltpu.VMEM((1, 2 * bq), F32),
                        pltpu.VMEM((bk, 2 * bq), F32),
                        pltpu.VMEM((1, 2 * bq), F32)],
        compiler_params=_cparams(("parallel", "parallel", "arbitrary")),
        name="diff_attn",
    )(vec(lq1), vec(lk1), vec(lq2), vec(lk2), qk, qk, qk, vt, u, vec(subln))


def kernel(x, p, rel_bias, a_w_qkv, a_w_o, b_w_qkv, b_w_o, b_lambda_q1, b_lambda_k1, b_lambda_q2, b_lambda_k2, b_subln, norm_mix, norm_mlp, w_ff1, w_ff2, norm_ple, w_ple_gate, w_ple_proj, final_norm):
    batch, seq, d = x.shape
    depth = p.shape[0]
    n = batch * seq
    h = x.reshape(n, d)
    p_all = p.reshape(depth, n, p.shape[-1])
    n_mixers = 2

    for i in range(depth):
        j = i // n_mixers
        if i % n_mixers == 0:
            w_a = a_w_qkv[j].astype(BF16)
            bias = _build_dilated_bias(rel_bias)
            outs, lses = [], []
            for g, (_, dilation) in enumerate(DIL_CONFIGS):
                w_g = w_a[:, g * 3 * d:(g + 1) * 3 * d]
                qkv = _qkv_dilated(h, norm_mix[i], w_g, dilation, tm=max(1024, BLOCK * dilation))
                o_g, lse_g = _dilated_attention(qkv, bias, g, dilation, batch, seq)
                outs.append(o_g)
                lses.append(lse_g)
            h = _combine_proj(h, outs, lses, a_w_o[j].astype(BF16))
            proj = None
        else:
            lambda_init = 0.8 - 0.6 * math.exp(-0.3 * i)
            bq, bk = 2048, 512
            w_b = b_w_qkv[j].astype(BF16)
            qk, vt = _qkv_diff(h, norm_mix[i], w_b, bk=bk)
            u = _build_diff_bias(rel_bias)
            o = _diff_attention(qk.reshape(batch, seq, 2 * d), vt, u,
                                b_lambda_q1[j], b_lambda_k1[j], b_lambda_q2[j], b_lambda_k2[j],
                                b_subln[j], lambda_init, batch, seq, bq=bq, bk=bk)
            proj = (o.reshape(n, d), b_w_o[j].astype(BF16))
        h = _mlp_ple(h, p_all, i, norm_mlp[i], w_ff1[i].astype(BF16), w_ff2[i].astype(BF16),
                     norm_ple[i], w_ple_gate[i].astype(BF16), w_ple_proj[i].astype(BF16),
                     final_norm, final_norm=(i == depth - 1), proj=proj)
    return h.reshape(batch, seq, d)
```

```python
import functools
import math

import jax
import jax.numpy as jnp
from jax import lax
from jax.experimental import pallas as pl
from jax.experimental.pallas import tpu as pltpu

F32 = jnp.float32
BF16 = jnp.bfloat16

D_MODEL = 1024
HEAD_DIM = 64
BLOCK = 128
QKV_CHUNK = 512
MLP_CHUNK = 512
DIL_RING = 6
DIL_QB = 4
DIL_CONFIGS = ((128, 1), (512, 4), (2048, 16))
N_GROUPS = len(DIL_CONFIGS)
N_HEADS_A = D_MODEL // HEAD_DIM
N_HEADS_B = D_MODEL // (2 * HEAD_DIM)
N_BUCKETS = 32
MAX_DISTANCE = 2048
N_BIAS_COLS = 16
EPS = 1e-6
NEG = -1e30
LANES = 128
SUBLANES = 8
LOG2_LANES = 7
QK_SCALE = HEAD_DIM ** -0.5
LOG2E = math.log2(math.e)
LN2 = math.log(2.0)

DIFF_CONST_TILE = (MAX_DISTANCE + LANES - 1) // LANES + 1
DIFF_MASK_TILE = DIFF_CONST_TILE + 1
DIFF_N_TILES = DIFF_MASK_TILE + 1
SUM_ROWS = 16

VMEM_LIMIT = 48 * 1024 * 1024
VMEM_LIMIT_PROJ = 56 * 1024 * 1024


def _cparams(sem, vmem_limit=VMEM_LIMIT):
    return pltpu.CompilerParams(dimension_semantics=sem, vmem_limit_bytes=vmem_limit)


def _rmsnorm_f32(x, g):
    ms = jnp.mean(x * x, axis=-1, keepdims=True)
    return x * lax.rsqrt(ms + EPS) * g


def _rel_bucket(dist):
    n = jnp.maximum(dist, 0)
    max_exact = N_BUCKETS // 2
    nf = jnp.maximum(n, 1).astype(F32)
    large = max_exact + (jnp.log(nf / max_exact) / math.log(MAX_DISTANCE / max_exact)
                         * (N_BUCKETS - max_exact)).astype(jnp.int32)
    large = jnp.minimum(large, N_BUCKETS - 1)
    return jnp.where(n < max_exact, n, large)


def _table_lookup(bucket, tab_ref, col):
    acc = jnp.zeros(bucket.shape, F32)
    for b in range(N_BUCKETS):
        acc = jnp.where(bucket == b, tab_ref[b, col], acc)
    return acc


def _dilated_bias_kernel(tab_ref, o_ref):
    g = pl.program_id(0)
    dilation = jnp.where(g == 0, DIL_CONFIGS[0][1],
                         jnp.where(g == 1, DIL_CONFIGS[1][1], DIL_CONFIGS[2][1]))
    kj = lax.broadcasted_iota(jnp.int32, (2 * BLOCK, BLOCK), 0)
    qi = lax.broadcasted_iota(jnp.int32, (2 * BLOCK, BLOCK), 1)
    sub = qi + BLOCK - kj
    band = (sub >= 0) & (sub <= BLOCK)
    band_first = band & (kj >= BLOCK)
    bucket = _rel_bucket(sub * dilation)
    for c in range(N_BIAS_COLS):
        lanes = slice((c % 2) * BLOCK, (c % 2 + 1) * BLOCK)
        bias = _table_lookup(bucket, tab_ref, c) * LOG2E
        o_ref[0, c // 2, :, lanes] = jnp.where(band, bias, NEG)
        o_ref[1, c // 2, :, lanes] = jnp.where(band_first, bias, NEG)


def _build_dilated_bias(rel_bias):
    return pl.pallas_call(
        _dilated_bias_kernel,
        grid=(N_GROUPS,),
        in_specs=[pl.BlockSpec(memory_space=pltpu.SMEM)],
        out_specs=pl.BlockSpec((None, 2, N_HEADS_A // 2, 2 * BLOCK, 2 * BLOCK),
                               lambda g: (g, 0, 0, 0, 0)),
        out_shape=jax.ShapeDtypeStruct((N_GROUPS, 2, N_HEADS_A // 2, 2 * BLOCK, 2 * BLOCK), F32),
        compiler_params=_cparams(("arbitrary",)),
        name="dilated_bias",
    )(rel_bias)


def _diff_bias_kernel(tab_ref, o_ref):
    t = pl.program_id(0)
    kj = lax.broadcasted_iota(jnp.int32, (LANES, LANES), 0)
    qi = lax.broadcasted_iota(jnp.int32, (LANES, LANES), 1)
    dist = t * LANES + qi - kj
    masked = (dist < 0) | (t == DIFF_MASK_TILE)
    bucket = _rel_bucket(dist)
    init = jnp.where(masked, NEG, 0.0)
    for c in range(N_BIAS_COLS):
        o_ref[c] = init
    live = jnp.where(masked, -1, bucket)

    def body(b, carry):
        hit = live == b
        for c in range(N_BIAS_COLS):
            o_ref[c] = jnp.where(hit, tab_ref[b, c] * LOG2E, o_ref[c])
        return carry

    lax.fori_loop(jnp.min(bucket), jnp.max(bucket) + 1, body, 0)


def _build_diff_bias(rel_bias):
    return pl.pallas_call(
        _diff_bias_kernel,
        grid=(DIFF_N_TILES,),
        in_specs=[pl.BlockSpec(memory_space=pltpu.SMEM)],
        out_specs=pl.BlockSpec((None, N_BIAS_COLS, LANES, LANES), lambda t: (t, 0, 0, 0)),
        out_shape=jax.ShapeDtypeStruct((DIFF_N_TILES, N_BIAS_COLS, LANES, LANES), F32),
        compiler_params=_cparams(("arbitrary",)),
        name="diff_bias",
    )(rel_bias)


def _qkv_dilated_kernel(*refs, dilation, n_slabs, chunk, whole):
    n_x = len(refs) - 4
    x_refs = refs[:n_x]
    g_ref, w_ref, o_ref, xn_sc = refs[n_x:]
    tm, d_model = xn_sc.shape
    rows = tm // dilation

    def normalise(r, a, b):
        if n_slabs is None:
            xs = [x_refs[0][a:b, :]]
        else:
            xs = [x_ref[pl.ds(r + (a - r * rows) * dilation, b - a, stride=dilation), :]
                  for x_ref in x_refs]
        slab_w = d_model // len(xs)
        sq = xs[0] * xs[0]
        for x in xs[1:]:
            sq = sq + x * x
        scale = lax.rsqrt(jnp.sum(sq, axis=-1, keepdims=True) * (1.0 / d_model) + EPS)
        for s, x in enumerate(xs):
            cols = slice(s * slab_w, (s + 1) * slab_w)
            xn_sc[a:b, cols] = (x * scale * g_ref[:, cols]).astype(BF16)

    def store(c, y, lo, r, a, b, col0):
        if c < 2:
            val = y[a - lo:b - lo]
            if c == 0:
                val = val * (QK_SCALE * LOG2E)
            o_ref[a - r * rows:b - r * rows, col0:col0 + d_model] = val.astype(o_ref.dtype)
            return
        for blk in range((b - a) // BLOCK):
            y0 = a - lo + blk * BLOCK
            s0 = a - r * rows + blk * BLOCK
            for hp in range(d_model // LANES):
                tile = y[y0:y0 + BLOCK, hp * LANES:(hp + 1) * LANES]
                o_ref[s0:s0 + BLOCK, col0 + hp * LANES:col0 + (hp + 1) * LANES] = (
                    tile.T.astype(o_ref.dtype))

    def run(comps, with_norm):
        for ch in range(tm // chunk):
            lo, hi = ch * chunk, (ch + 1) * chunk
            pieces = [(r, max(lo, r * rows), min(hi, (r + 1) * rows)) for r in range(dilation)
                      if max(lo, r * rows) < min(hi, (r + 1) * rows)]
            if with_norm:
                for r, a, b in pieces:
                    normalise(r, a, b)
            xn = xn_sc[lo:hi, :]
            for c in comps:
                w = w_ref[:, c * d_model:(c + 1) * d_model] if whole else w_ref[...]
                y = jnp.dot(xn, w, preferred_element_type=F32)
                for r, a, b in pieces:
                    store(c, y, lo, r, a, b, ((c * dilation if whole else 0) + r) * d_model)

    if whole:
        run((0, 1, 2), True)
    else:
        j = pl.program_id(1)
        for c in range(3):
            pl.when(j == c)(functools.partial(run, (c,), c == 0))


def _qkv_dilated(x, g, w, group, dilation, *, tm):
    n, d = x.shape
    assert w.shape[0] == d and w.shape[1] % (3 * d) == 0
    rows = tm // dilation
    assert tm % dilation == 0 and rows % BLOCK == 0
    if dilation == 1:
        n_slabs = None
        x_in = [x]
        x_specs = [pl.BlockSpec((tm, d), lambda i, j: (i, 0))]
    else:
        n_slabs = d // LANES
        x_in = [x] * n_slabs
        x_specs = [pl.BlockSpec((tm, LANES), functools.partial(lambda i, j, s: (i, s), s=s))
                   for s in range(n_slabs)]
    out_shape = jax.ShapeDtypeStruct((n // dilation, 3 * dilation * d), BF16)
    whole_bytes = 2 * (tm * 3 * d * 2) + 2 * (d * 3 * d * 2) + 2 * (tm * d * 4) + 4 * QKV_CHUNK * d * 4
    assert tm % QKV_CHUNK == 0 and (QKV_CHUNK % rows == 0 or rows % QKV_CHUNK == 0)
    if whole_bytes <= VMEM_LIMIT:
        return pl.pallas_call(
            functools.partial(_qkv_dilated_kernel, dilation=dilation, n_slabs=n_slabs,
                              chunk=QKV_CHUNK, whole=True),
            grid=(n // tm, 1),
            in_specs=x_specs + [pl.BlockSpec((1, d), lambda i, j: (0, 0)),
                                pl.BlockSpec((d, 3 * d), lambda i, j: (0, group))],
            out_specs=pl.BlockSpec((rows, 3 * dilation * d), lambda i, j: (i, 0)),
            out_shape=out_shape,
            scratch_shapes=[pltpu.VMEM((tm, d), BF16)],
            compiler_params=_cparams(("parallel", "arbitrary")),
            name=f"qkv_d{dilation}",
        )(*x_in, g.reshape(1, d), w)
    return pl.pallas_call(
        functools.partial(_qkv_dilated_kernel, dilation=dilation, n_slabs=n_slabs,
                          chunk=QKV_CHUNK, whole=False),
        grid=(n // tm, 3),
        in_specs=x_specs + [pl.BlockSpec((1, d), lambda i, j: (0, 0)),
                            pl.BlockSpec((d, d), lambda i, j: (0, 3 * group + j))],
        out_specs=pl.BlockSpec((rows, dilation * d), lambda i, j: (i, j)),
        out_shape=out_shape,
        scratch_shapes=[pltpu.VMEM((tm, d), BF16)],
        compiler_params=_cparams(("parallel", "arbitrary")),
        name=f"qkv_d{dilation}",
    )(*x_in, g.reshape(1, d), w)


def _qkv_diff_kernel(x_ref, g_ref, w_ref, qk_ref, vt_ref, *, bk):
    tm, d = x_ref.shape
    nt = (((1,), (1,)), ((), ()))
    for c in range(tm // bk):
        rows = slice(c * bk, (c + 1) * bk)
        xn = _rmsnorm_f32(x_ref[rows, :], g_ref[...]).astype(BF16)
        q = jnp.dot(xn, w_ref[0], preferred_element_type=F32) * (QK_SCALE * LOG2E)
        qk_ref[rows, :d] = q.astype(qk_ref.dtype)
        qk_ref[rows, d:] = jnp.dot(xn, w_ref[1], preferred_element_type=F32).astype(qk_ref.dtype)
        vt_ref[c] = lax.dot_general(w_ref[2], xn, nt,
                                    preferred_element_type=F32).astype(vt_ref.dtype)


def _qkv_diff(x, g, w_qkv, *, bk, tm=1024):
    n, d = x.shape
    w3 = jnp.stack([w_qkv[:, :d], w_qkv[:, d:2 * d], w_qkv[:, 2 * d:].T])
    return pl.pallas_call(
        functools.partial(_qkv_diff_kernel, bk=bk),
        grid=(n // tm,),
        in_specs=[pl.BlockSpec((tm, d), lambda i: (i, 0)),
                  pl.BlockSpec((1, d), lambda i: (0, 0)),
                  pl.BlockSpec((3, d, d), lambda i: (0, 0, 0))],
        out_specs=[pl.BlockSpec((tm, 2 * d), lambda i: (i, 0)),
                   pl.BlockSpec((tm // bk, d, bk), lambda i: (i, 0, 0))],
        out_shape=[jax.ShapeDtypeStruct((n, 2 * d), BF16),
                   jax.ShapeDtypeStruct((n // bk, d, bk), BF16)],
        compiler_params=_cparams(("parallel",)),
        name="qkv_diff",
    )(x, g.reshape(1, d), w3)


def _dilated_kernel(q_ref, kp_ref, kc_ref, vtp_ref, vtc_ref, bias_ref, o_ref, lse_ref,
                    s_sc, m_sc, pv_sc):
    first_step = (pl.program_id(2) == 0).astype(jnp.int32)
    lane = lax.broadcasted_iota(jnp.int32, (BLOCK, LANES), 1)
    row = lax.broadcasted_iota(jnp.int32, (BLOCK, LANES), 0)
    head_row = lax.broadcasted_iota(jnp.int32, (N_HEADS_A, BLOCK), 0)
    lo_row = row < HEAD_DIM
    mask_lo = jnp.where(lane < HEAD_DIM, 1.0, 0.0).astype(BF16)
    mask_hi = jnp.where(lane < HEAD_DIM, 0.0, 1.0).astype(BF16)
    nt = (((1,), (1,)), ((), ()))
    ones = jnp.ones((SUM_ROWS, 2 * BLOCK), BF16)
    n_pairs = N_HEADS_A // 2
    items = [(qb, hp) for qb in range(DIL_QB) for hp in range(n_pairs)]

    def rows(qb):
        return slice(qb * BLOCK, (qb + 1) * BLOCK)

    def prev_cur(prev_ref, cur_ref, qb, sl):
        prev = prev_ref[:, sl] if qb == 0 else cur_ref[rows(qb - 1), sl]
        return prev, cur_ref[rows(qb), sl]

    def scores(it):
        qb, hp = items[it]
        sl = slice(hp * LANES, (hp + 1) * LANES)
        q = q_ref[rows(qb), sl]
        q2 = jnp.concatenate([q * mask_lo, q * mask_hi], axis=0)
        k = jnp.concatenate(prev_cur(kp_ref, kc_ref, qb, sl), axis=0)
        s = lax.dot_general(k, q2, nt, preferred_element_type=F32)
        s = s + bias_ref[first_step if qb == 0 else 0, hp]
        s_sc[it % DIL_RING] = s
        m_sc[it] = jnp.max(s, axis=0, keepdims=True)

    def value_product(it):
        qb, hp = items[it]
        sl = slice(hp * LANES, (hp + 1) * LANES)
        p = jnp.exp2(s_sc[it % DIL_RING] - m_sc[it]).astype(BF16)
        vt = jnp.concatenate(prev_cur(vtp_ref, vtc_ref, qb, sl), axis=1)
        pv_sc[it] = jnp.dot(jnp.concatenate([vt, ones], axis=0), p,
                            preferred_element_type=F32)

    for it in range(len(items) + DIL_RING - 1):
        if it < len(items):
            scores(it)
        if it >= DIL_RING - 1:
            value_product(it - (DIL_RING - 1))
    for qb in range(DIL_QB):
        lse_t = jnp.zeros((N_HEADS_A, BLOCK), F32)
        for hp in range(n_pairs):
            it = qb * n_pairs + hp
            l = pv_sc[it, LANES:LANES + 1]
            acc = pv_sc[it, :LANES] / l
            o_t = jnp.where(lo_row, acc[:, :BLOCK], acc[:, BLOCK:])
            o_ref[rows(qb), hp * LANES:(hp + 1) * LANES] = o_t.T.astype(o_ref.dtype)
            lse = (m_sc[it] + jnp.log2(l)) * LN2
            lse_t = jnp.where(head_row == 2 * hp, lse[:, :BLOCK], lse_t)
            lse_t = jnp.where(head_row == 2 * hp + 1, lse[:, BLOCK:], lse_t)
        lse_full = jnp.concatenate([lse_t, jnp.zeros((BLOCK - N_HEADS_A, BLOCK), F32)], axis=0)
        lse_ref[rows(qb), :] = lse_full.T


def _dilated_attention(qkv, bias, group, dilation, batch, seq):
    assert seq % (dilation * BLOCK * DIL_QB) == 0
    sub_len = seq // dilation
    steps = sub_len // (BLOCK * DIL_QB)
    qkv_v = qkv.reshape(batch, sub_len, 3 * dilation * D_MODEL)

    def col(c):
        return lambda b, r, n: (b, n, c * dilation + r)

    def col_prev(c):
        return lambda b, r, n: (b, jnp.maximum(n * DIL_QB - 1, 0), c * dilation + r)

    blk = (None, DIL_QB * BLOCK, D_MODEL)
    blk_prev = (None, BLOCK, D_MODEL)
    n_items = DIL_QB * N_HEADS_A // 2
    o, lse = pl.pallas_call(
        _dilated_kernel,
        grid=(batch, dilation, steps),
        in_specs=[pl.BlockSpec(blk, col(0)),
                  pl.BlockSpec(blk_prev, col_prev(1)),
                  pl.BlockSpec(blk, col(1)),
                  pl.BlockSpec(blk_prev, col_prev(2)),
                  pl.BlockSpec(blk, col(2)),
                  pl.BlockSpec((None, 2, N_HEADS_A // 2, 2 * BLOCK, 2 * BLOCK),
                               lambda b, r, n: (group, 0, 0, 0, 0))],
        out_specs=[pl.BlockSpec(blk, lambda b, r, n: (b, n, r)),
                   pl.BlockSpec((None, DIL_QB * BLOCK, LANES), lambda b, r, n: (b, n, r))],
        out_shape=[jax.ShapeDtypeStruct((batch, sub_len, dilation * D_MODEL), BF16),
                   jax.ShapeDtypeStruct((batch, sub_len, dilation * LANES), F32)],
        scratch_shapes=[pltpu.VMEM((DIL_RING, 2 * BLOCK, 2 * BLOCK), F32),
                        pltpu.VMEM((n_items, 1, 2 * BLOCK), F32),
                        pltpu.VMEM((n_items, LANES + SUM_ROWS, 2 * BLOCK), F32)],
        compiler_params=_cparams(("parallel", "parallel", "arbitrary")),
        name=f"dilated_attn_d{dilation}",
    )(qkv_v, qkv_v, qkv_v, qkv_v, qkv_v, bias)
    return (o.reshape(batch * sub_len, dilation * D_MODEL),
            lse.reshape(batch * sub_len, dilation * LANES))


def _combine_proj_kernel(h_ref, o0_ref, o1_ref, o2_ref, l0_ref, l1_ref, l2_ref,
                         e_ref, w_ref, out_ref, lse_sc, o_sc):
    o_refs = [o0_ref, o1_ref, o2_ref]
    l_refs = [l0_ref, l1_ref, l2_ref]
    tm, d = h_ref.shape
    n_slabs = d // LANES
    for g, (_, dilation) in enumerate(DIL_CONFIGS):
        rows = tm // dilation
        for r in range(dilation):
            dst = pl.ds(r, rows, stride=dilation) if dilation > 1 else slice(None)
            lse_sc[g, dst, :] = l_refs[g][:, r * LANES:(r + 1) * LANES]
            for c in range(n_slabs):
                o_sc[g, c, dst, :] = o_refs[g][:, r * d + c * LANES:r * d + (c + 1) * LANES].astype(F32)
    lses = [lse_sc[g] for g in range(N_GROUPS)]
    mx = jnp.maximum(jnp.maximum(lses[0], lses[1]), lses[2])
    ws = [jnp.exp(l - mx) for l in lses]
    tot = ws[0] + ws[1] + ws[2]
    head_lane = lax.broadcasted_iota(jnp.int32, (tm, LANES), 1) < N_HEADS_A
    packed = None
    for g in range(N_GROUPS):
        a = jnp.where(head_lane, ws[g] / tot, 0.0)
        a_hi = a.astype(BF16).astype(F32)
        for part, piece in enumerate((a_hi, a - a_hi)):
            shift = (part * N_GROUPS + g) * N_HEADS_A
            moved = piece if shift == 0 else pltpu.roll(piece, shift, axis=1)
            packed = moved if packed is None else packed + moved
    ae_all = jnp.dot(packed.astype(BF16), e_ref[...], preferred_element_type=F32)
    aes = [ae_all[:, g * d:(g + 1) * d] for g in range(N_GROUPS)]
    slabs = []
    for c in range(n_slabs):
        cols = slice(c * LANES, (c + 1) * LANES)
        slabs.append(aes[0][:, cols] * o_sc[0, c] + aes[1][:, cols] * o_sc[1, c]
                     + aes[2][:, cols] * o_sc[2, c])
    o = jnp.concatenate(slabs, axis=1).astype(BF16)
    out_ref[...] = h_ref[...] + jnp.dot(o, w_ref[...], preferred_element_type=F32)


def _combine_proj(h, outs, lses, w, *, tm=512):
    n, d = h.shape
    rows = jnp.arange(LANES, dtype=jnp.int32)
    cols = jnp.arange(N_GROUPS * d, dtype=jnp.int32)
    row_group = (rows // N_HEADS_A) % N_GROUPS
    row_head = rows % N_HEADS_A
    row_used = rows < 2 * N_GROUPS * N_HEADS_A
    expand = (row_used[:, None] & (row_group[:, None] == (cols // d)[None, :])
              & (row_head[:, None] == ((cols % d) // HEAD_DIM)[None, :])).astype(BF16)
    row = lambda i: (i, 0)
    full = lambda i: (0, 0)
    dils = [dilation for _, dilation in DIL_CONFIGS]
    assert all(tm % (16 * dilation) == 0 for dilation in dils)
    return pl.pallas_call(
        _combine_proj_kernel,
        grid=(n // tm,),
        in_specs=[pl.BlockSpec((tm, d), row)]
                 + [pl.BlockSpec((tm // dilation, dilation * d), row) for dilation in dils]
                 + [pl.BlockSpec((tm // dilation, dilation * LANES), row) for dilation in dils]
                 + [pl.BlockSpec((LANES, N_GROUPS * d), full), pl.BlockSpec((d, d), full)],
        out_specs=pl.BlockSpec((tm, d), row),
        out_shape=jax.ShapeDtypeStruct((n, d), F32),
        scratch_shapes=[pltpu.VMEM((N_GROUPS, tm, LANES), F32),
                        pltpu.VMEM((N_GROUPS, d // LANES, tm, LANES), F32)],
        compiler_params=_cparams(("parallel",)),
        name="combine_proj",
    )(h, *outs, *lses, expand, w)


def _mlp_ple_kernel(*refs, final_norm, with_proj, n_steps, chunk):
    if with_proj:
        h_ref, o_ref, wo_ref = refs[:3]
        refs = refs[3:]
    else:
        h_ref = refs[0]
        refs = refs[1:]
    p_ref, g_ref, w1_ref, w2_ref, gp_ref, wg_ref, wp_ref, fg_ref, out_ref, xn_ref, acc_ref = refs
    f = pl.program_id(1)
    tm = h_ref.shape[0]

    def step(first, last):
        for ch in range(tm // chunk):
            rows = slice(ch * chunk, (ch + 1) * chunk)
            if first:
                if with_proj:
                    h0 = h_ref[rows, :] + jnp.dot(o_ref[rows, :], wo_ref[...],
                                                  preferred_element_type=F32)
                    out_ref[rows, :] = h0
                else:
                    h0 = h_ref[rows, :]
                xn_ref[rows, :] = _rmsnorm_f32(h0, g_ref[...]).astype(BF16)
            a = jnp.dot(xn_ref[rows, :], w1_ref[...], preferred_element_type=F32)
            a = jnp.maximum(a, 0.0)
            a = (a * a).astype(BF16)
            acc = jnp.dot(a, w2_ref[...], preferred_element_type=F32)
            if not first:
                acc = acc_ref[rows, :] + acc
            if not last:
                acc_ref[rows, :] = acc
                continue
            x = (out_ref[rows, :] if with_proj else h_ref[rows, :]) + acc
            xn = _rmsnorm_f32(x, gp_ref[...]).astype(BF16)
            gate = jax.nn.sigmoid(jnp.dot(xn, wg_ref[...], preferred_element_type=F32))
            proj = jnp.dot(p_ref[rows, :].astype(BF16), wp_ref[...], preferred_element_type=F32)
            y = x + gate * proj
            if final_norm:
                y = _rmsnorm_f32(y, fg_ref[...])
            out_ref[rows, :] = y

    pl.when(f == 0)(functools.partial(step, True, n_steps == 1))
    if n_steps > 2:
        pl.when((f > 0) & (f < n_steps - 1))(functools.partial(step, False, False))
    if n_steps > 1:
        pl.when(f == n_steps - 1)(functools.partial(step, False, True))


def _mlp_ple(h, p_all, layer, g, w1, w2, gp, wg, wp, fg, *, final_norm, proj=None, tm=1024, tf=1024):
    n, d = h.shape
    dff = w1.shape[1]
    pd = p_all.shape[-1]
    vec = lambda i, f: (0, 0)
    row = lambda i, f: (i, 0)
    lead_specs = [pl.BlockSpec((tm, d), row)]
    lead_args = [h]
    if proj is not None:
        o, wo = proj
        lead_specs += [pl.BlockSpec((tm, o.shape[1]), row), pl.BlockSpec(wo.shape, vec)]
        lead_args += [o, wo]
    return pl.pallas_call(
        functools.partial(_mlp_ple_kernel, final_norm=final_norm, with_proj=proj is not None,
                          n_steps=dff // tf, chunk=MLP_CHUNK),
        grid=(n // tm, dff // tf),
        in_specs=lead_specs + [
                  pl.BlockSpec((None, tm, pd), lambda i, f: (layer, i, 0)),
                  pl.BlockSpec((1, d), vec),
                  pl.BlockSpec((d, tf), lambda i, f: (0, f)),
                  pl.BlockSpec((tf, d), lambda i, f: (f, 0)),
                  pl.BlockSpec((1, d), vec),
                  pl.BlockSpec((d, d), vec),
                  pl.BlockSpec((pd, d), vec),
                  pl.BlockSpec((1, d), vec)],
        out_specs=pl.BlockSpec((tm, d), lambda i, f: (i, 0)),
        out_shape=jax.ShapeDtypeStruct((n, d), F32),
        scratch_shapes=[pltpu.VMEM((tm, d), BF16), pltpu.VMEM((tm, d), F32)],
        compiler_params=_cparams(("parallel", "arbitrary"),
                                 vmem_limit=VMEM_LIMIT_PROJ if proj is not None else VMEM_LIMIT),
        name="mlp_ple",
    )(*lead_args, p_all, g.reshape(1, d), w1, w2, gp.reshape(1, d), wg, wp, fg.reshape(1, d))


def _diff_attn_kernel(lq1_ref, lk1_ref, lq2_ref, lk2_ref, q_ref, qn_ref, k_ref, vt_ref, u_ref, sg_ref,
                      o_ref, m_sc, acc_sc, sa_sc, samax_sc, sb_sc, sbmax_sc, *, bq, bk, lambda_init):
    n_parts = bq // bk
    assert bq == n_parts * bk and n_parts % 2 == 0
    i_q = pl.program_id(2)
    q0 = i_q * bq
    nt = (((1,), (1,)), ((), ()))
    lane = lax.broadcasted_iota(jnp.int32, (bq, LANES), 1)

    def query_columns(ref):
        qf = ref[...].astype(F32)
        a = jnp.where(lane < HEAD_DIM, qf, 0.0)
        b = jnp.where(lane >= HEAD_DIM, qf, 0.0)
        pieces = []
        for part in range(n_parts):
            pieces += [a[part * bk:(part + 1) * bk], b[part * bk:(part + 1) * bk]]
        return jnp.concatenate(pieces, axis=0).astype(BF16)

    q2 = query_columns(q_ref)
    q2_next = query_columns(qn_ref)

    m_sc[...] = jnp.full(m_sc.shape, NEG, F32)
    acc_sc[...] = jnp.zeros(acc_sc.shape, F32)

    per_map = bk // LANES
    part_blocks = 2 * per_map
    n_blocks = n_parts * part_blocks
    first_diag = q0 // bk
    ones = jnp.ones((SUM_ROWS, bk), BF16)

    def scores(c, q2x, q0x, s_ref, smax_ref, first_block=0):
        k0 = pl.multiple_of(c * bk, bk)
        lanes = slice(first_block * LANES, n_blocks * LANES)
        s = lax.dot_general(k_ref[pl.ds(k0, bk), :], q2x[lanes], nt,
                            preferred_element_type=F32)
        rows = []
        for jb in range(bk // LANES):
            tiles = []
            for ib in range(first_block, n_blocks):
                part, within = divmod(ib, part_blocks)
                col, blk = divmod(within, per_map)
                i0 = part * bk + blk * LANES
                t = lax.shift_right_arithmetic(q0x + i0 - k0 - jb * LANES, LOG2_LANES)
                t = jnp.where(t < 0, DIFF_MASK_TILE, jnp.minimum(t, DIFF_CONST_TILE))
                tiles.append(u_ref[t, col])
            rows.append(jnp.concatenate(tiles, axis=1))
        s = s + jnp.concatenate(rows, axis=0)
        s_ref[:, lanes] = s
        smax_ref[:, lanes] = jnp.max(s, axis=0, keepdims=True)

    def accumulate(c, s_ref, smax_ref, first_block=0):
        lanes = slice(first_block * LANES, n_blocks * LANES)
        m_prev = m_sc[:, lanes]
        m_new = jnp.maximum(m_prev, smax_ref[:, lanes])
        m_sc[:, lanes] = m_new
        p = jnp.exp2(s_ref[:, lanes] - m_new).astype(BF16)
        v1 = jnp.concatenate([vt_ref[c], ones], axis=0)
        acc_sc[:, lanes] = (jnp.exp2(m_prev - m_new) * acc_sc[:, lanes]
                            + jnp.dot(v1, p, preferred_element_type=F32))

    @pl.when(i_q == 0)
    def _():
        scores(0, q2, q0, sa_sc, samax_sc)

    def body(i, carry):
        scores(2 * i + 1, q2, q0, sb_sc, sbmax_sc)
        accumulate(2 * i, sa_sc, samax_sc)
        scores(2 * i + 2, q2, q0, sa_sc, samax_sc)
        accumulate(2 * i + 1, sb_sc, sbmax_sc)
        return carry

    lax.fori_loop(0, first_diag // 2, body, 0)

    for dt in range(n_parts // 2):
        ja, jb = 2 * dt, 2 * dt + 1
        scores(first_diag + jb, q2, q0, sb_sc, sbmax_sc, first_block=jb * part_blocks)
        accumulate(first_diag + ja, sa_sc, samax_sc, first_block=ja * part_blocks)
        if jb + 1 < n_parts:
            scores(first_diag + jb + 1, q2, q0, sa_sc, samax_sc, first_block=(jb + 1) * part_blocks)
        else:
            scores(0, q2_next, q0 + bq, sa_sc, samax_sc)
        accumulate(first_diag + jb, sb_sc, sbmax_sc, first_block=jb * part_blocks)

    lam = (jnp.exp(jnp.sum(lq1_ref[...] * lk1_ref[...], keepdims=True))
           - jnp.exp(jnp.sum(lq2_ref[...] * lk2_ref[...], keepdims=True)) + lambda_init)
    accl = acc_sc[...]
    acc = accl[:LANES] / accl[LANES:LANES + 1]
    map0 = jnp.concatenate([acc[:, 2 * part * bk:(2 * part + 1) * bk] for part in range(n_parts)], axis=1)
    map1 = jnp.concatenate([acc[:, (2 * part + 1) * bk:(2 * part + 2) * bk] for part in range(n_parts)], axis=1)
    o = (map0 - lam * map1).T
    y = _rmsnorm_f32(o, sg_ref[...]) * (1.0 - lambda_init)
    o_ref[...] = y.astype(o_ref.dtype)


def _diff_attention(qk, vt, u, lq1, lk1, lq2, lk2, subln, lambda_init, batch, seq, *, bq, bk):
    nh = N_HEADS_B
    nkt = seq // bk
    nq = seq // bq
    vec = lambda a: a.reshape(1, -1)
    small = pl.BlockSpec((1, HEAD_DIM), lambda b, h, i: (0, 0))
    return pl.pallas_call(
        functools.partial(_diff_attn_kernel, bq=bq, bk=bk, lambda_init=lambda_init),
        grid=(batch, nh, seq // bq),
        in_specs=[small, small, small, small,
                  pl.BlockSpec((None, bq, LANES), lambda b, h, i: (b, i, h)),
                  pl.BlockSpec((None, bq, LANES), lambda b, h, i: (b, jnp.minimum(i + 1, nq - 1), h)),
                  pl.BlockSpec((None, seq, LANES), lambda b, h, i: (b, 0, nh + h)),
                  pl.BlockSpec((nkt, LANES, bk), lambda b, h, i: (b, h, 0)),
                  pl.BlockSpec((DIFF_N_TILES, 2, LANES, LANES), lambda b, h, i: (0, h, 0, 0)),
                  pl.BlockSpec((1, LANES), lambda b, h, i: (0, 0))],
        out_specs=pl.BlockSpec((None, bq, LANES), lambda b, h, i: (b, i, h)),
        out_shape=jax.ShapeDtypeStruct((batch, seq, D_MODEL), BF16),
        scratch_shapes=[pltpu.VMEM((1, 2 * bq), F32),
                        pltpu.VMEM((LANES + SUM_ROWS, 2 * bq), F32),
                        pltpu.VMEM((bk, 2 * bq), F32),
                        pltpu.VMEM((1, 2 * bq), F32),
                        pltpu.VMEM((bk, 2 * bq), F32),
                        pltpu.VMEM((1, 2 * bq), F32)],
        compiler_params=_cparams(("parallel", "parallel", "arbitrary")),
        name="diff_attn",
    )(vec(lq1), vec(lk1), vec(lq2), vec(lk2), qk, qk, qk, vt, u, vec(subln))


def kernel(x, p, rel_bias, a_w_qkv, a_w_o, b_w_qkv, b_w_o, b_lambda_q1, b_lambda_k1, b_lambda_q2, b_lambda_k2, b_subln, norm_mix, norm_mlp, w_ff1, w_ff2, norm_ple, w_ple_gate, w_ple_proj, final_norm):
    batch, seq, d = x.shape
    depth = p.shape[0]
    n = batch * seq
    h = x.reshape(n, d)
    p_all = p.reshape(depth, n, p.shape[-1])
    n_mixers = 2

    for i in range(depth):
        j = i // n_mixers
        if i % n_mixers == 0:
            w_a = a_w_qkv[j].astype(BF16)
            bias = _build_dilated_bias(rel_bias)
            outs, lses = [], []
            for g, (_, dilation) in enumerate(DIL_CONFIGS):
                qkv = _qkv_dilated(h, norm_mix[i], w_a, g, dilation, tm=max(1024, BLOCK * dilation))
                o_g, lse_g = _dilated_attention(qkv, bias, g, dilation, batch, seq)
                outs.append(o_g)
                lses.append(lse_g)
            h = _combine_proj(h, outs, lses, a_w_o[j].astype(BF16))
            proj = None
        else:
            lambda_init = 0.8 - 0.6 * math.exp(-0.3 * i)
            bq, bk = 2048, 512
            w_b = b_w_qkv[j].astype(BF16)
            qk, vt = _qkv_diff(h, norm_mix[i], w_b, bk=bk)
            u = _build_diff_bias(rel_bias)
            o = _diff_attention(qk.reshape(batch, seq, 2 * d), vt, u,
                                b_lambda_q1[j], b_lambda_k1[j], b_lambda_q2[j], b_lambda_k2[j],
                                b_subln[j], lambda_init, batch, seq, bq=bq, bk=bk)
            proj = (o.reshape(n, d), b_w_o[j].astype(BF16))
        h = _mlp_ple(h, p_all, i, norm_mlp[i], w_ff1[i].astype(BF16), w_ff2[i].astype(BF16),
                     norm_ple[i], w_ple_gate[i].astype(BF16), w_ple_proj[i].astype(BF16),
                     final_norm, final_norm=(i == depth - 1), proj=proj)
    return h.reshape(batch, seq, d)
```

```python
import functools
import math

import jax
import jax.numpy as jnp
from jax import lax
from jax.experimental import pallas as pl
from jax.experimental.pallas import tpu as pltpu

F32 = jnp.float32
BF16 = jnp.bfloat16

D_MODEL = 1024
HEAD_DIM = 64
BLOCK = 128
QKV_CHUNK = 512
MLP_CHUNK = 512
DIL_RING = 6
DIL_QB = 8
DIL_CONFIGS = ((128, 1), (512, 4), (2048, 16))
N_GROUPS = len(DIL_CONFIGS)
N_HEADS_A = D_MODEL // HEAD_DIM
N_HEADS_B = D_MODEL // (2 * HEAD_DIM)
N_BUCKETS = 32
MAX_DISTANCE = 2048
N_BIAS_COLS = 16
EPS = 1e-6
NEG = -1e30
LANES = 128
SUBLANES = 8
LOG2_LANES = 7
QK_SCALE = HEAD_DIM ** -0.5
LOG2E = math.log2(math.e)
LN2 = math.log(2.0)

DIFF_CONST_TILE = (MAX_DISTANCE + LANES - 1) // LANES + 1
DIFF_MASK_TILE = DIFF_CONST_TILE + 1
DIFF_N_TILES = DIFF_MASK_TILE + 1
SUM_ROWS = 16

VMEM_LIMIT = 48 * 1024 * 1024
VMEM_LIMIT_PROJ = 56 * 1024 * 1024


def _cparams(sem, vmem_limit=VMEM_LIMIT):
    return pltpu.CompilerParams(dimension_semantics=sem, vmem_limit_bytes=vmem_limit)


def _rmsnorm_f32(x, g):
    ms = jnp.mean(x * x, axis=-1, keepdims=True)
    return x * lax.rsqrt(ms + EPS) * g


def _rel_bucket(dist):
    n = jnp.maximum(dist, 0)
    max_exact = N_BUCKETS // 2
    nf = jnp.maximum(n, 1).astype(F32)
    large = max_exact + (jnp.log(nf / max_exact) / math.log(MAX_DISTANCE / max_exact)
                         * (N_BUCKETS - max_exact)).astype(jnp.int32)
    large = jnp.minimum(large, N_BUCKETS - 1)
    return jnp.where(n < max_exact, n, large)


def _table_lookup(bucket, tab_ref, col):
    acc = jnp.zeros(bucket.shape, F32)
    for b in range(N_BUCKETS):
        acc = jnp.where(bucket == b, tab_ref[b, col], acc)
    return acc


def _dilated_bias_kernel(tab_ref, o_ref):
    g = pl.program_id(0)
    dilation = jnp.where(g == 0, DIL_CONFIGS[0][1],
                         jnp.where(g == 1, DIL_CONFIGS[1][1], DIL_CONFIGS[2][1]))
    kj = lax.broadcasted_iota(jnp.int32, (2 * BLOCK, BLOCK), 0)
    qi = lax.broadcasted_iota(jnp.int32, (2 * BLOCK, BLOCK), 1)
    sub = qi + BLOCK - kj
    band = (sub >= 0) & (sub <= BLOCK)
    band_first = band & (kj >= BLOCK)
    bucket = _rel_bucket(sub * dilation)
    for c in range(N_BIAS_COLS):
        lanes = slice((c % 2) * BLOCK, (c % 2 + 1) * BLOCK)
        bias = _table_lookup(bucket, tab_ref, c) * LOG2E
        o_ref[0, c // 2, :, lanes] = jnp.where(band, bias, NEG)
        o_ref[1, c // 2, :, lanes] = jnp.where(band_first, bias, NEG)


def _build_dilated_bias(rel_bias):
    return pl.pallas_call(
        _dilated_bias_kernel,
        grid=(N_GROUPS,),
        in_specs=[pl.BlockSpec(memory_space=pltpu.SMEM)],
        out_specs=pl.BlockSpec((None, 2, N_HEADS_A // 2, 2 * BLOCK, 2 * BLOCK),
                               lambda g: (g, 0, 0, 0, 0)),
        out_shape=jax.ShapeDtypeStruct((N_GROUPS, 2, N_HEADS_A // 2, 2 * BLOCK, 2 * BLOCK), F32),
        compiler_params=_cparams(("arbitrary",)),
        name="dilated_bias",
    )(rel_bias)


def _diff_bias_kernel(tab_ref, o_ref):
    t = pl.program_id(0)
    kj = lax.broadcasted_iota(jnp.int32, (LANES, LANES), 0)
    qi = lax.broadcasted_iota(jnp.int32, (LANES, LANES), 1)
    dist = t * LANES + qi - kj
    masked = (dist < 0) | (t == DIFF_MASK_TILE)
    bucket = _rel_bucket(dist)
    init = jnp.where(masked, NEG, 0.0)
    for c in range(N_BIAS_COLS):
        o_ref[c] = init
    live = jnp.where(masked, -1, bucket)

    def body(b, carry):
        hit = live == b
        for c in range(N_BIAS_COLS):
            o_ref[c] = jnp.where(hit, tab_ref[b, c] * LOG2E, o_ref[c])
        return carry

    lax.fori_loop(jnp.min(bucket), jnp.max(bucket) + 1, body, 0)


def _build_diff_bias(rel_bias):
    return pl.pallas_call(
        _diff_bias_kernel,
        grid=(DIFF_N_TILES,),
        in_specs=[pl.BlockSpec(memory_space=pltpu.SMEM)],
        out_specs=pl.BlockSpec((None, N_BIAS_COLS, LANES, LANES), lambda t: (t, 0, 0, 0)),
        out_shape=jax.ShapeDtypeStruct((DIFF_N_TILES, N_BIAS_COLS, LANES, LANES), F32),
        compiler_params=_cparams(("arbitrary",)),
        name="diff_bias",
    )(rel_bias)


def _qkv_dilated_kernel(*refs, dilation, n_slabs, chunk, whole, pre_stride):
    n_x = n_slabs or 1
    x_refs = refs[:n_x]
    g_ref, w_ref, o_ref, xn_sc = refs[n_x:n_x + 4]
    tm, d_model = xn_sc.shape
    rows = tm // dilation
    if pre_stride:
        pre_sc = refs[n_x + 4]
        pre_rows = tm // pre_stride
        outer = dilation // pre_stride

    def pre_regroup():
        for s, x_ref in enumerate(x_refs):
            for b in range(pre_stride):
                pre_sc[s, b * pre_rows:(b + 1) * pre_rows, :] = x_ref[pl.ds(b, pre_rows, stride=pre_stride), :]

    def normalise(r, a, b):
        if n_slabs is None:
            xs = [x_refs[0][a:b, :]]
        elif pre_stride:
            start = (r % pre_stride) * pre_rows + r // pre_stride + (a - r * rows) * outer
            xs = [pre_sc[s, pl.ds(start, b - a, stride=outer), :] for s in range(n_slabs)]
        else:
            xs = [x_ref[pl.ds(r + (a - r * rows) * dilation, b - a, stride=dilation), :]
                  for x_ref in x_refs]
        slab_w = d_model // len(xs)
        sq = xs[0] * xs[0]
        for x in xs[1:]:
            sq = sq + x * x
        scale = lax.rsqrt(jnp.sum(sq, axis=-1, keepdims=True) * (1.0 / d_model) + EPS)
        for s, x in enumerate(xs):
            cols = slice(s * slab_w, (s + 1) * slab_w)
            xn_sc[a:b, cols] = (x * scale * g_ref[:, cols]).astype(BF16)

    def store(c, y, lo, r, a, b, col0):
        if c < 2:
            val = y[a - lo:b - lo]
            if c == 0:
                val = val * (QK_SCALE * LOG2E)
            o_ref[a - r * rows:b - r * rows, col0:col0 + d_model] = val.astype(o_ref.dtype)
            return
        for blk in range((b - a) // BLOCK):
            y0 = a - lo + blk * BLOCK
            s0 = a - r * rows + blk * BLOCK
            for hp in range(d_model // LANES):
                tile = y[y0:y0 + BLOCK, hp * LANES:(hp + 1) * LANES]
                o_ref[s0:s0 + BLOCK, col0 + hp * LANES:col0 + (hp + 1) * LANES] = (
                    tile.T.astype(o_ref.dtype))

    def run(comps, with_norm):
        if with_norm and pre_stride:
            pre_regroup()
        for ch in range(tm // chunk):
            lo, hi = ch * chunk, (ch + 1) * chunk
            pieces = [(r, max(lo, r * rows), min(hi, (r + 1) * rows)) for r in range(dilation)
                      if max(lo, r * rows) < min(hi, (r + 1) * rows)]
            if with_norm:
                for r, a, b in pieces:
                    normalise(r, a, b)
            xn = xn_sc[lo:hi, :]
            for c in comps:
                w = w_ref[:, c * d_model:(c + 1) * d_model] if whole else w_ref[...]
                y = jnp.dot(xn, w, preferred_element_type=F32)
                for r, a, b in pieces:
                    store(c, y, lo, r, a, b, ((c * dilation if whole else 0) + r) * d_model)

    if whole:
        run((0, 1, 2), True)
    else:
        j = pl.program_id(1)
        for c in range(3):
            pl.when(j == c)(functools.partial(run, (c,), c == 0))


def _qkv_dilated(x, g, w, group, dilation, *, tm):
    n, d = x.shape
    assert w.shape[0] == d and w.shape[1] % (3 * d) == 0
    rows = tm // dilation
    assert tm % dilation == 0 and rows % BLOCK == 0
    if dilation == 1:
        n_slabs = None
        x_in = [x]
        x_specs = [pl.BlockSpec((tm, d), lambda i, j: (i, 0))]
    else:
        n_slabs = d // LANES
        x_in = [x] * n_slabs
        x_specs = [pl.BlockSpec((tm, LANES), functools.partial(lambda i, j, s: (i, s), s=s))
                   for s in range(n_slabs)]
    pre_stride = 4 if dilation % 8 == 0 else None
    scratch = [pltpu.VMEM((tm, d), BF16)]
    if pre_stride:
        scratch.append(pltpu.VMEM((n_slabs, tm, LANES), F32))
    out_shape = jax.ShapeDtypeStruct((n // dilation, 3 * dilation * d), BF16)
    whole_bytes = 2 * (tm * 3 * d * 2) + 2 * (d * 3 * d * 2) + 2 * (tm * d * 4) + 4 * QKV_CHUNK * d * 4
    assert tm % QKV_CHUNK == 0 and (QKV_CHUNK % rows == 0 or rows % QKV_CHUNK == 0)
    if whole_bytes <= VMEM_LIMIT:
        return pl.pallas_call(
            functools.partial(_qkv_dilated_kernel, dilation=dilation, n_slabs=n_slabs,
                              chunk=QKV_CHUNK, whole=True, pre_stride=pre_stride),
            grid=(n // tm, 1),
            in_specs=x_specs + [pl.BlockSpec((1, d), lambda i, j: (0, 0)),
                                pl.BlockSpec((d, 3 * d), lambda i, j: (0, group))],
            out_specs=pl.BlockSpec((rows, 3 * dilation * d), lambda i, j: (i, 0)),
            out_shape=out_shape,
            scratch_shapes=scratch,
            compiler_params=_cparams(("parallel", "arbitrary")),
            name=f"qkv_d{dilation}",
        )(*x_in, g.reshape(1, d), w)
    return pl.pallas_call(
        functools.partial(_qkv_dilated_kernel, dilation=dilation, n_slabs=n_slabs,
                          chunk=QKV_CHUNK, whole=False, pre_stride=pre_stride),
        grid=(n // tm, 3),
        in_specs=x_specs + [pl.BlockSpec((1, d), lambda i, j: (0, 0)),
                            pl.BlockSpec((d, d), lambda i, j: (0, 3 * group + j))],
        out_specs=pl.BlockSpec((rows, dilation * d), lambda i, j: (i, j)),
        out_shape=out_shape,
        scratch_shapes=scratch,
        compiler_params=_cparams(("parallel", "arbitrary")),
        name=f"qkv_d{dilation}",
    )(*x_in, g.reshape(1, d), w)


def _qkv_diff_kernel(x_ref, g_ref, w_ref, qk_ref, vt_ref, *, bk):
    tm, d = x_ref.shape
    nt = (((1,), (1,)), ((), ()))
    for c in range(tm // bk):
        rows = slice(c * bk, (c + 1) * bk)
        xn = _rmsnorm_f32(x_ref[rows, :], g_ref[...]).astype(BF16)
        q = jnp.dot(xn, w_ref[0], preferred_element_type=F32) * (QK_SCALE * LOG2E)
        qk_ref[rows, :d] = q.astype(qk_ref.dtype)
        qk_ref[rows, d:] = jnp.dot(xn, w_ref[1], preferred_element_type=F32).astype(qk_ref.dtype)
        vt_ref[c] = lax.dot_general(w_ref[2], xn, nt,
                                    preferred_element_type=F32).astype(vt_ref.dtype)


def _qkv_diff(x, g, w_qkv, *, bk, tm=1024):
    n, d = x.shape
    w3 = jnp.stack([w_qkv[:, :d], w_qkv[:, d:2 * d], w_qkv[:, 2 * d:].T])
    return pl.pallas_call(
        functools.partial(_qkv_diff_kernel, bk=bk),
        grid=(n // tm,),
        in_specs=[pl.BlockSpec((tm, d), lambda i: (i, 0)),
                  pl.BlockSpec((1, d), lambda i: (0, 0)),
                  pl.BlockSpec((3, d, d), lambda i: (0, 0, 0))],
        out_specs=[pl.BlockSpec((tm, 2 * d), lambda i: (i, 0)),
                   pl.BlockSpec((tm // bk, d, bk), lambda i: (i, 0, 0))],
        out_shape=[jax.ShapeDtypeStruct((n, 2 * d), BF16),
                   jax.ShapeDtypeStruct((n // bk, d, bk), BF16)],
        compiler_params=_cparams(("parallel",)),
        name="qkv_diff",
    )(x, g.reshape(1, d), w3)


def _dilated_kernel(q_ref, kp_ref, kc_ref, vtp_ref, vtc_ref, bias_ref, o_ref, lse_ref,
                    s_sc, m_sc, pv_sc):
    first_step = (pl.program_id(2) == 0).astype(jnp.int32)
    lane = lax.broadcasted_iota(jnp.int32, (BLOCK, LANES), 1)
    row = lax.broadcasted_iota(jnp.int32, (BLOCK, LANES), 0)
    head_row = lax.broadcasted_iota(jnp.int32, (N_HEADS_A, BLOCK), 0)
    lo_row = row < HEAD_DIM
    mask_lo = jnp.where(lane < HEAD_DIM, 1.0, 0.0).astype(BF16)
    mask_hi = jnp.where(lane < HEAD_DIM, 0.0, 1.0).astype(BF16)
    nt = (((1,), (1,)), ((), ()))
    ones = jnp.ones((SUM_ROWS, 2 * BLOCK), BF16)
    n_pairs = N_HEADS_A // 2
    n_qb = q_ref.shape[0] // BLOCK
    items = [(qb, hp) for qb in range(n_qb) for hp in range(n_pairs)]

    def rows(qb):
        return slice(qb * BLOCK, (qb + 1) * BLOCK)

    def prev_cur(prev_ref, cur_ref, qb, sl):
        prev = prev_ref[:, sl] if qb == 0 else cur_ref[rows(qb - 1), sl]
        return prev, cur_ref[rows(qb), sl]

    def scores(it):
        qb, hp = items[it]
        sl = slice(hp * LANES, (hp + 1) * LANES)
        q = q_ref[rows(qb), sl]
        q2 = jnp.concatenate([q * mask_lo, q * mask_hi], axis=0)
        k = jnp.concatenate(prev_cur(kp_ref, kc_ref, qb, sl), axis=0)
        s = lax.dot_general(k, q2, nt, preferred_element_type=F32)
        s = s + bias_ref[first_step if qb == 0 else 0, hp]
        s_sc[it % DIL_RING] = s
        m_sc[it] = jnp.max(s, axis=0, keepdims=True)

    def value_product(it):
        qb, hp = items[it]
        sl = slice(hp * LANES, (hp + 1) * LANES)
        p = jnp.exp2(s_sc[it % DIL_RING] - m_sc[it]).astype(BF16)
        vt = jnp.concatenate(prev_cur(vtp_ref, vtc_ref, qb, sl), axis=1)
        pv_sc[it] = jnp.dot(jnp.concatenate([vt, ones], axis=0), p,
                            preferred_element_type=F32)

    for it in range(len(items) + DIL_RING - 1):
        if it < len(items):
            scores(it)
        if it >= DIL_RING - 1:
            value_product(it - (DIL_RING - 1))
    for qb in range(n_qb):
        lse_t = jnp.zeros((N_HEADS_A, BLOCK), F32)
        for hp in range(n_pairs):
            it = qb * n_pairs + hp
            l = pv_sc[it, LANES:LANES + 1]
            acc = pv_sc[it, :LANES] / l
            o_t = jnp.where(lo_row, acc[:, :BLOCK], acc[:, BLOCK:])
            o_ref[rows(qb), hp * LANES:(hp + 1) * LANES] = o_t.T.astype(o_ref.dtype)
            lse = (m_sc[it] + jnp.log2(l)) * LN2
            lse_t = jnp.where(head_row == 2 * hp, lse[:, :BLOCK], lse_t)
            lse_t = jnp.where(head_row == 2 * hp + 1, lse[:, BLOCK:], lse_t)
        lse_full = jnp.concatenate([lse_t, jnp.zeros((BLOCK - N_HEADS_A, BLOCK), F32)], axis=0)
        lse_ref[rows(qb), :] = lse_full.T


def _dilated_attention(qkv, bias, group, dilation, batch, seq):
    assert seq % (dilation * BLOCK) == 0
    sub_len = seq // dilation
    n_qb = min(DIL_QB, sub_len // BLOCK)
    assert sub_len % (BLOCK * n_qb) == 0
    steps = sub_len // (BLOCK * n_qb)
    qkv_v = qkv.reshape(batch, sub_len, 3 * dilation * D_MODEL)

    def col(c):
        return lambda b, r, n: (b, n, c * dilation + r)

    def col_prev(c):
        return lambda b, r, n: (b, jnp.maximum(n * n_qb - 1, 0), c * dilation + r)

    blk = (None, n_qb * BLOCK, D_MODEL)
    blk_prev = (None, BLOCK, D_MODEL)
    n_items = n_qb * N_HEADS_A // 2
    o, lse = pl.pallas_call(
        _dilated_kernel,
        grid=(batch, dilation, steps),
        in_specs=[pl.BlockSpec(blk, col(0)),
                  pl.BlockSpec(blk_prev, col_prev(1)),
                  pl.BlockSpec(blk, col(1)),
                  pl.BlockSpec(blk_prev, col_prev(2)),
                  pl.BlockSpec(blk, col(2)),
                  pl.BlockSpec((None, 2, N_HEADS_A // 2, 2 * BLOCK, 2 * BLOCK),
                               lambda b, r, n: (group, 0, 0, 0, 0))],
        out_specs=[pl.BlockSpec(blk, lambda b, r, n: (b, n, r)),
                   pl.BlockSpec((None, n_qb * BLOCK, LANES), lambda b, r, n: (b, n, r))],
        out_shape=[jax.ShapeDtypeStruct((batch, sub_len, dilation * D_MODEL), BF16),
                   jax.ShapeDtypeStruct((batch, sub_len, dilation * LANES), F32)],
        scratch_shapes=[pltpu.VMEM((DIL_RING, 2 * BLOCK, 2 * BLOCK), F32),
                        pltpu.VMEM((n_items, 1, 2 * BLOCK), F32),
                        pltpu.VMEM((n_items, LANES + SUM_ROWS, 2 * BLOCK), F32)],
        compiler_params=_cparams(("parallel", "parallel", "arbitrary")),
        name=f"dilated_attn_d{dilation}",
    )(qkv_v, qkv_v, qkv_v, qkv_v, qkv_v, bias)
    return (o.reshape(batch * sub_len, dilation * D_MODEL),
            lse.reshape(batch * sub_len, dilation * LANES))


def _combine_proj_kernel(h_ref, o0_ref, o1_ref, o2_ref, l0_ref, l1_ref, l2_ref,
                         e_ref, w_ref, out_ref, lse_sc, o_sc):
    o_refs = [o0_ref, o1_ref, o2_ref]
    l_refs = [l0_ref, l1_ref, l2_ref]
    tm, d = h_ref.shape
    n_slabs = d // LANES
    for g, (_, dilation) in enumerate(DIL_CONFIGS):
        rows = tm // dilation
        for r in range(dilation):
            dst = pl.ds(r, rows, stride=dilation) if dilation > 1 else slice(None)
            lse_sc[g, dst, :] = l_refs[g][:, r * LANES:(r + 1) * LANES]
            for c in range(n_slabs):
                o_sc[g, c, dst, :] = o_refs[g][:, r * d + c * LANES:r * d + (c + 1) * LANES].astype(F32)
    lses = [lse_sc[g] for g in range(N_GROUPS)]
    mx = jnp.maximum(jnp.maximum(lses[0], lses[1]), lses[2])
    ws = [jnp.exp(l - mx) for l in lses]
    tot = ws[0] + ws[1] + ws[2]
    head_lane = lax.broadcasted_iota(jnp.int32, (tm, LANES), 1) < N_HEADS_A
    packed = None
    for g in range(N_GROUPS):
        a = jnp.where(head_lane, ws[g] / tot, 0.0)
        a_hi = a.astype(BF16).astype(F32)
        for part, piece in enumerate((a_hi, a - a_hi)):
            shift = (part * N_GROUPS + g) * N_HEADS_A
            moved = piece if shift == 0 else pltpu.roll(piece, shift, axis=1)
            packed = moved if packed is None else packed + moved
    ae_all = jnp.dot(packed.astype(BF16), e_ref[...], preferred_element_type=F32)
    aes = [ae_all[:, g * d:(g + 1) * d] for g in range(N_GROUPS)]
    slabs = []
    for c in range(n_slabs):
        cols = slice(c * LANES, (c + 1) * LANES)
        slabs.append(aes[0][:, cols] * o_sc[0, c] + aes[1][:, cols] * o_sc[1, c]
                     + aes[2][:, cols] * o_sc[2, c])
    o = jnp.concatenate(slabs, axis=1).astype(BF16)
    out_ref[...] = h_ref[...] + jnp.dot(o, w_ref[...], preferred_element_type=F32)


def _combine_proj(h, outs, lses, w, *, tm=512):
    n, d = h.shape
    rows = jnp.arange(LANES, dtype=jnp.int32)
    cols = jnp.arange(N_GROUPS * d, dtype=jnp.int32)
    row_group = (rows // N_HEADS_A) % N_GROUPS
    row_head = rows % N_HEADS_A
    row_used = rows < 2 * N_GROUPS * N_HEADS_A
    expand = (row_used[:, None] & (row_group[:, None] == (cols // d)[None, :])
              & (row_head[:, None] == ((cols % d) // HEAD_DIM)[None, :])).astype(BF16)
    row = lambda i: (i, 0)
    full = lambda i: (0, 0)
    dils = [dilation for _, dilation in DIL_CONFIGS]
    assert all(tm % (16 * dilation) == 0 for dilation in dils)
    return pl.pallas_call(
        _combine_proj_kernel,
        grid=(n // tm,),
        in_specs=[pl.BlockSpec((tm, d), row)]
                 + [pl.BlockSpec((tm // dilation, dilation * d), row) for dilation in dils]
                 + [pl.BlockSpec((tm // dilation, dilation * LANES), row) for dilation in dils]
                 + [pl.BlockSpec((LANES, N_GROUPS * d), full), pl.BlockSpec((d, d), full)],
        out_specs=pl.BlockSpec((tm, d), row),
        out_shape=jax.ShapeDtypeStruct((n, d), F32),
        scratch_shapes=[pltpu.VMEM((N_GROUPS, tm, LANES), F32),
                        pltpu.VMEM((N_GROUPS, d // LANES, tm, LANES), F32)],
        compiler_params=_cparams(("parallel",)),
        name="combine_proj",
    )(h, *outs, *lses, expand, w)


def _mlp_ple_kernel(*refs, final_norm, with_proj, n_steps, chunk):
    if with_proj:
        h_ref, o_ref, wo_ref = refs[:3]
        refs = refs[3:]
    else:
        h_ref = refs[0]
        refs = refs[1:]
    p_ref, g_ref, w1_ref, w2_ref, gp_ref, wg_ref, wp_ref, fg_ref, out_ref, xn_ref, acc_ref = refs
    f = pl.program_id(1)
    tm = h_ref.shape[0]

    def step(first, last):
        for ch in range(tm // chunk):
            rows = slice(ch * chunk, (ch + 1) * chunk)
            if first:
                if with_proj:
                    h0 = h_ref[rows, :] + jnp.dot(o_ref[rows, :], wo_ref[...],
                                                  preferred_element_type=F32)
                    out_ref[rows, :] = h0
                else:
                    h0 = h_ref[rows, :]
                xn_ref[rows, :] = _rmsnorm_f32(h0, g_ref[...]).astype(BF16)
            a = jnp.dot(xn_ref[rows, :], w1_ref[...], preferred_element_type=F32)
            a = jnp.maximum(a, 0.0)
            a = (a * a).astype(BF16)
            acc = jnp.dot(a, w2_ref[...], preferred_element_type=F32)
            if not first:
                acc = acc_ref[rows, :] + acc
            if not last:
                acc_ref[rows, :] = acc
                continue
            x = (out_ref[rows, :] if with_proj else h_ref[rows, :]) + acc
            xn = _rmsnorm_f32(x, gp_ref[...]).astype(BF16)
            gate = jax.nn.sigmoid(jnp.dot(xn, wg_ref[...], preferred_element_type=F32))
            proj = jnp.dot(p_ref[rows, :].astype(BF16), wp_ref[...], preferred_element_type=F32)
            y = x + gate * proj
            if final_norm:
                y = _rmsnorm_f32(y, fg_ref[...])
            out_ref[rows, :] = y

    pl.when(f == 0)(functools.partial(step, True, n_steps == 1))
    if n_steps > 2:
        pl.when((f > 0) & (f < n_steps - 1))(functools.partial(step, False, False))
    if n_steps > 1:
        pl.when(f == n_steps - 1)(functools.partial(step, False, True))


def _mlp_ple(h, p_all, layer, g, w1, w2, gp, wg, wp, fg, *, final_norm, proj=None, tm=1024, tf=1024):
    n, d = h.shape
    dff = w1.shape[1]
    pd = p_all.shape[-1]
    vec = lambda i, f: (0, 0)
    row = lambda i, f: (i, 0)
    lead_specs = [pl.BlockSpec((tm, d), row)]
    lead_args = [h]
    if proj is not None:
        o, wo = proj
        lead_specs += [pl.BlockSpec((tm, o.shape[1]), row), pl.BlockSpec(wo.shape, vec)]
        lead_args += [o, wo]
    return pl.pallas_call(
        functools.partial(_mlp_ple_kernel, final_norm=final_norm, with_proj=proj is not None,
                          n_steps=dff // tf, chunk=MLP_CHUNK),
        grid=(n // tm, dff // tf),
        in_specs=lead_specs + [
                  pl.BlockSpec((None, tm, pd), lambda i, f: (layer, i, 0)),
                  pl.BlockSpec((1, d), vec),
                  pl.BlockSpec((d, tf), lambda i, f: (0, f)),
                  pl.BlockSpec((tf, d), lambda i, f: (f, 0)),
                  pl.BlockSpec((1, d), vec),
                  pl.BlockSpec((d, d), vec),
                  pl.BlockSpec((pd, d), vec),
                  pl.BlockSpec((1, d), vec)],
        out_specs=pl.BlockSpec((tm, d), lambda i, f: (i, 0)),
        out_shape=jax.ShapeDtypeStruct((n, d), F32),
        scratch_shapes=[pltpu.VMEM((tm, d), BF16), pltpu.VMEM((tm, d), F32)],
        compiler_params=_cparams(("parallel", "arbitrary"),
                                 vmem_limit=VMEM_LIMIT_PROJ if proj is not None else VMEM_LIMIT),
        name="mlp_ple",
    )(*lead_args, p_all, g.reshape(1, d), w1, w2, gp.reshape(1, d), wg, wp, fg.reshape(1, d))


def _diff_attn_kernel(lq1_ref, lk1_ref, lq2_ref, lk2_ref, q_ref, qn_ref, k_ref, vt_ref, u_ref, sg_ref,
                      o_ref, m_sc, acc_sc, sa_sc, samax_sc, sb_sc, sbmax_sc, *, bq, bk, lambda_init):
    n_parts = bq // bk
    assert bq == n_parts * bk and n_parts % 2 == 0
    i_q = pl.program_id(2)
    q0 = i_q * bq
    nt = (((1,), (1,)), ((), ()))
    lane = lax.broadcasted_iota(jnp.int32, (bq, LANES), 1)

    def query_columns(ref):
        qf = ref[...].astype(F32)
        a = jnp.where(lane < HEAD_DIM, qf, 0.0)
        b = jnp.where(lane >= HEAD_DIM, qf, 0.0)
        pieces = []
        for part in range(n_parts):
            pieces += [a[part * bk:(part + 1) * bk], b[part * bk:(part + 1) * bk]]
        return jnp.concatenate(pieces, axis=0).astype(BF16)

    q2 = query_columns(q_ref)
    q2_next = query_columns(qn_ref)

    m_sc[...] = jnp.full(m_sc.shape, NEG, F32)
    acc_sc[...] = jnp.zeros(acc_sc.shape, F32)

    per_map = bk // LANES
    part_blocks = 2 * per_map
    n_blocks = n_parts * part_blocks
    first_diag = q0 // bk
    ones = jnp.ones((SUM_ROWS, bk), BF16)

    def scores(c, q2x, q0x, s_ref, smax_ref, first_block=0):
        k0 = pl.multiple_of(c * bk, bk)
        lanes = slice(first_block * LANES, n_blocks * LANES)
        s = lax.dot_general(k_ref[pl.ds(k0, bk), :], q2x[lanes], nt,
                            preferred_element_type=F32)
        rows = []
        for jb in range(bk // LANES):
            tiles = []
            for ib in range(first_block, n_blocks):
                part, within = divmod(ib, part_blocks)
                col, blk = divmod(within, per_map)
                i0 = part * bk + blk * LANES
                t = lax.shift_right_arithmetic(q0x + i0 - k0 - jb * LANES, LOG2_LANES)
                t = jnp.where(t < 0, DIFF_MASK_TILE, jnp.minimum(t, DIFF_CONST_TILE))
                tiles.append(u_ref[t, col])
            rows.append(jnp.concatenate(tiles, axis=1))
        s = s + jnp.concatenate(rows, axis=0)
        s_ref[:, lanes] = s
        smax_ref[:, lanes] = jnp.max(s, axis=0, keepdims=True)

    def accumulate(c, s_ref, smax_ref, first_block=0):
        lanes = slice(first_block * LANES, n_blocks * LANES)
        m_prev = m_sc[:, lanes]
        m_new = jnp.maximum(m_prev, smax_ref[:, lanes])
        m_sc[:, lanes] = m_new
        p = jnp.exp2(s_ref[:, lanes] - m_new).astype(BF16)
        v1 = jnp.concatenate([vt_ref[c], ones], axis=0)
        acc_sc[:, lanes] = (jnp.exp2(m_prev - m_new) * acc_sc[:, lanes]
                            + jnp.dot(v1, p, preferred_element_type=F32))

    @pl.when(i_q == 0)
    def _():
        scores(0, q2, q0, sa_sc, samax_sc)

    def body(i, carry):
        scores(2 * i + 1, q2, q0, sb_sc, sbmax_sc)
        accumulate(2 * i, sa_sc, samax_sc)
        scores(2 * i + 2, q2, q0, sa_sc, samax_sc)
        accumulate(2 * i + 1, sb_sc, sbmax_sc)
        return carry

    lax.fori_loop(0, first_diag // 2, body, 0)

    for dt in range(n_parts // 2):
        ja, jb = 2 * dt, 2 * dt + 1
        scores(first_diag + jb, q2, q0, sb_sc, sbmax_sc, first_block=jb * part_blocks)
        accumulate(first_diag + ja, sa_sc, samax_sc, first_block=ja * part_blocks)
        if jb + 1 < n_parts:
            scores(first_diag + jb + 1, q2, q0, sa_sc, samax_sc, first_block=(jb + 1) * part_blocks)
        else:
            scores(0, q2_next, q0 + bq, sa_sc, samax_sc)
        accumulate(first_diag + jb, sb_sc, sbmax_sc, first_block=jb * part_blocks)

    lam = (jnp.exp(jnp.sum(lq1_ref[...] * lk1_ref[...], keepdims=True))
           - jnp.exp(jnp.sum(lq2_ref[...] * lk2_ref[...], keepdims=True)) + lambda_init)
    accl = acc_sc[...]
    acc = accl[:LANES] / accl[LANES:LANES + 1]
    map0 = jnp.concatenate([acc[:, 2 * part * bk:(2 * part + 1) * bk] for part in range(n_parts)], axis=1)
    map1 = jnp.concatenate([acc[:, (2 * part + 1) * bk:(2 * part + 2) * bk] for part in range(n_parts)], axis=1)
    o = (map0 - lam * map1).T
    y = _rmsnorm_f32(o, sg_ref[...]) * (1.0 - lambda_init)
    o_ref[...] = y.astype(o_ref.dtype)


def _diff_attention(qk, vt, u, lq1, lk1, lq2, lk2, subln, lambda_init, batch, seq, *, bq, bk):
    nh = N_HEADS_B
    nkt = seq // bk
    nq = seq // bq
    vec = lambda a: a.reshape(1, -1)
    small = pl.BlockSpec((1, HEAD_DIM), lambda b, h, i: (0, 0))
    return pl.pallas_call(
        functools.partial(_diff_attn_kernel, bq=bq, bk=bk, lambda_init=lambda_init),
        grid=(batch, nh, seq // bq),
        in_specs=[small, small, small, small,
                  pl.BlockSpec((None, bq, LANES), lambda b, h, i: (b, i, h)),
                  pl.BlockSpec((None, bq, LANES), lambda b, h, i: (b, jnp.minimum(i + 1, nq - 1), h)),
                  pl.BlockSpec((None, seq, LANES), lambda b, h, i: (b, 0, nh + h)),
                  pl.BlockSpec((nkt, LANES, bk), lambda b, h, i: (b, h, 0)),
                  pl.BlockSpec((DIFF_N_TILES, 2, LANES, LANES), lambda b, h, i: (0, h, 0, 0)),
                  pl.BlockSpec((1, LANES), lambda b, h, i: (0, 0))],
        out_specs=pl.BlockSpec((None, bq, LANES), lambda b, h, i: (b, i, h)),
        out_shape=jax.ShapeDtypeStruct((batch, seq, D_MODEL), BF16),
        scratch_shapes=[pltpu.VMEM((1, 2 * bq), F32),
                        pltpu.VMEM((LANES + SUM_ROWS, 2 * bq), F32),
                        pltpu.VMEM((bk, 2 * bq), F32),
                        pltpu.VMEM((1, 2 * bq), F32),
                        pltpu.VMEM((bk, 2 * bq), F32),
                        pltpu.VMEM((1, 2 * bq), F32)],
        compiler_params=_cparams(("parallel", "parallel", "arbitrary")),
        name="diff_attn",
    )(vec(lq1), vec(lk1), vec(lq2), vec(lk2), qk, qk, qk, vt, u, vec(subln))


def kernel(x, p, rel_bias, a_w_qkv, a_w_o, b_w_qkv, b_w_o, b_lambda_q1, b_lambda_k1, b_lambda_q2, b_lambda_k2, b_subln, norm_mix, norm_mlp, w_ff1, w_ff2, norm_ple, w_ple_gate, w_ple_proj, final_norm):
    batch, seq, d = x.shape
    depth = p.shape[0]
    n = batch * seq
    h = x.reshape(n, d)
    p_all = p.reshape(depth, n, p.shape[-1])
    n_mixers = 2

    for i in range(depth):
        j = i // n_mixers
        if i % n_mixers == 0:
            w_a = a_w_qkv[j].astype(BF16)
            bias = _build_dilated_bias(rel_bias)
            outs, lses = [], []
            for g, (_, dilation) in enumerate(DIL_CONFIGS):
                qkv = _qkv_dilated(h, norm_mix[i], w_a, g, dilation, tm=max(1024, BLOCK * dilation))
                o_g, lse_g = _dilated_attention(qkv, bias, g, dilation, batch, seq)
                outs.append(o_g)
                lses.append(lse_g)
            h = _combine_proj(h, outs, lses, a_w_o[j].astype(BF16))
            proj = None
        else:
            lambda_init = 0.8 - 0.6 * math.exp(-0.3 * i)
            bq, bk = 2048, 512
            w_b = b_w_qkv[j].astype(BF16)
            qk, vt = _qkv_diff(h, norm_mix[i], w_b, bk=bk)
            u = _build_diff_bias(rel_bias)
            o = _diff_attention(qk.reshape(batch, seq, 2 * d), vt, u,
                                b_lambda_q1[j], b_lambda_k1[j], b_lambda_q2[j], b_lambda_k2[j],
                                b_subln[j], lambda_init, batch, seq, bq=bq, bk=bk)
            proj = (o.reshape(n, d), b_w_o[j].astype(BF16))
        h = _mlp_ple(h, p_all, i, norm_mlp[i], w_ff1[i].astype(BF16), w_ff2[i].astype(BF16),
                     norm_ple[i], w_ple_gate[i].astype(BF16), w_ple_proj[i].astype(BF16),
                     final_norm, final_norm=(i == depth - 1), proj=proj)
    return h.reshape(batch, seq, d)
```

```python
import functools
import math

import jax
import jax.numpy as jnp
from jax import lax
from jax.experimental import pallas as pl
from jax.experimental.pallas import tpu as pltpu

F32 = jnp.float32
BF16 = jnp.bfloat16

D_MODEL = 1024
HEAD_DIM = 64
BLOCK = 128
QKV_CHUNK = 512
MLP_CHUNK = 512
DIL_RING = 6
DIL_QB = 8
DIL_CONFIGS = ((128, 1), (512, 4), (2048, 16))
N_GROUPS = len(DIL_CONFIGS)
N_HEADS_A = D_MODEL // HEAD_DIM
N_HEADS_B = D_MODEL // (2 * HEAD_DIM)
N_BUCKETS = 32
MAX_DISTANCE = 2048
N_BIAS_COLS = 16
EPS = 1e-6
NEG = -1e30
LANES = 128
SUBLANES = 8
LOG2_LANES = 7
QK_SCALE = HEAD_DIM ** -0.5
LOG2E = math.log2(math.e)
LN2 = math.log(2.0)

DIFF_CONST_TILE = (MAX_DISTANCE + LANES - 1) // LANES + 1
DIFF_MASK_TILE = DIFF_CONST_TILE + 1
DIFF_N_TILES = DIFF_MASK_TILE + 1
SUM_ROWS = 16

VMEM_LIMIT = 48 * 1024 * 1024
VMEM_LIMIT_PROJ = 56 * 1024 * 1024


def _cparams(sem, vmem_limit=VMEM_LIMIT):
    return pltpu.CompilerParams(dimension_semantics=sem, vmem_limit_bytes=vmem_limit)


def _rmsnorm_f32(x, g):
    ms = jnp.mean(x * x, axis=-1, keepdims=True)
    return x * lax.rsqrt(ms + EPS) * g


def _rel_bucket(dist):
    n = jnp.maximum(dist, 0)
    max_exact = N_BUCKETS // 2
    nf = jnp.maximum(n, 1).astype(F32)
    large = max_exact + (jnp.log(nf / max_exact) / math.log(MAX_DISTANCE / max_exact)
                         * (N_BUCKETS - max_exact)).astype(jnp.int32)
    large = jnp.minimum(large, N_BUCKETS - 1)
    return jnp.where(n < max_exact, n, large)


def _table_lookup(bucket, tab_ref, col):
    acc = jnp.zeros(bucket.shape, F32)
    for b in range(N_BUCKETS):
        acc = jnp.where(bucket == b, tab_ref[b, col], acc)
    return acc


def _dilated_bias_kernel(tab_ref, o_ref):
    g = pl.program_id(0)
    dilation = jnp.where(g == 0, DIL_CONFIGS[0][1],
                         jnp.where(g == 1, DIL_CONFIGS[1][1], DIL_CONFIGS[2][1]))
    kj = lax.broadcasted_iota(jnp.int32, (2 * BLOCK, BLOCK), 0)
    qi = lax.broadcasted_iota(jnp.int32, (2 * BLOCK, BLOCK), 1)
    sub = qi + BLOCK - kj
    band = (sub >= 0) & (sub <= BLOCK)
    band_first = band & (kj >= BLOCK)
    bucket = _rel_bucket(sub * dilation)
    for c in range(N_BIAS_COLS):
        lanes = slice((c % 2) * BLOCK, (c % 2 + 1) * BLOCK)
        bias = _table_lookup(bucket, tab_ref, c) * LOG2E
        o_ref[0, c // 2, :, lanes] = jnp.where(band, bias, NEG)
        o_ref[1, c // 2, :, lanes] = jnp.where(band_first, bias, NEG)


def _build_dilated_bias(rel_bias):
    return pl.pallas_call(
        _dilated_bias_kernel,
        grid=(N_GROUPS,),
        in_specs=[pl.BlockSpec(memory_space=pltpu.SMEM)],
        out_specs=pl.BlockSpec((None, 2, N_HEADS_A // 2, 2 * BLOCK, 2 * BLOCK),
                               lambda g: (g, 0, 0, 0, 0)),
        out_shape=jax.ShapeDtypeStruct((N_GROUPS, 2, N_HEADS_A // 2, 2 * BLOCK, 2 * BLOCK), F32),
        compiler_params=_cparams(("arbitrary",)),
        name="dilated_bias",
    )(rel_bias)


def _diff_bias_kernel(tab_ref, o_ref):
    t = pl.program_id(0)
    kj = lax.broadcasted_iota(jnp.int32, (LANES, LANES), 0)
    qi = lax.broadcasted_iota(jnp.int32, (LANES, LANES), 1)
    dist = t * LANES + qi - kj
    masked = (dist < 0) | (t == DIFF_MASK_TILE)
    bucket = _rel_bucket(dist)
    init = jnp.where(masked, NEG, 0.0)
    for c in range(N_BIAS_COLS):
        o_ref[c] = init
    live = jnp.where(masked, -1, bucket)

    def body(b, carry):
        hit = live == b
        for c in range(N_BIAS_COLS):
            o_ref[c] = jnp.where(hit, tab_ref[b, c] * LOG2E, o_ref[c])
        return carry

    lax.fori_loop(jnp.min(bucket), jnp.max(bucket) + 1, body, 0)


def _build_diff_bias(rel_bias):
    return pl.pallas_call(
        _diff_bias_kernel,
        grid=(DIFF_N_TILES,),
        in_specs=[pl.BlockSpec(memory_space=pltpu.SMEM)],
        out_specs=pl.BlockSpec((None, N_BIAS_COLS, LANES, LANES), lambda t: (t, 0, 0, 0)),
        out_shape=jax.ShapeDtypeStruct((DIFF_N_TILES, N_BIAS_COLS, LANES, LANES), F32),
        compiler_params=_cparams(("arbitrary",)),
        name="diff_bias",
    )(rel_bias)


def _qkv_dilated_kernel(*refs, dilation, n_slabs, chunk, whole, pre_stride):
    n_x = n_slabs or 1
    x_refs = refs[:n_x]
    g_ref, w_ref, o_ref, xn_sc = refs[n_x:n_x + 4]
    tm, d_model = xn_sc.shape
    rows = tm // dilation
    if pre_stride:
        pre_sc = refs[n_x + 4]
        pre_rows = tm // pre_stride
        outer = dilation // pre_stride

    def pre_regroup():
        for s, x_ref in enumerate(x_refs):
            for b in range(pre_stride):
                pre_sc[s, b * pre_rows:(b + 1) * pre_rows, :] = x_ref[pl.ds(b, pre_rows, stride=pre_stride), :]

    def normalise(r, a, b):
        if n_slabs is None:
            xs = [x_refs[0][a:b, :]]
        elif pre_stride:
            start = (r % pre_stride) * pre_rows + r // pre_stride + (a - r * rows) * outer
            xs = [pre_sc[s, pl.ds(start, b - a, stride=outer), :] for s in range(n_slabs)]
        else:
            xs = [x_ref[pl.ds(r + (a - r * rows) * dilation, b - a, stride=dilation), :]
                  for x_ref in x_refs]
        slab_w = d_model // len(xs)
        sq = xs[0] * xs[0]
        for x in xs[1:]:
            sq = sq + x * x
        scale = lax.rsqrt(jnp.sum(sq, axis=-1, keepdims=True) * (1.0 / d_model) + EPS)
        for s, x in enumerate(xs):
            cols = slice(s * slab_w, (s + 1) * slab_w)
            xn_sc[a:b, cols] = (x * scale * g_ref[:, cols]).astype(BF16)

    def store(c, y, lo, r, a, b, col0):
        if c < 2:
            val = y[a - lo:b - lo]
            if c == 0:
                val = val * (QK_SCALE * LOG2E)
            o_ref[a - r * rows:b - r * rows, col0:col0 + d_model] = val.astype(o_ref.dtype)
            return
        for blk in range((b - a) // BLOCK):
            y0 = a - lo + blk * BLOCK
            s0 = a - r * rows + blk * BLOCK
            for hp in range(d_model // LANES):
                tile = y[y0:y0 + BLOCK, hp * LANES:(hp + 1) * LANES]
                o_ref[s0:s0 + BLOCK, col0 + hp * LANES:col0 + (hp + 1) * LANES] = (
                    tile.T.astype(o_ref.dtype))

    def run(comps, with_norm):
        if with_norm and pre_stride:
            pre_regroup()
        for ch in range(tm // chunk):
            lo, hi = ch * chunk, (ch + 1) * chunk
            pieces = [(r, max(lo, r * rows), min(hi, (r + 1) * rows)) for r in range(dilation)
                      if max(lo, r * rows) < min(hi, (r + 1) * rows)]
            if with_norm:
                for r, a, b in pieces:
                    normalise(r, a, b)
            xn = xn_sc[lo:hi, :]
            for c in comps:
                w = w_ref[:, c * d_model:(c + 1) * d_model] if whole else w_ref[...]
                y = jnp.dot(xn, w, preferred_element_type=F32)
                for r, a, b in pieces:
                    store(c, y, lo, r, a, b, ((c * dilation if whole else 0) + r) * d_model)

    if whole:
        run((0, 1, 2), True)
    else:
        j = pl.program_id(1)
        for c in range(3):
            pl.when(j == c)(functools.partial(run, (c,), c == 0))


def _qkv_dilated(x, g, w, group, dilation, *, tm):
    n, d = x.shape
    assert w.shape[0] == d and w.shape[1] % (3 * d) == 0
    rows = tm // dilation
    assert tm % dilation == 0 and rows % BLOCK == 0
    if dilation == 1:
        n_slabs = None
        x_in = [x]
        x_specs = [pl.BlockSpec((tm, d), lambda i, j: (i, 0))]
    else:
        n_slabs = d // LANES
        x_in = [x] * n_slabs
        x_specs = [pl.BlockSpec((tm, LANES), functools.partial(lambda i, j, s: (i, s), s=s))
                   for s in range(n_slabs)]
    pre_stride = 4 if dilation % 8 == 0 else None
    scratch = [pltpu.VMEM((tm, d), BF16)]
    if pre_stride:
        scratch.append(pltpu.VMEM((n_slabs, tm, LANES), F32))
    out_shape = jax.ShapeDtypeStruct((n // dilation, 3 * dilation * d), BF16)
    whole_bytes = 2 * (tm * 3 * d * 2) + 2 * (d * 3 * d * 2) + 2 * (tm * d * 4) + 4 * QKV_CHUNK * d * 4
    assert tm % QKV_CHUNK == 0 and (QKV_CHUNK % rows == 0 or rows % QKV_CHUNK == 0)
    if whole_bytes <= VMEM_LIMIT:
        return pl.pallas_call(
            functools.partial(_qkv_dilated_kernel, dilation=dilation, n_slabs=n_slabs,
                              chunk=QKV_CHUNK, whole=True, pre_stride=pre_stride),
            grid=(n // tm, 1),
            in_specs=x_specs + [pl.BlockSpec((1, d), lambda i, j: (0, 0)),
                                pl.BlockSpec((d, 3 * d), lambda i, j: (0, group))],
            out_specs=pl.BlockSpec((rows, 3 * dilation * d), lambda i, j: (i, 0)),
            out_shape=out_shape,
            scratch_shapes=scratch,
            compiler_params=_cparams(("parallel", "arbitrary")),
            name=f"qkv_d{dilation}",
        )(*x_in, g.reshape(1, d), w)
    return pl.pallas_call(
        functools.partial(_qkv_dilated_kernel, dilation=dilation, n_slabs=n_slabs,
                          chunk=QKV_CHUNK, whole=False, pre_stride=pre_stride),
        grid=(n // tm, 3),
        in_specs=x_specs + [pl.BlockSpec((1, d), lambda i, j: (0, 0)),
                            pl.BlockSpec((d, d), lambda i, j: (0, 3 * group + j))],
        out_specs=pl.BlockSpec((rows, dilation * d), lambda i, j: (i, j)),
        out_shape=out_shape,
        scratch_shapes=scratch,
        compiler_params=_cparams(("parallel", "arbitrary")),
        name=f"qkv_d{dilation}",
    )(*x_in, g.reshape(1, d), w)


def _qkv_diff_kernel(x_ref, g_ref, wqk_ref, wvt_ref, qk_ref, vt_ref, *, bk):
    tm, d = x_ref.shape
    nt = (((1,), (1,)), ((), ()))
    for c in range(tm // bk):
        rows = slice(c * bk, (c + 1) * bk)
        xn = _rmsnorm_f32(x_ref[rows, :], g_ref[...]).astype(BF16)
        q = jnp.dot(xn, wqk_ref[:, :d], preferred_element_type=F32) * (QK_SCALE * LOG2E)
        qk_ref[rows, :d] = q.astype(qk_ref.dtype)
        qk_ref[rows, d:] = jnp.dot(xn, wqk_ref[:, d:], preferred_element_type=F32).astype(qk_ref.dtype)
        vt_ref[c] = lax.dot_general(wvt_ref[...], xn, nt,
                                    preferred_element_type=F32).astype(vt_ref.dtype)


def _qkv_diff(x, g, w_qkv, *, bk, tm=1024):
    n, d = x.shape
    return pl.pallas_call(
        functools.partial(_qkv_diff_kernel, bk=bk),
        grid=(n // tm,),
        in_specs=[pl.BlockSpec((tm, d), lambda i: (i, 0)),
                  pl.BlockSpec((1, d), lambda i: (0, 0)),
                  pl.BlockSpec((d, 2 * d), lambda i: (0, 0)),
                  pl.BlockSpec((d, d), lambda i: (0, 0))],
        out_specs=[pl.BlockSpec((tm, 2 * d), lambda i: (i, 0)),
                   pl.BlockSpec((tm // bk, d, bk), lambda i: (i, 0, 0))],
        out_shape=[jax.ShapeDtypeStruct((n, 2 * d), BF16),
                   jax.ShapeDtypeStruct((n // bk, d, bk), BF16)],
        compiler_params=_cparams(("parallel",)),
        name="qkv_diff",
    )(x, g.reshape(1, d), w_qkv, w_qkv[:, 2 * d:].T)


def _dilated_kernel(q_ref, kp_ref, kc_ref, vtp_ref, vtc_ref, bias_ref, o_ref, lse_ref,
                    s_sc, m_sc, pv_sc):
    first_step = (pl.program_id(2) == 0).astype(jnp.int32)
    lane = lax.broadcasted_iota(jnp.int32, (BLOCK, LANES), 1)
    row = lax.broadcasted_iota(jnp.int32, (BLOCK, LANES), 0)
    head_row = lax.broadcasted_iota(jnp.int32, (N_HEADS_A, BLOCK), 0)
    lo_row = row < HEAD_DIM
    mask_lo = jnp.where(lane < HEAD_DIM, 1.0, 0.0).astype(BF16)
    mask_hi = jnp.where(lane < HEAD_DIM, 0.0, 1.0).astype(BF16)
    nt = (((1,), (1,)), ((), ()))
    ones = jnp.ones((SUM_ROWS, 2 * BLOCK), BF16)
    n_pairs = N_HEADS_A // 2
    n_qb = q_ref.shape[0] // BLOCK
    items = [(qb, hp) for qb in range(n_qb) for hp in range(n_pairs)]

    def rows(qb):
        return slice(qb * BLOCK, (qb + 1) * BLOCK)

    def prev_cur(prev_ref, cur_ref, qb, sl):
        prev = prev_ref[:, sl] if qb == 0 else cur_ref[rows(qb - 1), sl]
        return prev, cur_ref[rows(qb), sl]

    def scores(it):
        qb, hp = items[it]
        sl = slice(hp * LANES, (hp + 1) * LANES)
        q = q_ref[rows(qb), sl]
        q2 = jnp.concatenate([q * mask_lo, q * mask_hi], axis=0)
        k = jnp.concatenate(prev_cur(kp_ref, kc_ref, qb, sl), axis=0)
        s = lax.dot_general(k, q2, nt, preferred_element_type=F32)
        s = s + bias_ref[first_step if qb == 0 else 0, hp]
        s_sc[it % DIL_RING] = s
        m_sc[it] = jnp.max(s, axis=0, keepdims=True)

    def value_product(it):
        qb, hp = items[it]
        sl = slice(hp * LANES, (hp + 1) * LANES)
        p = jnp.exp2(s_sc[it % DIL_RING] - m_sc[it]).astype(BF16)
        vt = jnp.concatenate(prev_cur(vtp_ref, vtc_ref, qb, sl), axis=1)
        pv_sc[it] = jnp.dot(jnp.concatenate([vt, ones], axis=0), p,
                            preferred_element_type=F32)

    for it in range(len(items) + DIL_RING - 1):
        if it < len(items):
            scores(it)
        if it >= DIL_RING - 1:
            value_product(it - (DIL_RING - 1))
    for qb in range(n_qb):
        lse_t = jnp.zeros((N_HEADS_A, BLOCK), F32)
        for hp in range(n_pairs):
            it = qb * n_pairs + hp
            l = pv_sc[it, LANES:LANES + 1]
            acc = pv_sc[it, :LANES] / l
            o_t = jnp.where(lo_row, acc[:, :BLOCK], acc[:, BLOCK:])
            o_ref[rows(qb), hp * LANES:(hp + 1) * LANES] = o_t.T.astype(o_ref.dtype)
            lse = (m_sc[it] + jnp.log2(l)) * LN2
            lse_t = jnp.where(head_row == 2 * hp, lse[:, :BLOCK], lse_t)
            lse_t = jnp.where(head_row == 2 * hp + 1, lse[:, BLOCK:], lse_t)
        lse_full = jnp.concatenate([lse_t, jnp.zeros((BLOCK - N_HEADS_A, BLOCK), F32)], axis=0)
        lse_ref[rows(qb), :] = lse_full.T


def _dilated_attention(qkv, bias, group, dilation, batch, seq):
    assert seq % (dilation * BLOCK) == 0
    sub_len = seq // dilation
    n_qb = min(DIL_QB, sub_len // BLOCK)
    assert sub_len % (BLOCK * n_qb) == 0
    steps = sub_len // (BLOCK * n_qb)
    qkv_v = qkv.reshape(batch, sub_len, 3 * dilation * D_MODEL)

    def col(c):
        return lambda b, r, n: (b, n, c * dilation + r)

    def col_prev(c):
        return lambda b, r, n: (b, jnp.maximum(n * n_qb - 1, 0), c * dilation + r)

    blk = (None, n_qb * BLOCK, D_MODEL)
    blk_prev = (None, BLOCK, D_MODEL)
    n_items = n_qb * N_HEADS_A // 2
    o, lse = pl.pallas_call(
        _dilated_kernel,
        grid=(batch, dilation, steps),
        in_specs=[pl.BlockSpec(blk, col(0)),
                  pl.BlockSpec(blk_prev, col_prev(1)),
                  pl.BlockSpec(blk, col(1)),
                  pl.BlockSpec(blk_prev, col_prev(2)),
                  pl.BlockSpec(blk, col(2)),
                  pl.BlockSpec((None, 2, N_HEADS_A // 2, 2 * BLOCK, 2 * BLOCK),
                               lambda b, r, n: (group, 0, 0, 0, 0))],
        out_specs=[pl.BlockSpec(blk, lambda b, r, n: (b, n, r)),
                   pl.BlockSpec((None, n_qb * BLOCK, LANES), lambda b, r, n: (b, n, r))],
        out_shape=[jax.ShapeDtypeStruct((batch, sub_len, dilation * D_MODEL), BF16),
                   jax.ShapeDtypeStruct((batch, sub_len, dilation * LANES), F32)],
        scratch_shapes=[pltpu.VMEM((DIL_RING, 2 * BLOCK, 2 * BLOCK), F32),
                        pltpu.VMEM((n_items, 1, 2 * BLOCK), F32),
                        pltpu.VMEM((n_items, LANES + SUM_ROWS, 2 * BLOCK), F32)],
        compiler_params=_cparams(("parallel", "parallel", "arbitrary")),
        name=f"dilated_attn_d{dilation}",
    )(qkv_v, qkv_v, qkv_v, qkv_v, qkv_v, bias)
    return (o.reshape(batch * sub_len, dilation * D_MODEL),
            lse.reshape(batch * sub_len, dilation * LANES))


def _combine_proj_kernel(h_ref, o0_ref, o1_ref, o2_ref, l0_ref, l1_ref, l2_ref,
                         e_ref, w_ref, out_ref, lse_sc, o_sc):
    o_refs = [o0_ref, o1_ref, o2_ref]
    l_refs = [l0_ref, l1_ref, l2_ref]
    tm, d = h_ref.shape
    n_slabs = d // LANES
    for g, (_, dilation) in enumerate(DIL_CONFIGS):
        rows = tm // dilation
        for r in range(dilation):
            dst = pl.ds(r, rows, stride=dilation) if dilation > 1 else slice(None)
            lse_sc[g, dst, :] = l_refs[g][:, r * LANES:(r + 1) * LANES]
            for c in range(n_slabs):
                o_sc[g, c, dst, :] = o_refs[g][:, r * d + c * LANES:r * d + (c + 1) * LANES].astype(F32)
    lses = [lse_sc[g] for g in range(N_GROUPS)]
    mx = jnp.maximum(jnp.maximum(lses[0], lses[1]), lses[2])
    ws = [jnp.exp(l - mx) for l in lses]
    tot = ws[0] + ws[1] + ws[2]
    head_lane = lax.broadcasted_iota(jnp.int32, (tm, LANES), 1) < N_HEADS_A
    packed = None
    for g in range(N_GROUPS):
        a = jnp.where(head_lane, ws[g] / tot, 0.0)
        a_hi = a.astype(BF16).astype(F32)
        for part, piece in enumerate((a_hi, a - a_hi)):
            shift = (part * N_GROUPS + g) * N_HEADS_A
            moved = piece if shift == 0 else pltpu.roll(piece, shift, axis=1)
            packed = moved if packed is None else packed + moved
    ae_all = jnp.dot(packed.astype(BF16), e_ref[...], preferred_element_type=F32)
    aes = [ae_all[:, g * d:(g + 1) * d] for g in range(N_GROUPS)]
    slabs = []
    for c in range(n_slabs):
        cols = slice(c * LANES, (c + 1) * LANES)
        slabs.append(aes[0][:, cols] * o_sc[0, c] + aes[1][:, cols] * o_sc[1, c]
                     + aes[2][:, cols] * o_sc[2, c])
    o = jnp.concatenate(slabs, axis=1).astype(BF16)
    out_ref[...] = h_ref[...] + jnp.dot(o, w_ref[...], preferred_element_type=F32)


def _combine_proj(h, outs, lses, w, *, tm=512):
    n, d = h.shape
    rows = jnp.arange(LANES, dtype=jnp.int32)
    cols = jnp.arange(N_GROUPS * d, dtype=jnp.int32)
    row_group = (rows // N_HEADS_A) % N_GROUPS
    row_head = rows % N_HEADS_A
    row_used = rows < 2 * N_GROUPS * N_HEADS_A
    expand = (row_used[:, None] & (row_group[:, None] == (cols // d)[None, :])
              & (row_head[:, None] == ((cols % d) // HEAD_DIM)[None, :])).astype(BF16)
    row = lambda i: (i, 0)
    full = lambda i: (0, 0)
    dils = [dilation for _, dilation in DIL_CONFIGS]
    assert all(tm % (16 * dilation) == 0 for dilation in dils)
    return pl.pallas_call(
        _combine_proj_kernel,
        grid=(n // tm,),
        in_specs=[pl.BlockSpec((tm, d), row)]
                 + [pl.BlockSpec((tm // dilation, dilation * d), row) for dilation in dils]
                 + [pl.BlockSpec((tm // dilation, dilation * LANES), row) for dilation in dils]
                 + [pl.BlockSpec((LANES, N_GROUPS * d), full), pl.BlockSpec((d, d), full)],
        out_specs=pl.BlockSpec((tm, d), row),
        out_shape=jax.ShapeDtypeStruct((n, d), F32),
        scratch_shapes=[pltpu.VMEM((N_GROUPS, tm, LANES), F32),
                        pltpu.VMEM((N_GROUPS, d // LANES, tm, LANES), F32)],
        compiler_params=_cparams(("parallel",)),
        name="combine_proj",
    )(h, *outs, *lses, expand, w)


def _mlp_ple_kernel(*refs, final_norm, with_proj, n_steps, chunk):
    if with_proj:
        h_ref, o_ref, wo_ref = refs[:3]
        refs = refs[3:]
    else:
        h_ref = refs[0]
        refs = refs[1:]
    p_ref, g_ref, w1_ref, w2_ref, gp_ref, wg_ref, wp_ref, fg_ref, out_ref, xn_ref, acc_ref = refs
    f = pl.program_id(1)
    tm = h_ref.shape[0]

    def step(first, last):
        for ch in range(tm // chunk):
            rows = slice(ch * chunk, (ch + 1) * chunk)
            if first:
                if with_proj:
                    h0 = h_ref[rows, :] + jnp.dot(o_ref[rows, :], wo_ref[...],
                                                  preferred_element_type=F32)
                    out_ref[rows, :] = h0
                else:
                    h0 = h_ref[rows, :]
                xn_ref[rows, :] = _rmsnorm_f32(h0, g_ref[...]).astype(BF16)
            a = jnp.dot(xn_ref[rows, :], w1_ref[...], preferred_element_type=F32)
            a = jnp.maximum(a, 0.0)
            a = (a * a).astype(BF16)
            acc = jnp.dot(a, w2_ref[...], preferred_element_type=F32)
            if not first:
                acc = acc_ref[rows, :] + acc
            if not last:
                acc_ref[rows, :] = acc
                continue
            x = (out_ref[rows, :] if with_proj else h_ref[rows, :]) + acc
            xn = _rmsnorm_f32(x, gp_ref[...]).astype(BF16)
            gate = jax.nn.sigmoid(jnp.dot(xn, wg_ref[...], preferred_element_type=F32))
            proj = jnp.dot(p_ref[rows, :].astype(BF16), wp_ref[...], preferred_element_type=F32)
            y = x + gate * proj
            if final_norm:
                y = _rmsnorm_f32(y, fg_ref[...])
            out_ref[rows, :] = y

    pl.when(f == 0)(functools.partial(step, True, n_steps == 1))
    if n_steps > 2:
        pl.when((f > 0) & (f < n_steps - 1))(functools.partial(step, False, False))
    if n_steps > 1:
        pl.when(f == n_steps - 1)(functools.partial(step, False, True))


def _mlp_ple(h, p_all, layer, g, w1, w2, gp, wg, wp, fg, *, final_norm, proj=None, tm=1024, tf=1024):
    n, d = h.shape
    dff = w1.shape[1]
    pd = p_all.shape[-1]
    vec = lambda i, f: (0, 0)
    row = lambda i, f: (i, 0)
    lead_specs = [pl.BlockSpec((tm, d), row)]
    lead_args = [h]
    if proj is not None:
        o, wo = proj
        lead_specs += [pl.BlockSpec((tm, o.shape[1]), row), pl.BlockSpec(wo.shape, vec)]
        lead_args += [o, wo]
    return pl.pallas_call(
        functools.partial(_mlp_ple_kernel, final_norm=final_norm, with_proj=proj is not None,
                          n_steps=dff // tf, chunk=MLP_CHUNK),
        grid=(n // tm, dff // tf),
        in_specs=lead_specs + [
                  pl.BlockSpec((None, tm, pd), lambda i, f: (layer, i, 0)),
                  pl.BlockSpec((1, d), vec),
                  pl.BlockSpec((d, tf), lambda i, f: (0, f)),
                  pl.BlockSpec((tf, d), lambda i, f: (f, 0)),
                  pl.BlockSpec((1, d), vec),
                  pl.BlockSpec((d, d), vec),
                  pl.BlockSpec((pd, d), vec),
                  pl.BlockSpec((1, d), vec)],
        out_specs=pl.BlockSpec((tm, d), lambda i, f: (i, 0)),
        out_shape=jax.ShapeDtypeStruct((n, d), F32),
        scratch_shapes=[pltpu.VMEM((tm, d), BF16), pltpu.VMEM((tm, d), F32)],
        compiler_params=_cparams(("parallel", "arbitrary"),
                                 vmem_limit=VMEM_LIMIT_PROJ if proj is not None else VMEM_LIMIT),
        name="mlp_ple",
    )(*lead_args, p_all, g.reshape(1, d), w1, w2, gp.reshape(1, d), wg, wp, fg.reshape(1, d))


def _diff_attn_kernel(lq1_ref, lk1_ref, lq2_ref, lk2_ref, q_ref, qn_ref, k_ref, vt_ref, u_ref, sg_ref,
                      o_ref, m_sc, acc_sc, sa_sc, samax_sc, sb_sc, sbmax_sc, *, bq, bk, lambda_init):
    n_parts = bq // bk
    assert bq == n_parts * bk and n_parts % 2 == 0
    i_q = pl.program_id(2)
    q0 = i_q * bq
    nt = (((1,), (1,)), ((), ()))
    lane = lax.broadcasted_iota(jnp.int32, (bq, LANES), 1)

    def query_columns(ref):
        qf = ref[...].astype(F32)
        a = jnp.where(lane < HEAD_DIM, qf, 0.0)
        b = jnp.where(lane >= HEAD_DIM, qf, 0.0)
        pieces = []
        for part in range(n_parts):
            pieces += [a[part * bk:(part + 1) * bk], b[part * bk:(part + 1) * bk]]
        return jnp.concatenate(pieces, axis=0).astype(BF16)

    q2 = query_columns(q_ref)
    q2_next = query_columns(qn_ref)

    m_sc[...] = jnp.full(m_sc.shape, NEG, F32)
    acc_sc[...] = jnp.zeros(acc_sc.shape, F32)

    per_map = bk // LANES
    part_blocks = 2 * per_map
    n_blocks = n_parts * part_blocks
    first_diag = q0 // bk
    ones = jnp.ones((SUM_ROWS, bk), BF16)

    def scores(c, q2x, q0x, s_ref, smax_ref, first_block=0):
        k0 = pl.multiple_of(c * bk, bk)
        lanes = slice(first_block * LANES, n_blocks * LANES)
        s = lax.dot_general(k_ref[pl.ds(k0, bk), :], q2x[lanes], nt,
                            preferred_element_type=F32)
        rows = []
        for jb in range(bk // LANES):
            tiles = []
            for ib in range(first_block, n_blocks):
                part, within = divmod(ib, part_blocks)
                col, blk = divmod(within, per_map)
                i0 = part * bk + blk * LANES
                t = lax.shift_right_arithmetic(q0x + i0 - k0 - jb * LANES, LOG2_LANES)
                t = jnp.where(t < 0, DIFF_MASK_TILE, jnp.minimum(t, DIFF_CONST_TILE))
                tiles.append(u_ref[t, col])
            rows.append(jnp.concatenate(tiles, axis=1))
        s = s + jnp.concatenate(rows, axis=0)
        s_ref[:, lanes] = s
        smax_ref[:, lanes] = jnp.max(s, axis=0, keepdims=True)

    def accumulate(c, s_ref, smax_ref, first_block=0):
        lanes = slice(first_block * LANES, n_blocks * LANES)
        m_prev = m_sc[:, lanes]
        m_new = jnp.maximum(m_prev, smax_ref[:, lanes])
        m_sc[:, lanes] = m_new
        p = jnp.exp2(s_ref[:, lanes] - m_new).astype(BF16)
        v1 = jnp.concatenate([vt_ref[c], ones], axis=0)
        acc_sc[:, lanes] = (jnp.exp2(m_prev - m_new) * acc_sc[:, lanes]
                            + jnp.dot(v1, p, preferred_element_type=F32))

    @pl.when(i_q == 0)
    def _():
        scores(0, q2, q0, sa_sc, samax_sc)

    def body(i, carry):
        scores(2 * i + 1, q2, q0, sb_sc, sbmax_sc)
        accumulate(2 * i, sa_sc, samax_sc)
        scores(2 * i + 2, q2, q0, sa_sc, samax_sc)
        accumulate(2 * i + 1, sb_sc, sbmax_sc)
        return carry

    lax.fori_loop(0, first_diag // 2, body, 0)

    for dt in range(n_parts // 2):
        ja, jb = 2 * dt, 2 * dt + 1
        scores(first_diag + jb, q2, q0, sb_sc, sbmax_sc, first_block=jb * part_blocks)
        accumulate(first_diag + ja, sa_sc, samax_sc, first_block=ja * part_blocks)
        if jb + 1 < n_parts:
            scores(first_diag + jb + 1, q2, q0, sa_sc, samax_sc, first_block=(jb + 1) * part_blocks)
        else:
            scores(0, q2_next, q0 + bq, sa_sc, samax_sc)
        accumulate(first_diag + jb, sb_sc, sbmax_sc, first_block=jb * part_blocks)

    lam = (jnp.exp(jnp.sum(lq1_ref[...] * lk1_ref[...], keepdims=True))
           - jnp.exp(jnp.sum(lq2_ref[...] * lk2_ref[...], keepdims=True)) + lambda_init)
    accl = acc_sc[...]
    acc = accl[:LANES] / accl[LANES:LANES + 1]
    map0 = jnp.concatenate([acc[:, 2 * part * bk:(2 * part + 1) * bk] for part in range(n_parts)], axis=1)
    map1 = jnp.concatenate([acc[:, (2 * part + 1) * bk:(2 * part + 2) * bk] for part in range(n_parts)], axis=1)
    o = (map0 - lam * map1).T
    y = _rmsnorm_f32(o, sg_ref[...]) * (1.0 - lambda_init)
    o_ref[...] = y.astype(o_ref.dtype)


def _diff_attention(qk, vt, u, lq1, lk1, lq2, lk2, subln, lambda_init, batch, seq, *, bq, bk):
    nh = N_HEADS_B
    nkt = seq // bk
    nq = seq // bq
    vec = lambda a: a.reshape(1, -1)
    small = pl.BlockSpec((1, HEAD_DIM), lambda b, h, i: (0, 0))
    return pl.pallas_call(
        functools.partial(_diff_attn_kernel, bq=bq, bk=bk, lambda_init=lambda_init),
        grid=(batch, nh, seq // bq),
        in_specs=[small, small, small, small,
                  pl.BlockSpec((None, bq, LANES), lambda b, h, i: (b, i, h)),
                  pl.BlockSpec((None, bq, LANES), lambda b, h, i: (b, jnp.minimum(i + 1, nq - 1), h)),
                  pl.BlockSpec((None, seq, LANES), lambda b, h, i: (b, 0, nh + h)),
                  pl.BlockSpec((nkt, LANES, bk), lambda b, h, i: (b, h, 0)),
                  pl.BlockSpec((DIFF_N_TILES, 2, LANES, LANES), lambda b, h, i: (0, h, 0, 0)),
                  pl.BlockSpec((1, LANES), lambda b, h, i: (0, 0))],
        out_specs=pl.BlockSpec((None, bq, LANES), lambda b, h, i: (b, i, h)),
        out_shape=jax.ShapeDtypeStruct((batch, seq, D_MODEL), BF16),
        scratch_shapes=[pltpu.VMEM((1, 2 * bq), F32),
                        pltpu.VMEM((LANES + SUM_ROWS, 2 * bq), F32),
                        pltpu.VMEM((bk, 2 * bq), F32),
                        pltpu.VMEM((1, 2 * bq), F32),
                        pltpu.VMEM((bk, 2 * bq), F32),
                        pltpu.VMEM((1, 2 * bq), F32)],
        compiler_params=_cparams(("parallel", "parallel", "arbitrary")),
        name="diff_attn",
    )(vec(lq1), vec(lk1), vec(lq2), vec(lk2), qk, qk, qk, vt, u, vec(subln))


def kernel(x, p, rel_bias, a_w_qkv, a_w_o, b_w_qkv, b_w_o, b_lambda_q1, b_lambda_k1, b_lambda_q2, b_lambda_k2, b_subln, norm_mix, norm_mlp, w_ff1, w_ff2, norm_ple, w_ple_gate, w_ple_proj, final_norm):
    batch, seq, d = x.shape
    depth = p.shape[0]
    n = batch * seq
    h = x.reshape(n, d)
    p_all = p.reshape(depth, n, p.shape[-1])
    n_mixers = 2

    for i in range(depth):
        j = i // n_mixers
        if i % n_mixers == 0:
            w_a = a_w_qkv[j].astype(BF16)
            bias = _build_dilated_bias(rel_bias)
            outs, lses = [], []
            for g, (_, dilation) in enumerate(DIL_CONFIGS):
                qkv = _qkv_dilated(h, norm_mix[i], w_a, g, dilation, tm=max(1024, BLOCK * dilation))
                o_g, lse_g = _dilated_attention(qkv, bias, g, dilation, batch, seq)
                outs.append(o_g)
                lses.append(lse_g)
            h = _combine_proj(h, outs, lses, a_w_o[j].astype(BF16))
            proj = None
        else:
            lambda_init = 0.8 - 0.6 * math.exp(-0.3 * i)
            bq, bk = 2048, 512
            w_b = b_w_qkv[j].astype(BF16)
            qk, vt = _qkv_diff(h, norm_mix[i], w_b, bk=bk)
            u = _build_diff_bias(rel_bias)
            o = _diff_attention(qk.reshape(batch, seq, 2 * d), vt, u,
                                b_lambda_q1[j], b_lambda_k1[j], b_lambda_q2[j], b_lambda_k2[j],
                                b_subln[j], lambda_init, batch, seq, bq=bq, bk=bk)
            proj = (o.reshape(n, d), b_w_o[j].astype(BF16))
        h = _mlp_ple(h, p_all, i, norm_mlp[i], w_ff1[i].astype(BF16), w_ff2[i].astype(BF16),
                     norm_ple[i], w_ple_gate[i].astype(BF16), w_ple_proj[i].astype(BF16),
                     final_norm, final_norm=(i == depth - 1), proj=proj)
    return h.reshape(batch, seq, d)
```

```python
import functools
import math

import jax
import jax.numpy as jnp
from jax import lax
from jax.experimental import pallas as pl
from jax.experimental.pallas import tpu as pltpu

F32 = jnp.float32
BF16 = jnp.bfloat16

D_MODEL = 1024
HEAD_DIM = 64
BLOCK = 128
QKV_CHUNK = 512
MLP_CHUNK = 512
DIL_RING = 6
DIL_QB = 8
DIL_CONFIGS = ((128, 1), (512, 4), (2048, 16))
N_GROUPS = len(DIL_CONFIGS)
N_HEADS_A = D_MODEL // HEAD_DIM
N_HEADS_B = D_MODEL // (2 * HEAD_DIM)
N_BUCKETS = 32
MAX_DISTANCE = 2048
N_BIAS_COLS = 16
EPS = 1e-6
NEG = -1e30
LANES = 128
SUBLANES = 8
LOG2_LANES = 7
QK_SCALE = HEAD_DIM ** -0.5
LOG2E = math.log2(math.e)
LN2 = math.log(2.0)

DIFF_CONST_TILE = (MAX_DISTANCE + LANES - 1) // LANES + 1
DIFF_MASK_TILE = DIFF_CONST_TILE + 1
DIFF_N_TILES = DIFF_MASK_TILE + 1
SUM_ROWS = 16

VMEM_LIMIT = 48 * 1024 * 1024
VMEM_LIMIT_PROJ = 56 * 1024 * 1024


def _cparams(sem, vmem_limit=VMEM_LIMIT):
    return pltpu.CompilerParams(dimension_semantics=sem, vmem_limit_bytes=vmem_limit)


def _rmsnorm_f32(x, g):
    ms = jnp.mean(x * x, axis=-1, keepdims=True)
    return x * lax.rsqrt(ms + EPS) * g


def _rel_bucket(dist):
    n = jnp.maximum(dist, 0)
    max_exact = N_BUCKETS // 2
    nf = jnp.maximum(n, 1).astype(F32)
    large = max_exact + (jnp.log(nf / max_exact) / math.log(MAX_DISTANCE / max_exact)
                         * (N_BUCKETS - max_exact)).astype(jnp.int32)
    large = jnp.minimum(large, N_BUCKETS - 1)
    return jnp.where(n < max_exact, n, large)


def _table_lookup(bucket, tab_ref, col):
    acc = jnp.zeros(bucket.shape, F32)
    for b in range(N_BUCKETS):
        acc = jnp.where(bucket == b, tab_ref[b, col], acc)
    return acc


def _dilated_bias_kernel(tab_ref, o_ref):
    g = pl.program_id(0)
    dilation = jnp.where(g == 0, DIL_CONFIGS[0][1],
                         jnp.where(g == 1, DIL_CONFIGS[1][1], DIL_CONFIGS[2][1]))
    kj = lax.broadcasted_iota(jnp.int32, (2 * BLOCK, BLOCK), 0)
    qi = lax.broadcasted_iota(jnp.int32, (2 * BLOCK, BLOCK), 1)
    sub = qi + BLOCK - kj
    band = (sub >= 0) & (sub <= BLOCK)
    band_first = band & (kj >= BLOCK)
    bucket = _rel_bucket(sub * dilation)
    for c in range(N_BIAS_COLS):
        lanes = slice((c % 2) * BLOCK, (c % 2 + 1) * BLOCK)
        bias = _table_lookup(bucket, tab_ref, c) * LOG2E
        o_ref[0, c // 2, :, lanes] = jnp.where(band, bias, NEG)
        o_ref[1, c // 2, :, lanes] = jnp.where(band_first, bias, NEG)


def _build_dilated_bias(rel_bias):
    return pl.pallas_call(
        _dilated_bias_kernel,
        grid=(N_GROUPS,),
        in_specs=[pl.BlockSpec(memory_space=pltpu.SMEM)],
        out_specs=pl.BlockSpec((None, 2, N_HEADS_A // 2, 2 * BLOCK, 2 * BLOCK),
                               lambda g: (g, 0, 0, 0, 0)),
        out_shape=jax.ShapeDtypeStruct((N_GROUPS, 2, N_HEADS_A // 2, 2 * BLOCK, 2 * BLOCK), F32),
        compiler_params=_cparams(("arbitrary",)),
        name="dilated_bias",
    )(rel_bias)


def _diff_bias_kernel(tab_ref, o_ref):
    t = pl.program_id(0)
    kj = lax.broadcasted_iota(jnp.int32, (LANES, LANES), 0)
    qi = lax.broadcasted_iota(jnp.int32, (LANES, LANES), 1)
    dist = t * LANES + qi - kj
    masked = (dist < 0) | (t == DIFF_MASK_TILE)
    bucket = _rel_bucket(dist)
    init = jnp.where(masked, NEG, 0.0)
    for c in range(N_BIAS_COLS):
        o_ref[c] = init
    live = jnp.where(masked, -1, bucket)

    def body(b, carry):
        hit = live == b
        for c in range(N_BIAS_COLS):
            o_ref[c] = jnp.where(hit, tab_ref[b, c] * LOG2E, o_ref[c])
        return carry

    lax.fori_loop(jnp.min(bucket), jnp.max(bucket) + 1, body, 0)


def _build_diff_bias(rel_bias):
    return pl.pallas_call(
        _diff_bias_kernel,
        grid=(DIFF_N_TILES,),
        in_specs=[pl.BlockSpec(memory_space=pltpu.SMEM)],
        out_specs=pl.BlockSpec((None, N_BIAS_COLS, LANES, LANES), lambda t: (t, 0, 0, 0)),
        out_shape=jax.ShapeDtypeStruct((DIFF_N_TILES, N_BIAS_COLS, LANES, LANES), F32),
        compiler_params=_cparams(("arbitrary",)),
        name="diff_bias",
    )(rel_bias)


def _qkv_dilated_kernel(*refs, dilation, n_slabs, chunk, whole, pre_stride):
    n_x = n_slabs or 1
    x_refs = refs[:n_x]
    g_ref, w_ref, o_ref, xn_sc = refs[n_x:n_x + 4]
    tm, d_model = xn_sc.shape
    rows = tm // dilation
    if pre_stride:
        pre_sc = refs[n_x + 4]
        pre_rows = tm // pre_stride
        outer = dilation // pre_stride

    def pre_regroup():
        for s, x_ref in enumerate(x_refs):
            for b in range(pre_stride):
                pre_sc[s, b * pre_rows:(b + 1) * pre_rows, :] = x_ref[pl.ds(b, pre_rows, stride=pre_stride), :]

    def normalise(r, a, b):
        if n_slabs is None:
            xs = [x_refs[0][a:b, :]]
        elif pre_stride:
            start = (r % pre_stride) * pre_rows + r // pre_stride + (a - r * rows) * outer
            xs = [pre_sc[s, pl.ds(start, b - a, stride=outer), :] for s in range(n_slabs)]
        else:
            xs = [x_ref[pl.ds(r + (a - r * rows) * dilation, b - a, stride=dilation), :]
                  for x_ref in x_refs]
        slab_w = d_model // len(xs)
        sq = xs[0] * xs[0]
        for x in xs[1:]:
            sq = sq + x * x
        scale = lax.rsqrt(jnp.sum(sq, axis=-1, keepdims=True) * (1.0 / d_model) + EPS)
        for s, x in enumerate(xs):
            cols = slice(s * slab_w, (s + 1) * slab_w)
            xn_sc[a:b, cols] = (x * scale * g_ref[:, cols]).astype(BF16)

    def store(c, y, lo, r, a, b, col0):
        if c < 2:
            val = y[a - lo:b - lo]
            if c == 0:
                val = val * (QK_SCALE * LOG2E)
            o_ref[a - r * rows:b - r * rows, col0:col0 + d_model] = val.astype(o_ref.dtype)
            return
        for blk in range((b - a) // BLOCK):
            y0 = a - lo + blk * BLOCK
            s0 = a - r * rows + blk * BLOCK
            for hp in range(d_model // LANES):
                tile = y[y0:y0 + BLOCK, hp * LANES:(hp + 1) * LANES]
                o_ref[s0:s0 + BLOCK, col0 + hp * LANES:col0 + (hp + 1) * LANES] = (
                    tile.T.astype(o_ref.dtype))

    def run(comps, with_norm):
        if with_norm and pre_stride:
            pre_regroup()
        for ch in range(tm // chunk):
            lo, hi = ch * chunk, (ch + 1) * chunk
            pieces = [(r, max(lo, r * rows), min(hi, (r + 1) * rows)) for r in range(dilation)
                      if max(lo, r * rows) < min(hi, (r + 1) * rows)]
            if with_norm:
                for r, a, b in pieces:
                    normalise(r, a, b)
            xn = xn_sc[lo:hi, :]
            for c in comps:
                w = w_ref[:, c * d_model:(c + 1) * d_model] if whole else w_ref[...]
                y = jnp.dot(xn, w, preferred_element_type=F32)
                for r, a, b in pieces:
                    store(c, y, lo, r, a, b, ((c * dilation if whole else 0) + r) * d_model)

    if whole:
        run((0, 1, 2), True)
    else:
        j = pl.program_id(1)
        for c in range(3):
            pl.when(j == c)(functools.partial(run, (c,), c == 0))


def _qkv_dilated(x, g, w, group, dilation, *, tm):
    n, d = x.shape
    assert w.shape[0] == d and w.shape[1] % (3 * d) == 0
    rows = tm // dilation
    assert tm % dilation == 0 and rows % BLOCK == 0
    if dilation == 1:
        n_slabs = None
        x_in = [x]
        x_specs = [pl.BlockSpec((tm, d), lambda i, j: (i, 0))]
    else:
        n_slabs = d // LANES
        x_in = [x] * n_slabs
        x_specs = [pl.BlockSpec((tm, LANES), functools.partial(lambda i, j, s: (i, s), s=s))
                   for s in range(n_slabs)]
    pre_stride = 4 if dilation % 8 == 0 else None
    scratch = [pltpu.VMEM((tm, d), BF16)]
    if pre_stride:
        scratch.append(pltpu.VMEM((n_slabs, tm, LANES), F32))
    out_shape = jax.ShapeDtypeStruct((n // dilation, 3 * dilation * d), BF16)
    whole_bytes = 2 * (tm * 3 * d * 2) + 2 * (d * 3 * d * 2) + 2 * (tm * d * 4) + 4 * QKV_CHUNK * d * 4
    assert tm % QKV_CHUNK == 0 and (QKV_CHUNK % rows == 0 or rows % QKV_CHUNK == 0)
    if whole_bytes <= VMEM_LIMIT:
        return pl.pallas_call(
            functools.partial(_qkv_dilated_kernel, dilation=dilation, n_slabs=n_slabs,
                              chunk=QKV_CHUNK, whole=True, pre_stride=pre_stride),
            grid=(n // tm, 1),
            in_specs=x_specs + [pl.BlockSpec((1, d), lambda i, j: (0, 0)),
                                pl.BlockSpec((d, 3 * d), lambda i, j: (0, group))],
            out_specs=pl.BlockSpec((rows, 3 * dilation * d), lambda i, j: (i, 0)),
            out_shape=out_shape,
            scratch_shapes=scratch,
            compiler_params=_cparams(("parallel", "arbitrary")),
            name=f"qkv_d{dilation}",
        )(*x_in, g.reshape(1, d), w)
    return pl.pallas_call(
        functools.partial(_qkv_dilated_kernel, dilation=dilation, n_slabs=n_slabs,
                          chunk=QKV_CHUNK, whole=False, pre_stride=pre_stride),
        grid=(n // tm, 3),
        in_specs=x_specs + [pl.BlockSpec((1, d), lambda i, j: (0, 0)),
                            pl.BlockSpec((d, d), lambda i, j: (0, 3 * group + j))],
        out_specs=pl.BlockSpec((rows, dilation * d), lambda i, j: (i, j)),
        out_shape=out_shape,
        scratch_shapes=scratch,
        compiler_params=_cparams(("parallel", "arbitrary")),
        name=f"qkv_d{dilation}",
    )(*x_in, g.reshape(1, d), w)


def _qkv_diff_kernel(x_ref, g_ref, w_ref, qk_ref, vt_ref, *, bk):
    tm, d = x_ref.shape
    for c in range(tm // bk):
        rows = slice(c * bk, (c + 1) * bk)
        xn = _rmsnorm_f32(x_ref[rows, :], g_ref[...]).astype(BF16)
        q = jnp.dot(xn, w_ref[:, :d], preferred_element_type=F32) * (QK_SCALE * LOG2E)
        qk_ref[rows, :d] = q.astype(qk_ref.dtype)
        qk_ref[rows, d:] = jnp.dot(xn, w_ref[:, d:2 * d], preferred_element_type=F32).astype(qk_ref.dtype)
        v = jnp.dot(xn, w_ref[:, 2 * d:], preferred_element_type=F32)
        for hp in range(d // LANES):
            cols = slice(hp * LANES, (hp + 1) * LANES)
            vt_ref[c, cols, :] = v[:, cols].T.astype(vt_ref.dtype)


def _qkv_diff(x, g, w_qkv, *, bk, tm=1024):
    n, d = x.shape
    return pl.pallas_call(
        functools.partial(_qkv_diff_kernel, bk=bk),
        grid=(n // tm,),
        in_specs=[pl.BlockSpec((tm, d), lambda i: (i, 0)),
                  pl.BlockSpec((1, d), lambda i: (0, 0)),
                  pl.BlockSpec((d, 3 * d), lambda i: (0, 0))],
        out_specs=[pl.BlockSpec((tm, 2 * d), lambda i: (i, 0)),
                   pl.BlockSpec((tm // bk, d, bk), lambda i: (i, 0, 0))],
        out_shape=[jax.ShapeDtypeStruct((n, 2 * d), BF16),
                   jax.ShapeDtypeStruct((n // bk, d, bk), BF16)],
        compiler_params=_cparams(("parallel",)),
        name="qkv_diff",
    )(x, g.reshape(1, d), w_qkv)


def _dilated_kernel(q_ref, kp_ref, kc_ref, vtp_ref, vtc_ref, bias_ref, o_ref, lse_ref,
                    s_sc, m_sc, pv_sc):
    first_step = (pl.program_id(2) == 0).astype(jnp.int32)
    lane = lax.broadcasted_iota(jnp.int32, (BLOCK, LANES), 1)
    row = lax.broadcasted_iota(jnp.int32, (BLOCK, LANES), 0)
    head_row = lax.broadcasted_iota(jnp.int32, (N_HEADS_A, BLOCK), 0)
    lo_row = row < HEAD_DIM
    mask_lo = jnp.where(lane < HEAD_DIM, 1.0, 0.0).astype(BF16)
    mask_hi = jnp.where(lane < HEAD_DIM, 0.0, 1.0).astype(BF16)
    nt = (((1,), (1,)), ((), ()))
    ones = jnp.ones((SUM_ROWS, 2 * BLOCK), BF16)
    n_pairs = N_HEADS_A // 2
    n_qb = q_ref.shape[0] // BLOCK
    items = [(qb, hp) for qb in range(n_qb) for hp in range(n_pairs)]

    def rows(qb):
        return slice(qb * BLOCK, (qb + 1) * BLOCK)

    def prev_cur(prev_ref, cur_ref, qb, sl):
        prev = prev_ref[:, sl] if qb == 0 else cur_ref[rows(qb - 1), sl]
        return prev, cur_ref[rows(qb), sl]

    def scores(it):
        qb, hp = items[it]
        sl = slice(hp * LANES, (hp + 1) * LANES)
        q = q_ref[rows(qb), sl]
        q2 = jnp.concatenate([q * mask_lo, q * mask_hi], axis=0)
        k = jnp.concatenate(prev_cur(kp_ref, kc_ref, qb, sl), axis=0)
        s = lax.dot_general(k, q2, nt, preferred_element_type=F32)
        s = s + bias_ref[first_step if qb == 0 else 0, hp]
        s_sc[it % DIL_RING] = s
        m_sc[it] = jnp.max(s, axis=0, keepdims=True)

    def value_product(it):
        qb, hp = items[it]
        sl = slice(hp * LANES, (hp + 1) * LANES)
        p = jnp.exp2(s_sc[it % DIL_RING] - m_sc[it]).astype(BF16)
        vt = jnp.concatenate(prev_cur(vtp_ref, vtc_ref, qb, sl), axis=1)
        pv_sc[it] = jnp.dot(jnp.concatenate([vt, ones], axis=0), p,
                            preferred_element_type=F32)

    for it in range(len(items) + DIL_RING - 1):
        if it < len(items):
            scores(it)
        if it >= DIL_RING - 1:
            value_product(it - (DIL_RING - 1))
    for qb in range(n_qb):
        lse_t = jnp.zeros((N_HEADS_A, BLOCK), F32)
        for hp in range(n_pairs):
            it = qb * n_pairs + hp
            l = pv_sc[it, LANES:LANES + 1]
            acc = pv_sc[it, :LANES] / l
            o_t = jnp.where(lo_row, acc[:, :BLOCK], acc[:, BLOCK:])
            o_ref[rows(qb), hp * LANES:(hp + 1) * LANES] = o_t.T.astype(o_ref.dtype)
            lse = (m_sc[it] + jnp.log2(l)) * LN2
            lse_t = jnp.where(head_row == 2 * hp, lse[:, :BLOCK], lse_t)
            lse_t = jnp.where(head_row == 2 * hp + 1, lse[:, BLOCK:], lse_t)
        lse_full = jnp.concatenate([lse_t, jnp.zeros((BLOCK - N_HEADS_A, BLOCK), F32)], axis=0)
        lse_ref[rows(qb), :] = lse_full.T


def _dilated_attention(qkv, bias, group, dilation, batch, seq):
    assert seq % (dilation * BLOCK) == 0
    sub_len = seq // dilation
    n_qb = min(DIL_QB, sub_len // BLOCK)
    assert sub_len % (BLOCK * n_qb) == 0
    steps = sub_len // (BLOCK * n_qb)
    qkv_v = qkv.reshape(batch, sub_len, 3 * dilation * D_MODEL)

    def col(c):
        return lambda b, r, n: (b, n, c * dilation + r)

    def col_prev(c):
        return lambda b, r, n: (b, jnp.maximum(n * n_qb - 1, 0), c * dilation + r)

    blk = (None, n_qb * BLOCK, D_MODEL)
    blk_prev = (None, BLOCK, D_MODEL)
    n_items = n_qb * N_HEADS_A // 2
    o, lse = pl.pallas_call(
        _dilated_kernel,
        grid=(batch, dilation, steps),
        in_specs=[pl.BlockSpec(blk, col(0)),
                  pl.BlockSpec(blk_prev, col_prev(1)),
                  pl.BlockSpec(blk, col(1)),
                  pl.BlockSpec(blk_prev, col_prev(2)),
                  pl.BlockSpec(blk, col(2)),
                  pl.BlockSpec((None, 2, N_HEADS_A // 2, 2 * BLOCK, 2 * BLOCK),
                               lambda b, r, n: (group, 0, 0, 0, 0))],
        out_specs=[pl.BlockSpec(blk, lambda b, r, n: (b, n, r)),
                   pl.BlockSpec((None, n_qb * BLOCK, LANES), lambda b, r, n: (b, n, r))],
        out_shape=[jax.ShapeDtypeStruct((batch, sub_len, dilation * D_MODEL), BF16),
                   jax.ShapeDtypeStruct((batch, sub_len, dilation * LANES), F32)],
        scratch_shapes=[pltpu.VMEM((DIL_RING, 2 * BLOCK, 2 * BLOCK), F32),
                        pltpu.VMEM((n_items, 1, 2 * BLOCK), F32),
                        pltpu.VMEM((n_items, LANES + SUM_ROWS, 2 * BLOCK), F32)],
        compiler_params=_cparams(("parallel", "parallel", "arbitrary")),
        name=f"dilated_attn_d{dilation}",
    )(qkv_v, qkv_v, qkv_v, qkv_v, qkv_v, bias)
    return (o.reshape(batch * sub_len, dilation * D_MODEL),
            lse.reshape(batch * sub_len, dilation * LANES))


def _combine_proj_kernel(h_ref, o0_ref, o1_ref, o2_ref, l0_ref, l1_ref, l2_ref,
                         e_ref, w_ref, out_ref, lse_sc, o_sc):
    o_refs = [o0_ref, o1_ref, o2_ref]
    l_refs = [l0_ref, l1_ref, l2_ref]
    tm, d = h_ref.shape
    n_slabs = d // LANES
    for g, (_, dilation) in enumerate(DIL_CONFIGS):
        rows = tm // dilation
        for r in range(dilation):
            dst = pl.ds(r, rows, stride=dilation) if dilation > 1 else slice(None)
            lse_sc[g, dst, :] = l_refs[g][:, r * LANES:(r + 1) * LANES]
            for c in range(n_slabs):
                o_sc[g, c, dst, :] = o_refs[g][:, r * d + c * LANES:r * d + (c + 1) * LANES].astype(F32)
    lses = [lse_sc[g] for g in range(N_GROUPS)]
    mx = jnp.maximum(jnp.maximum(lses[0], lses[1]), lses[2])
    ws = [jnp.exp(l - mx) for l in lses]
    tot = ws[0] + ws[1] + ws[2]
    head_lane = lax.broadcasted_iota(jnp.int32, (tm, LANES), 1) < N_HEADS_A
    packed = None
    for g in range(N_GROUPS):
        a = jnp.where(head_lane, ws[g] / tot, 0.0)
        a_hi = a.astype(BF16).astype(F32)
        for part, piece in enumerate((a_hi, a - a_hi)):
            shift = (part * N_GROUPS + g) * N_HEADS_A
            moved = piece if shift == 0 else pltpu.roll(piece, shift, axis=1)
            packed = moved if packed is None else packed + moved
    ae_all = jnp.dot(packed.astype(BF16), e_ref[...], preferred_element_type=F32)
    aes = [ae_all[:, g * d:(g + 1) * d] for g in range(N_GROUPS)]
    slabs = []
    for c in range(n_slabs):
        cols = slice(c * LANES, (c + 1) * LANES)
        slabs.append(aes[0][:, cols] * o_sc[0, c] + aes[1][:, cols] * o_sc[1, c]
                     + aes[2][:, cols] * o_sc[2, c])
    o = jnp.concatenate(slabs, axis=1).astype(BF16)
    out_ref[...] = h_ref[...] + jnp.dot(o, w_ref[...], preferred_element_type=F32)


def _combine_proj(h, outs, lses, w, *, tm=512):
    n, d = h.shape
    rows = jnp.arange(LANES, dtype=jnp.int32)
    cols = jnp.arange(N_GROUPS * d, dtype=jnp.int32)
    row_group = (rows // N_HEADS_A) % N_GROUPS
    row_head = rows % N_HEADS_A
    row_used = rows < 2 * N_GROUPS * N_HEADS_A
    expand = (row_used[:, None] & (row_group[:, None] == (cols // d)[None, :])
              & (row_head[:, None] == ((cols % d) // HEAD_DIM)[None, :])).astype(BF16)
    row = lambda i: (i, 0)
    full = lambda i: (0, 0)
    dils = [dilation for _, dilation in DIL_CONFIGS]
    assert all(tm % (16 * dilation) == 0 for dilation in dils)
    return pl.pallas_call(
        _combine_proj_kernel,
        grid=(n // tm,),
        in_specs=[pl.BlockSpec((tm, d), row)]
                 + [pl.BlockSpec((tm // dilation, dilation * d), row) for dilation in dils]
                 + [pl.BlockSpec((tm // dilation, dilation * LANES), row) for dilation in dils]
                 + [pl.BlockSpec((LANES, N_GROUPS * d), full), pl.BlockSpec((d, d), full)],
        out_specs=pl.BlockSpec((tm, d), row),
        out_shape=jax.ShapeDtypeStruct((n, d), F32),
        scratch_shapes=[pltpu.VMEM((N_GROUPS, tm, LANES), F32),
                        pltpu.VMEM((N_GROUPS, d // LANES, tm, LANES), F32)],
        compiler_params=_cparams(("parallel",)),
        name="combine_proj",
    )(h, *outs, *lses, expand, w)


def _mlp_ple_kernel(*refs, final_norm, with_proj, n_steps, chunk):
    if with_proj:
        h_ref, o_ref, wo_ref = refs[:3]
        refs = refs[3:]
    else:
        h_ref = refs[0]
        refs = refs[1:]
    p_ref, g_ref, w1_ref, w2_ref, gp_ref, wg_ref, wp_ref, fg_ref, out_ref, xn_ref, acc_ref = refs
    f = pl.program_id(1)
    tm = h_ref.shape[0]

    def step(first, last):
        for ch in range(tm // chunk):
            rows = slice(ch * chunk, (ch + 1) * chunk)
            if first:
                if with_proj:
                    h0 = h_ref[rows, :] + jnp.dot(o_ref[rows, :], wo_ref[...],
                                                  preferred_element_type=F32)
                    out_ref[rows, :] = h0
                else:
                    h0 = h_ref[rows, :]
                xn_ref[rows, :] = _rmsnorm_f32(h0, g_ref[...]).astype(BF16)
            a = jnp.dot(xn_ref[rows, :], w1_ref[...], preferred_element_type=F32)
            a = jnp.maximum(a, 0.0)
            a = (a * a).astype(BF16)
            acc = jnp.dot(a, w2_ref[...], preferred_element_type=F32)
            if not first:
                acc = acc_ref[rows, :] + acc
            if not last:
                acc_ref[rows, :] = acc
                continue
            x = (out_ref[rows, :] if with_proj else h_ref[rows, :]) + acc
            xn = _rmsnorm_f32(x, gp_ref[...]).astype(BF16)
            gate = jax.nn.sigmoid(jnp.dot(xn, wg_ref[...], preferred_element_type=F32))
            proj = jnp.dot(p_ref[rows, :].astype(BF16), wp_ref[...], preferred_element_type=F32)
            y = x + gate * proj
            if final_norm:
                y = _rmsnorm_f32(y, fg_ref[...])
            out_ref[rows, :] = y

    pl.when(f == 0)(functools.partial(step, True, n_steps == 1))
    if n_steps > 2:
        pl.when((f > 0) & (f < n_steps - 1))(functools.partial(step, False, False))
    if n_steps > 1:
        pl.when(f == n_steps - 1)(functools.partial(step, False, True))


def _mlp_ple(h, p_all, layer, g, w1, w2, gp, wg, wp, fg, *, final_norm, proj=None, tm=1024, tf=1024):
    n, d = h.shape
    dff = w1.shape[1]
    pd = p_all.shape[-1]
    vec = lambda i, f: (0, 0)
    row = lambda i, f: (i, 0)
    lead_specs = [pl.BlockSpec((tm, d), row)]
    lead_args = [h]
    if proj is not None:
        o, wo = proj
        lead_specs += [pl.BlockSpec((tm, o.shape[1]), row), pl.BlockSpec(wo.shape, vec)]
        lead_args += [o, wo]
    return pl.pallas_call(
        functools.partial(_mlp_ple_kernel, final_norm=final_norm, with_proj=proj is not None,
                          n_steps=dff // tf, chunk=MLP_CHUNK),
        grid=(n // tm, dff // tf),
        in_specs=lead_specs + [
                  pl.BlockSpec((None, tm, pd), lambda i, f: (layer, i, 0)),
                  pl.BlockSpec((1, d), vec),
                  pl.BlockSpec((d, tf), lambda i, f: (0, f)),
                  pl.BlockSpec((tf, d), lambda i, f: (f, 0)),
                  pl.BlockSpec((1, d), vec),
                  pl.BlockSpec((d, d), vec),
                  pl.BlockSpec((pd, d), vec),
                  pl.BlockSpec((1, d), vec)],
        out_specs=pl.BlockSpec((tm, d), lambda i, f: (i, 0)),
        out_shape=jax.ShapeDtypeStruct((n, d), F32),
        scratch_shapes=[pltpu.VMEM((tm, d), BF16), pltpu.VMEM((tm, d), F32)],
        compiler_params=_cparams(("parallel", "arbitrary"),
                                 vmem_limit=VMEM_LIMIT_PROJ if proj is not None else VMEM_LIMIT),
        name="mlp_ple",
    )(*lead_args, p_all, g.reshape(1, d), w1, w2, gp.reshape(1, d), wg, wp, fg.reshape(1, d))


def _diff_attn_kernel(lq1_ref, lk1_ref, lq2_ref, lk2_ref, q_ref, qn_ref, k_ref, vt_ref, u_ref, sg_ref,
                      o_ref, m_sc, acc_sc, sa_sc, samax_sc, sb_sc, sbmax_sc, *, bq, bk, lambda_init):
    n_parts = bq // bk
    assert bq == n_parts * bk and n_parts % 2 == 0
    i_q = pl.program_id(2)
    q0 = i_q * bq
    nt = (((1,), (1,)), ((), ()))
    lane = lax.broadcasted_iota(jnp.int32, (bq, LANES), 1)

    def query_columns(ref):
        qf = ref[...].astype(F32)
        a = jnp.where(lane < HEAD_DIM, qf, 0.0)
        b = jnp.where(lane >= HEAD_DIM, qf, 0.0)
        pieces = []
        for part in range(n_parts):
            pieces += [a[part * bk:(part + 1) * bk], b[part * bk:(part + 1) * bk]]
        return jnp.concatenate(pieces, axis=0).astype(BF16)

    q2 = query_columns(q_ref)
    q2_next = query_columns(qn_ref)

    m_sc[...] = jnp.full(m_sc.shape, NEG, F32)
    acc_sc[...] = jnp.zeros(acc_sc.shape, F32)

    per_map = bk // LANES
    part_blocks = 2 * per_map
    n_blocks = n_parts * part_blocks
    first_diag = q0 // bk
    ones = jnp.ones((SUM_ROWS, bk), BF16)

    def scores(c, q2x, q0x, s_ref, smax_ref, first_block=0):
        k0 = pl.multiple_of(c * bk, bk)
        lanes = slice(first_block * LANES, n_blocks * LANES)
        s = lax.dot_general(k_ref[pl.ds(k0, bk), :], q2x[lanes], nt,
                            preferred_element_type=F32)
        rows = []
        for jb in range(bk // LANES):
            tiles = []
            for ib in range(first_block, n_blocks):
                part, within = divmod(ib, part_blocks)
                col, blk = divmod(within, per_map)
                i0 = part * bk + blk * LANES
                t = lax.shift_right_arithmetic(q0x + i0 - k0 - jb * LANES, LOG2_LANES)
                t = jnp.where(t < 0, DIFF_MASK_TILE, jnp.minimum(t, DIFF_CONST_TILE))
                tiles.append(u_ref[t, col])
            rows.append(jnp.concatenate(tiles, axis=1))
        s = s + jnp.concatenate(rows, axis=0)
        s_ref[:, lanes] = s
        smax_ref[:, lanes] = jnp.max(s, axis=0, keepdims=True)

    def accumulate(c, s_ref, smax_ref, first_block=0):
        lanes = slice(first_block * LANES, n_blocks * LANES)
        m_prev = m_sc[:, lanes]
        m_new = jnp.maximum(m_prev, smax_ref[:, lanes])
        m_sc[:, lanes] = m_new
        p = jnp.exp2(s_ref[:, lanes] - m_new).astype(BF16)
        v1 = jnp.concatenate([vt_ref[c], ones], axis=0)
        acc_sc[:, lanes] = (jnp.exp2(m_prev - m_new) * acc_sc[:, lanes]
                            + jnp.dot(v1, p, preferred_element_type=F32))

    @pl.when(i_q == 0)
    def _():
        scores(0, q2, q0, sa_sc, samax_sc)

    def body(i, carry):
        scores(2 * i + 1, q2, q0, sb_sc, sbmax_sc)
        accumulate(2 * i, sa_sc, samax_sc)
        scores(2 * i + 2, q2, q0, sa_sc, samax_sc)
        accumulate(2 * i + 1, sb_sc, sbmax_sc)
        return carry

    lax.fori_loop(0, first_diag // 2, body, 0)

    for dt in range(n_parts // 2):
        ja, jb = 2 * dt, 2 * dt + 1
        scores(first_diag + jb, q2, q0, sb_sc, sbmax_sc, first_block=jb * part_blocks)
        accumulate(first_diag + ja, sa_sc, samax_sc, first_block=ja * part_blocks)
        if jb + 1 < n_parts:
            scores(first_diag + jb + 1, q2, q0, sa_sc, samax_sc, first_block=(jb + 1) * part_blocks)
        else:
            scores(0, q2_next, q0 + bq, sa_sc, samax_sc)
        accumulate(first_diag + jb, sb_sc, sbmax_sc, first_block=jb * part_blocks)

    lam = (jnp.exp(jnp.sum(lq1_ref[...] * lk1_ref[...], keepdims=True))
           - jnp.exp(jnp.sum(lq2_ref[...] * lk2_ref[...], keepdims=True)) + lambda_init)
    accl = acc_sc[...]
    acc = accl[:LANES] / accl[LANES:LANES + 1]
    map0 = jnp.concatenate([acc[:, 2 * part * bk:(2 * part + 1) * bk] for part in range(n_parts)], axis=1)
    map1 = jnp.concatenate([acc[:, (2 * part + 1) * bk:(2 * part + 2) * bk] for part in range(n_parts)], axis=1)
    o = (map0 - lam * map1).T
    y = _rmsnorm_f32(o, sg_ref[...]) * (1.0 - lambda_init)
    o_ref[...] = y.astype(o_ref.dtype)


def _diff_attention(qk, vt, u, lq1, lk1, lq2, lk2, subln, lambda_init, batch, seq, *, bq, bk):
    nh = N_HEADS_B
    nkt = seq // bk
    nq = seq // bq
    vec = lambda a: a.reshape(1, -1)
    small = pl.BlockSpec((1, HEAD_DIM), lambda b, h, i: (0, 0))
    return pl.pallas_call(
        functools.partial(_diff_attn_kernel, bq=bq, bk=bk, lambda_init=lambda_init),
        grid=(batch, nh, seq // bq),
        in_specs=[small, small, small, small,
                  pl.BlockSpec((None, bq, LANES), lambda b, h, i: (b, i, h)),
                  pl.BlockSpec((None, bq, LANES), lambda b, h, i: (b, jnp.minimum(i + 1, nq - 1), h)),
                  pl.BlockSpec((None, seq, LANES), lambda b, h, i: (b, 0, nh + h)),
                  pl.BlockSpec((nkt, LANES, bk), lambda b, h, i: (b, h, 0)),
                  pl.BlockSpec((DIFF_N_TILES, 2, LANES, LANES), lambda b, h, i: (0, h, 0, 0)),
                  pl.BlockSpec((1, LANES), lambda b, h, i: (0, 0))],
        out_specs=pl.BlockSpec((None, bq, LANES), lambda b, h, i: (b, i, h)),
        out_shape=jax.ShapeDtypeStruct((batch, seq, D_MODEL), BF16),
        scratch_shapes=[pltpu.VMEM((1, 2 * bq), F32),
                        pltpu.VMEM((LANES + SUM_ROWS, 2 * bq), F32),
                        pltpu.VMEM((bk, 2 * bq), F32),
                        pltpu.VMEM((1, 2 * bq), F32),
                        pltpu.VMEM((bk, 2 * bq), F32),
                        pltpu.VMEM((1, 2 * bq), F32)],
        compiler_params=_cparams(("parallel", "parallel", "arbitrary")),
        name="diff_attn",
    )(vec(lq1), vec(lk1), vec(lq2), vec(lk2), qk, qk, qk, vt, u, vec(subln))


def kernel(x, p, rel_bias, a_w_qkv, a_w_o, b_w_qkv, b_w_o, b_lambda_q1, b_lambda_k1, b_lambda_q2, b_lambda_k2, b_subln, norm_mix, norm_mlp, w_ff1, w_ff2, norm_ple, w_ple_gate, w_ple_proj, final_norm):
    batch, seq, d = x.shape
    depth = p.shape[0]
    n = batch * seq
    h = x.reshape(n, d)
    p_all = p.reshape(depth, n, p.shape[-1])
    n_mixers = 2

    for i in range(depth):
        j = i // n_mixers
        if i % n_mixers == 0:
            w_a = a_w_qkv[j].astype(BF16)
            bias = _build_dilated_bias(rel_bias)
            outs, lses = [], []
            for g, (_, dilation) in enumerate(DIL_CONFIGS):
                qkv = _qkv_dilated(h, norm_mix[i], w_a, g, dilation, tm=max(1024, BLOCK * dilation))
                o_g, lse_g = _dilated_attention(qkv, bias, g, dilation, batch, seq)
                outs.append(o_g)
                lses.append(lse_g)
            h = _combine_proj(h, outs, lses, a_w_o[j].astype(BF16))
            proj = None
        else:
            lambda_init = 0.8 - 0.6 * math.exp(-0.3 * i)
            bq, bk = 2048, 512
            w_b = b_w_qkv[j].astype(BF16)
            qk, vt = _qkv_diff(h, norm_mix[i], w_b, bk=bk)
            u = _build_diff_bias(rel_bias)
            o = _diff_attention(qk.reshape(batch, seq, 2 * d), vt, u,
                                b_lambda_q1[j], b_lambda_k1[j], b_lambda_q2[j], b_lambda_k2[j],
                                b_subln[j], lambda_init, batch, seq, bq=bq, bk=bk)
            proj = (o.reshape(n, d), b_w_o[j].astype(BF16))
        h = _mlp_ple(h, p_all, i, norm_mlp[i], w_ff1[i].astype(BF16), w_ff2[i].astype(BF16),
                     norm_ple[i], w_ple_gate[i].astype(BF16), w_ple_proj[i].astype(BF16),
                     final_norm, final_norm=(i == depth - 1), proj=proj)
    return h.reshape(batch, seq, d)
```

```python
import functools
import math

import jax
import jax.numpy as jnp
from jax import lax
from jax.experimental import pallas as pl
from jax.experimental.pallas import tpu as pltpu

F32 = jnp.float32
BF16 = jnp.bfloat16

D_MODEL = 1024
HEAD_DIM = 64
BLOCK = 128
QKV_CHUNK = 512
MLP_CHUNK = 512
DIL_RING = 6
DIL_QB = 8
DIL_CONFIGS = ((128, 1), (512, 4), (2048, 16))
N_GROUPS = len(DIL_CONFIGS)
N_HEADS_A = D_MODEL // HEAD_DIM
N_HEADS_B = D_MODEL // (2 * HEAD_DIM)
N_BUCKETS = 32
MAX_DISTANCE = 2048
N_BIAS_COLS = 16
EPS = 1e-6
NEG = -1e30
LANES = 128
SUBLANES = 8
LOG2_LANES = 7
QK_SCALE = HEAD_DIM ** -0.5
LOG2E = math.log2(math.e)
LN2 = math.log(2.0)

DIFF_CONST_TILE = (MAX_DISTANCE + LANES - 1) // LANES + 1
DIFF_MASK_TILE = DIFF_CONST_TILE + 1
DIFF_N_TILES = DIFF_MASK_TILE + 1
SUM_ROWS = 16

VMEM_LIMIT = 48 * 1024 * 1024
VMEM_LIMIT_PROJ = 56 * 1024 * 1024


def _cparams(sem, vmem_limit=VMEM_LIMIT):
    return pltpu.CompilerParams(dimension_semantics=sem, vmem_limit_bytes=vmem_limit)


def _rmsnorm_f32(x, g):
    ms = jnp.mean(x * x, axis=-1, keepdims=True)
    return x * lax.rsqrt(ms + EPS) * g


def _rel_bucket(dist):
    n = jnp.maximum(dist, 0)
    max_exact = N_BUCKETS // 2
    nf = jnp.maximum(n, 1).astype(F32)
    large = max_exact + (jnp.log(nf / max_exact) / math.log(MAX_DISTANCE / max_exact)
                         * (N_BUCKETS - max_exact)).astype(jnp.int32)
    large = jnp.minimum(large, N_BUCKETS - 1)
    return jnp.where(n < max_exact, n, large)


def _table_lookup(bucket, tab_ref, col):
    acc = jnp.zeros(bucket.shape, F32)
    for b in range(N_BUCKETS):
        acc = jnp.where(bucket == b, tab_ref[b, col], acc)
    return acc


def _dilated_bias_kernel(tab_ref, o_ref):
    g = pl.program_id(0)
    dilation = jnp.where(g == 0, DIL_CONFIGS[0][1],
                         jnp.where(g == 1, DIL_CONFIGS[1][1], DIL_CONFIGS[2][1]))
    kj = lax.broadcasted_iota(jnp.int32, (2 * BLOCK, BLOCK), 0)
    qi = lax.broadcasted_iota(jnp.int32, (2 * BLOCK, BLOCK), 1)
    sub = qi + BLOCK - kj
    band = (sub >= 0) & (sub <= BLOCK)
    band_first = band & (kj >= BLOCK)
    bucket = _rel_bucket(sub * dilation)
    for c in range(N_BIAS_COLS):
        lanes = slice((c % 2) * BLOCK, (c % 2 + 1) * BLOCK)
        bias = _table_lookup(bucket, tab_ref, c) * LOG2E
        o_ref[0, c // 2, :, lanes] = jnp.where(band, bias, NEG)
        o_ref[1, c // 2, :, lanes] = jnp.where(band_first, bias, NEG)


def _build_dilated_bias(rel_bias):
    return pl.pallas_call(
        _dilated_bias_kernel,
        grid=(N_GROUPS,),
        in_specs=[pl.BlockSpec(memory_space=pltpu.SMEM)],
        out_specs=pl.BlockSpec((None, 2, N_HEADS_A // 2, 2 * BLOCK, 2 * BLOCK),
                               lambda g: (g, 0, 0, 0, 0)),
        out_shape=jax.ShapeDtypeStruct((N_GROUPS, 2, N_HEADS_A // 2, 2 * BLOCK, 2 * BLOCK), F32),
        compiler_params=_cparams(("arbitrary",)),
        name="dilated_bias",
    )(rel_bias)


def _diff_bias_kernel(tab_ref, o_ref):
    t = pl.program_id(0)
    kj = lax.broadcasted_iota(jnp.int32, (LANES, LANES), 0)
    qi = lax.broadcasted_iota(jnp.int32, (LANES, LANES), 1)
    dist = t * LANES + qi - kj
    masked = (dist < 0) | (t == DIFF_MASK_TILE)
    bucket = _rel_bucket(dist)
    init = jnp.where(masked, NEG, 0.0)
    for c in range(N_BIAS_COLS):
        o_ref[c] = init
    live = jnp.where(masked, -1, bucket)

    def body(b, carry):
        hit = live == b
        for c in range(N_BIAS_COLS):
            o_ref[c] = jnp.where(hit, tab_ref[b, c] * LOG2E, o_ref[c])
        return carry

    lax.fori_loop(jnp.min(bucket), jnp.max(bucket) + 1, body, 0)


def _build_diff_bias(rel_bias):
    return pl.pallas_call(
        _diff_bias_kernel,
        grid=(DIFF_N_TILES,),
        in_specs=[pl.BlockSpec(memory_space=pltpu.SMEM)],
        out_specs=pl.BlockSpec((None, N_BIAS_COLS, LANES, LANES), lambda t: (t, 0, 0, 0)),
        out_shape=jax.ShapeDtypeStruct((DIFF_N_TILES, N_BIAS_COLS, LANES, LANES), F32),
        compiler_params=_cparams(("arbitrary",)),
        name="diff_bias",
    )(rel_bias)


def _qkv_dilated_kernel(*refs, dilation, n_slabs, chunk, whole, pre_stride):
    n_x = n_slabs or 1
    x_refs = refs[:n_x]
    g_ref, w_ref, o_ref, xn_sc = refs[n_x:n_x + 4]
    tm, d_model = xn_sc.shape
    rows = tm // dilation
    if pre_stride:
        pre_sc = refs[n_x + 4]
        pre_rows = tm // pre_stride
        outer = dilation // pre_stride

    def pre_regroup():
        for s, x_ref in enumerate(x_refs):
            for b in range(pre_stride):
                pre_sc[s, b * pre_rows:(b + 1) * pre_rows, :] = x_ref[pl.ds(b, pre_rows, stride=pre_stride), :]

    def normalise(r, a, b):
        if n_slabs is None:
            xs = [x_refs[0][a:b, :]]
        elif pre_stride:
            start = (r % pre_stride) * pre_rows + r // pre_stride + (a - r * rows) * outer
            xs = [pre_sc[s, pl.ds(start, b - a, stride=outer), :] for s in range(n_slabs)]
        else:
            xs = [x_ref[pl.ds(r + (a - r * rows) * dilation, b - a, stride=dilation), :]
                  for x_ref in x_refs]
        slab_w = d_model // len(xs)
        sq = xs[0] * xs[0]
        for x in xs[1:]:
            sq = sq + x * x
        scale = lax.rsqrt(jnp.sum(sq, axis=-1, keepdims=True) * (1.0 / d_model) + EPS)
        for s, x in enumerate(xs):
            cols = slice(s * slab_w, (s + 1) * slab_w)
            xn_sc[a:b, cols] = (x * scale * g_ref[:, cols]).astype(BF16)

    def store(c, y, lo, r, a, b, col0):
        if c < 2:
            val = y[a - lo:b - lo]
            if c == 0:
                val = val * (QK_SCALE * LOG2E)
            o_ref[a - r * rows:b - r * rows, col0:col0 + d_model] = val.astype(o_ref.dtype)
            return
        for blk in range((b - a) // BLOCK):
            y0 = a - lo + blk * BLOCK
            s0 = a - r * rows + blk * BLOCK
            for hp in range(d_model // LANES):
                tile = y[y0:y0 + BLOCK, hp * LANES:(hp + 1) * LANES]
                o_ref[s0:s0 + BLOCK, col0 + hp * LANES:col0 + (hp + 1) * LANES] = (
                    tile.T.astype(o_ref.dtype))

    def run(comps, with_norm):
        if with_norm and pre_stride:
            pre_regroup()
        for ch in range(tm // chunk):
            lo, hi = ch * chunk, (ch + 1) * chunk
            pieces = [(r, max(lo, r * rows), min(hi, (r + 1) * rows)) for r in range(dilation)
                      if max(lo, r * rows) < min(hi, (r + 1) * rows)]
            if with_norm:
                for r, a, b in pieces:
                    normalise(r, a, b)
            xn = xn_sc[lo:hi, :]
            for c in comps:
                w = w_ref[:, c * d_model:(c + 1) * d_model] if whole else w_ref[...]
                y = jnp.dot(xn, w, preferred_element_type=F32)
                for r, a, b in pieces:
                    store(c, y, lo, r, a, b, ((c * dilation if whole else 0) + r) * d_model)

    if whole:
        run((0, 1, 2), True)
    else:
        j = pl.program_id(1)
        for c in range(3):
            pl.when(j == c)(functools.partial(run, (c,), c == 0))


def _qkv_dilated(x, g, w, group, dilation, *, tm):
    n, d = x.shape
    assert w.shape[0] == d and w.shape[1] % (3 * d) == 0
    rows = tm // dilation
    assert tm % dilation == 0 and rows % BLOCK == 0
    if dilation == 1:
        n_slabs = None
        x_in = [x]
        x_specs = [pl.BlockSpec((tm, d), lambda i, j: (i, 0))]
    else:
        n_slabs = d // LANES
        x_in = [x] * n_slabs
        x_specs = [pl.BlockSpec((tm, LANES), functools.partial(lambda i, j, s: (i, s), s=s))
                   for s in range(n_slabs)]
    pre_stride = 4 if dilation % 8 == 0 else None
    scratch = [pltpu.VMEM((tm, d), BF16)]
    if pre_stride:
        scratch.append(pltpu.VMEM((n_slabs, tm, LANES), F32))
    out_shape = jax.ShapeDtypeStruct((n // dilation, 3 * dilation * d), BF16)
    whole_bytes = 2 * (tm * 3 * d * 2) + 2 * (d * 3 * d * 2) + 2 * (tm * d * 4) + 4 * QKV_CHUNK * d * 4
    assert tm % QKV_CHUNK == 0 and (QKV_CHUNK % rows == 0 or rows % QKV_CHUNK == 0)
    if whole_bytes <= VMEM_LIMIT:
        return pl.pallas_call(
            functools.partial(_qkv_dilated_kernel, dilation=dilation, n_slabs=n_slabs,
                              chunk=QKV_CHUNK, whole=True, pre_stride=pre_stride),
            grid=(n // tm, 1),
            in_specs=x_specs + [pl.BlockSpec((1, d), lambda i, j: (0, 0)),
                                pl.BlockSpec((d, 3 * d), lambda i, j: (0, group))],
            out_specs=pl.BlockSpec((rows, 3 * dilation * d), lambda i, j: (i, 0)),
            out_shape=out_shape,
            scratch_shapes=scratch,
            compiler_params=_cparams(("parallel", "arbitrary")),
            name=f"qkv_d{dilation}",
        )(*x_in, g.reshape(1, d), w)
    return pl.pallas_call(
        functools.partial(_qkv_dilated_kernel, dilation=dilation, n_slabs=n_slabs,
                          chunk=QKV_CHUNK, whole=False, pre_stride=pre_stride),
        grid=(n // tm, 3),
        in_specs=x_specs + [pl.BlockSpec((1, d), lambda i, j: (0, 0)),
                            pl.BlockSpec((d, d), lambda i, j: (0, 3 * group + j))],
        out_specs=pl.BlockSpec((rows, dilation * d), lambda i, j: (i, j)),
        out_shape=out_shape,
        scratch_shapes=scratch,
        compiler_params=_cparams(("parallel", "arbitrary")),
        name=f"qkv_d{dilation}",
    )(*x_in, g.reshape(1, d), w)


def _qkv_diff_kernel(x_ref, g_ref, w_ref, qk_ref, vt_ref, *, bk):
    tm, d = x_ref.shape
    for c in range(tm // bk):
        rows = slice(c * bk, (c + 1) * bk)
        xn = _rmsnorm_f32(x_ref[rows, :], g_ref[...]).astype(BF16)
        q = jnp.dot(xn, w_ref[:, :d], preferred_element_type=F32) * (QK_SCALE * LOG2E)
        qk_ref[rows, :d] = q.astype(qk_ref.dtype)
        qk_ref[rows, d:] = jnp.dot(xn, w_ref[:, d:2 * d], preferred_element_type=F32).astype(qk_ref.dtype)
        v = jnp.dot(xn, w_ref[:, 2 * d:], preferred_element_type=F32)
        for hp in range(d // LANES):
            cols = slice(hp * LANES, (hp + 1) * LANES)
            vt_ref[c, cols, :] = v[:, cols].T.astype(vt_ref.dtype)


def _qkv_diff(x, g, w_qkv, *, bk, tm=1024):
    n, d = x.shape
    return pl.pallas_call(
        functools.partial(_qkv_diff_kernel, bk=bk),
        grid=(n // tm,),
        in_specs=[pl.BlockSpec((tm, d), lambda i: (i, 0)),
                  pl.BlockSpec((1, d), lambda i: (0, 0)),
                  pl.BlockSpec((d, 3 * d), lambda i: (0, 0))],
        out_specs=[pl.BlockSpec((tm, 2 * d), lambda i: (i, 0)),
                   pl.BlockSpec((tm // bk, d, bk), lambda i: (i, 0, 0))],
        out_shape=[jax.ShapeDtypeStruct((n, 2 * d), BF16),
                   jax.ShapeDtypeStruct((n // bk, d, bk), BF16)],
        compiler_params=_cparams(("parallel",)),
        name="qkv_diff",
    )(x, g.reshape(1, d), w_qkv)


def _dilated_kernel(q_ref, kp_ref, kc_ref, vtp_ref, vtc_ref, bias_ref, o_ref, lse_ref,
                    s_sc, m_sc, pv_sc):
    first_step = (pl.program_id(2) == 0).astype(jnp.int32)
    lane = lax.broadcasted_iota(jnp.int32, (BLOCK, LANES), 1)
    row = lax.broadcasted_iota(jnp.int32, (BLOCK, LANES), 0)
    head_row = lax.broadcasted_iota(jnp.int32, (N_HEADS_A, BLOCK), 0)
    lo_row = row < HEAD_DIM
    mask_lo = jnp.where(lane < HEAD_DIM, 1.0, 0.0).astype(BF16)
    mask_hi = jnp.where(lane < HEAD_DIM, 0.0, 1.0).astype(BF16)
    nt = (((1,), (1,)), ((), ()))
    ones = jnp.ones((SUM_ROWS, 2 * BLOCK), BF16)
    n_pairs = N_HEADS_A // 2
    n_qb = q_ref.shape[0] // BLOCK
    items = [(qb, hp) for qb in range(n_qb) for hp in range(n_pairs)]

    def rows(qb):
        return slice(qb * BLOCK, (qb + 1) * BLOCK)

    def prev_cur(prev_ref, cur_ref, qb, sl):
        prev = prev_ref[:, sl] if qb == 0 else cur_ref[rows(qb - 1), sl]
        return prev, cur_ref[rows(qb), sl]

    def scores(it):
        qb, hp = items[it]
        sl = slice(hp * LANES, (hp + 1) * LANES)
        q = q_ref[rows(qb), sl]
        q2 = jnp.concatenate([q * mask_lo, q * mask_hi], axis=0)
        k = jnp.concatenate(prev_cur(kp_ref, kc_ref, qb, sl), axis=0)
        s = lax.dot_general(k, q2, nt, preferred_element_type=F32)
        s = s + bias_ref[first_step if qb == 0 else 0, hp]
        s_sc[it % DIL_RING] = s
        m_sc[it] = jnp.max(s, axis=0, keepdims=True)

    def value_product(it):
        qb, hp = items[it]
        sl = slice(hp * LANES, (hp + 1) * LANES)
        p = jnp.exp2(s_sc[it % DIL_RING] - m_sc[it]).astype(BF16)
        vt = jnp.concatenate(prev_cur(vtp_ref, vtc_ref, qb, sl), axis=1)
        pv_sc[it] = jnp.dot(jnp.concatenate([vt, ones], axis=0), p,
                            preferred_element_type=F32)

    for it in range(len(items) + DIL_RING - 1):
        if it < len(items):
            scores(it)
        if it >= DIL_RING - 1:
            value_product(it - (DIL_RING - 1))
    for qb in range(n_qb):
        lse_t = jnp.zeros((N_HEADS_A, BLOCK), F32)
        for hp in range(n_pairs):
            it = qb * n_pairs + hp
            l = pv_sc[it, LANES:LANES + 1]
            acc = pv_sc[it, :LANES] / l
            o_t = jnp.where(lo_row, acc[:, :BLOCK], acc[:, BLOCK:])
            o_ref[rows(qb), hp * LANES:(hp + 1) * LANES] = o_t.T.astype(o_ref.dtype)
            lse = (m_sc[it] + jnp.log2(l)) * LN2
            lse_t = jnp.where(head_row == 2 * hp, lse[:, :BLOCK], lse_t)
            lse_t = jnp.where(head_row == 2 * hp + 1, lse[:, BLOCK:], lse_t)
        lse_full = jnp.concatenate([lse_t, jnp.zeros((BLOCK - N_HEADS_A, BLOCK), F32)], axis=0)
        lse_ref[rows(qb), :] = lse_full.T


def _dilated_attention(qkv, bias, group, dilation, batch, seq):
    assert seq % (dilation * BLOCK) == 0
    sub_len = seq // dilation
    n_qb = min(DIL_QB, sub_len // BLOCK)
    assert sub_len % (BLOCK * n_qb) == 0
    steps = sub_len // (BLOCK * n_qb)
    qkv_v = qkv.reshape(batch, sub_len, 3 * dilation * D_MODEL)

    def col(c):
        return lambda b, r, n: (b, n, c * dilation + r)

    def col_prev(c):
        return lambda b, r, n: (b, jnp.maximum(n * n_qb - 1, 0), c * dilation + r)

    blk = (None, n_qb * BLOCK, D_MODEL)
    blk_prev = (None, BLOCK, D_MODEL)
    n_items = n_qb * N_HEADS_A // 2
    o, lse = pl.pallas_call(
        _dilated_kernel,
        grid=(batch, dilation, steps),
        in_specs=[pl.BlockSpec(blk, col(0)),
                  pl.BlockSpec(blk_prev, col_prev(1)),
                  pl.BlockSpec(blk, col(1)),
                  pl.BlockSpec(blk_prev, col_prev(2)),
                  pl.BlockSpec(blk, col(2)),
                  pl.BlockSpec((None, 2, N_HEADS_A // 2, 2 * BLOCK, 2 * BLOCK),
                               lambda b, r, n: (group, 0, 0, 0, 0))],
        out_specs=[pl.BlockSpec(blk, lambda b, r, n: (b, n, r)),
                   pl.BlockSpec((None, n_qb * BLOCK, LANES), lambda b, r, n: (b, n, r))],
        out_shape=[jax.ShapeDtypeStruct((batch, sub_len, dilation * D_MODEL), BF16),
                   jax.ShapeDtypeStruct((batch, sub_len, dilation * LANES), F32)],
        scratch_shapes=[pltpu.VMEM((DIL_RING, 2 * BLOCK, 2 * BLOCK), F32),
                        pltpu.VMEM((n_items, 1, 2 * BLOCK), F32),
                        pltpu.VMEM((n_items, LANES + SUM_ROWS, 2 * BLOCK), F32)],
        compiler_params=_cparams(("parallel", "parallel", "arbitrary")),
        name=f"dilated_attn_d{dilation}",
    )(qkv_v, qkv_v, qkv_v, qkv_v, qkv_v, bias)
    return (o.reshape(batch * sub_len, dilation * D_MODEL),
            lse.reshape(batch * sub_len, dilation * LANES))


def _combine_proj_kernel(h_ref, o0_ref, o1_ref, o2_ref, l0_ref, l1_ref, l2_ref,
                         e_ref, w_ref, out_ref, lse_sc, o_sc):
    o_refs = [o0_ref, o1_ref, o2_ref]
    l_refs = [l0_ref, l1_ref, l2_ref]
    tm, d = h_ref.shape
    n_slabs = d // LANES
    for g, (_, dilation) in enumerate(DIL_CONFIGS):
        rows = tm // dilation
        for r in range(dilation):
            dst = pl.ds(r, rows, stride=dilation) if dilation > 1 else slice(None)
            lse_sc[g, dst, :] = l_refs[g][:, r * LANES:(r + 1) * LANES]
            for c in range(n_slabs):
                o_sc[g, c, dst, :] = o_refs[g][:, r * d + c * LANES:r * d + (c + 1) * LANES].astype(F32)
    lses = [lse_sc[g] for g in range(N_GROUPS)]
    mx = jnp.maximum(jnp.maximum(lses[0], lses[1]), lses[2])
    ws = [jnp.exp(l - mx) for l in lses]
    tot = ws[0] + ws[1] + ws[2]
    head_lane = lax.broadcasted_iota(jnp.int32, (tm, LANES), 1) < N_HEADS_A
    packed = None
    for g in range(N_GROUPS):
        a = jnp.where(head_lane, ws[g] / tot, 0.0)
        a_hi = a.astype(BF16).astype(F32)
        for part, piece in enumerate((a_hi, a - a_hi)):
            shift = (part * N_GROUPS + g) * N_HEADS_A
            moved = piece if shift == 0 else pltpu.roll(piece, shift, axis=1)
            packed = moved if packed is None else packed + moved
    ae_all = jnp.dot(packed.astype(BF16), e_ref[...], preferred_element_type=F32)
    aes = [ae_all[:, g * d:(g + 1) * d] for g in range(N_GROUPS)]
    slabs = []
    for c in range(n_slabs):
        cols = slice(c * LANES, (c + 1) * LANES)
        slabs.append(aes[0][:, cols] * o_sc[0, c] + aes[1][:, cols] * o_sc[1, c]
                     + aes[2][:, cols] * o_sc[2, c])
    o = jnp.concatenate(slabs, axis=1).astype(BF16)
    out_ref[...] = h_ref[...] + jnp.dot(o, w_ref[...], preferred_element_type=F32)


def _combine_proj(h, outs, lses, w, *, tm=512):
    n, d = h.shape
    rows = jnp.arange(LANES, dtype=jnp.int32)
    cols = jnp.arange(N_GROUPS * d, dtype=jnp.int32)
    row_group = (rows // N_HEADS_A) % N_GROUPS
    row_head = rows % N_HEADS_A
    row_used = rows < 2 * N_GROUPS * N_HEADS_A
    expand = (row_used[:, None] & (row_group[:, None] == (cols // d)[None, :])
              & (row_head[:, None] == ((cols % d) // HEAD_DIM)[None, :])).astype(BF16)
    row = lambda i: (i, 0)
    full = lambda i: (0, 0)
    dils = [dilation for _, dilation in DIL_CONFIGS]
    assert all(tm % (16 * dilation) == 0 for dilation in dils)
    return pl.pallas_call(
        _combine_proj_kernel,
        grid=(n // tm,),
        in_specs=[pl.BlockSpec((tm, d), row)]
                 + [pl.BlockSpec((tm // dilation, dilation * d), row) for dilation in dils]
                 + [pl.BlockSpec((tm // dilation, dilation * LANES), row) for dilation in dils]
                 + [pl.BlockSpec((LANES, N_GROUPS * d), full), pl.BlockSpec((d, d), full)],
        out_specs=pl.BlockSpec((tm, d), row),
        out_shape=jax.ShapeDtypeStruct((n, d), F32),
        scratch_shapes=[pltpu.VMEM((N_GROUPS, tm, LANES), F32),
                        pltpu.VMEM((N_GROUPS, d // LANES, tm, LANES), F32)],
        compiler_params=_cparams(("parallel",)),
        name="combine_proj",
    )(h, *outs, *lses, expand, w)


def _mlp_ple_kernel(*refs, final_norm, with_proj, n_steps, chunk):
    if with_proj:
        h_ref, o_ref, wo_ref = refs[:3]
        refs = refs[3:]
    else:
        h_ref = refs[0]
        refs = refs[1:]
    p_ref, g_ref, w1_ref, w2_ref, gp_ref, wg_ref, wp_ref, fg_ref, out_ref, xn_ref, acc_ref = refs
    f = pl.program_id(1)
    tm = h_ref.shape[0]

    def step(first, last):
        for ch in range(tm // chunk):
            rows = slice(ch * chunk, (ch + 1) * chunk)
            if first:
                if with_proj:
                    h0 = h_ref[rows, :] + jnp.dot(o_ref[rows, :], wo_ref[...],
                                                  preferred_element_type=F32)
                    out_ref[rows, :] = h0
                else:
                    h0 = h_ref[rows, :]
                xn_ref[rows, :] = _rmsnorm_f32(h0, g_ref[...]).astype(BF16)
            a = jnp.dot(xn_ref[rows, :], w1_ref[...], preferred_element_type=F32)
            a = jnp.maximum(a, 0.0)
            a = (a * a).astype(BF16)
            acc = jnp.dot(a, w2_ref[...], preferred_element_type=F32)
            if not first:
                acc = acc_ref[rows, :] + acc
            if not last:
                acc_ref[rows, :] = acc
                continue
            x = (out_ref[rows, :] if with_proj else h_ref[rows, :]) + acc
            xn = _rmsnorm_f32(x, gp_ref[...]).astype(BF16)
            gate = jax.nn.sigmoid(jnp.dot(xn, wg_ref[...], preferred_element_type=F32))
            proj = jnp.dot(p_ref[rows, :].astype(BF16), wp_ref[...], preferred_element_type=F32)
            y = x + gate * proj
            if final_norm:
                y = _rmsnorm_f32(y, fg_ref[...])
            out_ref[rows, :] = y

    pl.when(f == 0)(functools.partial(step, True, n_steps == 1))
    if n_steps > 2:
        pl.when((f > 0) & (f < n_steps - 1))(functools.partial(step, False, False))
    if n_steps > 1:
        pl.when(f == n_steps - 1)(functools.partial(step, False, True))


def _mlp_ple(h, p_all, layer, g, w1, w2, gp, wg, wp, fg, *, final_norm, proj=None, tm=1024, tf=1024):
    n, d = h.shape
    dff = w1.shape[2]
    pd = p_all.shape[-1]
    vec = lambda i, f: (0, 0)
    lvec = lambda i, f: (layer, 0, 0)
    row = lambda i, f: (i, 0)
    lead_specs = [pl.BlockSpec((tm, d), row)]
    lead_args = [h]
    if proj is not None:
        o, wo = proj
        lead_specs += [pl.BlockSpec((tm, o.shape[1]), row), pl.BlockSpec(wo.shape, vec)]
        lead_args += [o, wo]
    return pl.pallas_call(
        functools.partial(_mlp_ple_kernel, final_norm=final_norm, with_proj=proj is not None,
                          n_steps=dff // tf, chunk=MLP_CHUNK),
        grid=(n // tm, dff // tf),
        in_specs=lead_specs + [
                  pl.BlockSpec((None, tm, pd), lambda i, f: (layer, i, 0)),
                  pl.BlockSpec((1, d), vec),
                  pl.BlockSpec((None, d, tf), lambda i, f: (layer, 0, f)),
                  pl.BlockSpec((None, tf, d), lambda i, f: (layer, f, 0)),
                  pl.BlockSpec((1, d), vec),
                  pl.BlockSpec((None, d, d), lvec),
                  pl.BlockSpec((None, pd, d), lvec),
                  pl.BlockSpec((1, d), vec)],
        out_specs=pl.BlockSpec((tm, d), lambda i, f: (i, 0)),
        out_shape=jax.ShapeDtypeStruct((n, d), F32),
        scratch_shapes=[pltpu.VMEM((tm, d), BF16), pltpu.VMEM((tm, d), F32)],
        compiler_params=_cparams(("parallel", "arbitrary"),
                                 vmem_limit=VMEM_LIMIT_PROJ if proj is not None else VMEM_LIMIT),
        name="mlp_ple",
    )(*lead_args, p_all, g.reshape(1, d), w1, w2, gp.reshape(1, d), wg, wp, fg.reshape(1, d))


def _diff_attn_kernel(lq1_ref, lk1_ref, lq2_ref, lk2_ref, q_ref, qn_ref, k_ref, vt_ref, u_ref, sg_ref,
                      o_ref, m_sc, acc_sc, sa_sc, samax_sc, sb_sc, sbmax_sc, *, bq, bk, lambda_init):
    n_parts = bq // bk
    assert bq == n_parts * bk and n_parts % 2 == 0
    i_q = pl.program_id(2)
    q0 = i_q * bq
    nt = (((1,), (1,)), ((), ()))
    lane = lax.broadcasted_iota(jnp.int32, (bq, LANES), 1)

    def query_columns(ref):
        qf = ref[...].astype(F32)
        a = jnp.where(lane < HEAD_DIM, qf, 0.0)
        b = jnp.where(lane >= HEAD_DIM, qf, 0.0)
        pieces = []
        for part in range(n_parts):
            pieces += [a[part * bk:(part + 1) * bk], b[part * bk:(part + 1) * bk]]
        return jnp.concatenate(pieces, axis=0).astype(BF16)

    q2 = query_columns(q_ref)
    q2_next = query_columns(qn_ref)

    m_sc[...] = jnp.full(m_sc.shape, NEG, F32)
    acc_sc[...] = jnp.zeros(acc_sc.shape, F32)

    per_map = bk // LANES
    part_blocks = 2 * per_map
    n_blocks = n_parts * part_blocks
    first_diag = q0 // bk
    ones = jnp.ones((SUM_ROWS, bk), BF16)

    def scores(c, q2x, q0x, s_ref, smax_ref, first_block=0):
        k0 = pl.multiple_of(c * bk, bk)
        lanes = slice(first_block * LANES, n_blocks * LANES)
        s = lax.dot_general(k_ref[pl.ds(k0, bk), :], q2x[lanes], nt,
                            preferred_element_type=F32)
        rows = []
        for jb in range(bk // LANES):
            tiles = []
            for ib in range(first_block, n_blocks):
                part, within = divmod(ib, part_blocks)
                col, blk = divmod(within, per_map)
                i0 = part * bk + blk * LANES
                t = lax.shift_right_arithmetic(q0x + i0 - k0 - jb * LANES, LOG2_LANES)
                t = jnp.where(t < 0, DIFF_MASK_TILE, jnp.minimum(t, DIFF_CONST_TILE))
                tiles.append(u_ref[t, col])
            rows.append(jnp.concatenate(tiles, axis=1))
        s = s + jnp.concatenate(rows, axis=0)
        s_ref[:, lanes] = s
        smax_ref[:, lanes] = jnp.max(s, axis=0, keepdims=True)

    def accumulate(c, s_ref, smax_ref, first_block=0):
        lanes = slice(first_block * LANES, n_blocks * LANES)
        m_prev = m_sc[:, lanes]
        m_new = jnp.maximum(m_prev, smax_ref[:, lanes])
        m_sc[:, lanes] = m_new
        p = jnp.exp2(s_ref[:, lanes] - m_new).astype(BF16)
        v1 = jnp.concatenate([vt_ref[c], ones], axis=0)
        acc_sc[:, lanes] = (jnp.exp2(m_prev - m_new) * acc_sc[:, lanes]
                            + jnp.dot(v1, p, preferred_element_type=F32))

    @pl.when(i_q == 0)
    def _():
        scores(0, q2, q0, sa_sc, samax_sc)

    def body(i, carry):
        scores(2 * i + 1, q2, q0, sb_sc, sbmax_sc)
        accumulate(2 * i, sa_sc, samax_sc)
        scores(2 * i + 2, q2, q0, sa_sc, samax_sc)
        accumulate(2 * i + 1, sb_sc, sbmax_sc)
        return carry

    lax.fori_loop(0, first_diag // 2, body, 0)

    for dt in range(n_parts // 2):
        ja, jb = 2 * dt, 2 * dt + 1
        scores(first_diag + jb, q2, q0, sb_sc, sbmax_sc, first_block=jb * part_blocks)
        accumulate(first_diag + ja, sa_sc, samax_sc, first_block=ja * part_blocks)
        if jb + 1 < n_parts:
            scores(first_diag + jb + 1, q2, q0, sa_sc, samax_sc, first_block=(jb + 1) * part_blocks)
        else:
            scores(0, q2_next, q0 + bq, sa_sc, samax_sc)
        accumulate(first_diag + jb, sb_sc, sbmax_sc, first_block=jb * part_blocks)

    lam = (jnp.exp(jnp.sum(lq1_ref[...] * lk1_ref[...], keepdims=True))
           - jnp.exp(jnp.sum(lq2_ref[...] * lk2_ref[...], keepdims=True)) + lambda_init)
    accl = acc_sc[...]
    acc = accl[:LANES] / accl[LANES:LANES + 1]
    map0 = jnp.concatenate([acc[:, 2 * part * bk:(2 * part + 1) * bk] for part in range(n_parts)], axis=1)
    map1 = jnp.concatenate([acc[:, (2 * part + 1) * bk:(2 * part + 2) * bk] for part in range(n_parts)], axis=1)
    o = (map0 - lam * map1).T
    y = _rmsnorm_f32(o, sg_ref[...]) * (1.0 - lambda_init)
    o_ref[...] = y.astype(o_ref.dtype)


def _diff_attention(qk, vt, u, lq1, lk1, lq2, lk2, subln, lambda_init, batch, seq, *, bq, bk):
    nh = N_HEADS_B
    nkt = seq // bk
    nq = seq // bq
    vec = lambda a: a.reshape(1, -1)
    small = pl.BlockSpec((1, HEAD_DIM), lambda b, h, i: (0, 0))
    return pl.pallas_call(
        functools.partial(_diff_attn_kernel, bq=bq, bk=bk, lambda_init=lambda_init),
        grid=(batch, nh, seq // bq),
        in_specs=[small, small, small, small,
                  pl.BlockSpec((None, bq, LANES), lambda b, h, i: (b, i, h)),
                  pl.BlockSpec((None, bq, LANES), lambda b, h, i: (b, jnp.minimum(i + 1, nq - 1), h)),
                  pl.BlockSpec((None, seq, LANES), lambda b, h, i: (b, 0, nh + h)),
                  pl.BlockSpec((nkt, LANES, bk), lambda b, h, i: (b, h, 0)),
                  pl.BlockSpec((DIFF_N_TILES, 2, LANES, LANES), lambda b, h, i: (0, h, 0, 0)),
                  pl.BlockSpec((1, LANES), lambda b, h, i: (0, 0))],
        out_specs=pl.BlockSpec((None, bq, LANES), lambda b, h, i: (b, i, h)),
        out_shape=jax.ShapeDtypeStruct((batch, seq, D_MODEL), BF16),
        scratch_shapes=[pltpu.VMEM((1, 2 * bq), F32),
                        pltpu.VMEM((LANES + SUM_ROWS, 2 * bq), F32),
                        pltpu.VMEM((bk, 2 * bq), F32),
                        pltpu.VMEM((1, 2 * bq), F32),
                        pltpu.VMEM((bk, 2 * bq), F32),
                        pltpu.VMEM((1, 2 * bq), F32)],
        compiler_params=_cparams(("parallel", "parallel", "arbitrary")),
        name="diff_attn",
    )(vec(lq1), vec(lk1), vec(lq2), vec(lk2), qk, qk, qk, vt, u, vec(subln))


def kernel(x, p, rel_bias, a_w_qkv, a_w_o, b_w_qkv, b_w_o, b_lambda_q1, b_lambda_k1, b_lambda_q2, b_lambda_k2, b_subln, norm_mix, norm_mlp, w_ff1, w_ff2, norm_ple, w_ple_gate, w_ple_proj, final_norm):
    batch, seq, d = x.shape
    depth = p.shape[0]
    n = batch * seq
    h = x.reshape(n, d)
    p_all = p.reshape(depth, n, p.shape[-1])
    n_mixers = 2
    w1_all, w2_all = w_ff1.astype(BF16), w_ff2.astype(BF16)
    wg_all, wp_all = w_ple_gate.astype(BF16), w_ple_proj.astype(BF16)

    for i in range(depth):
        j = i // n_mixers
        if i % n_mixers == 0:
            w_a = a_w_qkv[j].astype(BF16)
            bias = _build_dilated_bias(rel_bias)
            outs, lses = [], []
            for g, (_, dilation) in enumerate(DIL_CONFIGS):
                qkv = _qkv_dilated(h, norm_mix[i], w_a, g, dilation, tm=max(1024, BLOCK * dilation))
                o_g, lse_g = _dilated_attention(qkv, bias, g, dilation, batch, seq)
                outs.append(o_g)
                lses.append(lse_g)
            h = _combine_proj(h, outs, lses, a_w_o[j].astype(BF16))
            proj = None
        else:
            lambda_init = 0.8 - 0.6 * math.exp(-0.3 * i)
            bq, bk = 2048, 512
            w_b = b_w_qkv[j].astype(BF16)
            qk, vt = _qkv_diff(h, norm_mix[i], w_b, bk=bk)
            u = _build_diff_bias(rel_bias)
            o = _diff_attention(qk.reshape(batch, seq, 2 * d), vt, u,
                                b_lambda_q1[j], b_lambda_k1[j], b_lambda_q2[j], b_lambda_k2[j],
                                b_subln[j], lambda_init, batch, seq, bq=bq, bk=bk)
            proj = (o.reshape(n, d), b_w_o[j].astype(BF16))
        h = _mlp_ple(h, p_all, i, norm_mlp[i], w1_all, w2_all, norm_ple[i], wg_all, wp_all,
                     final_norm, final_norm=(i == depth - 1), proj=proj)
    return h.reshape(batch, seq, d)
```

```python
import functools
import math

import jax
import jax.numpy as jnp
from jax import lax
from jax.experimental import pallas as pl
from jax.experimental.pallas import tpu as pltpu

F32 = jnp.float32
BF16 = jnp.bfloat16

D_MODEL = 1024
HEAD_DIM = 64
BLOCK = 128
QKV_CHUNK = 512
MLP_CHUNK = 512
DIL_RING = 6
DIL_QB = 8
DIL_CONFIGS = ((128, 1), (512, 4), (2048, 16))
N_GROUPS = len(DIL_CONFIGS)
N_HEADS_A = D_MODEL // HEAD_DIM
N_HEADS_B = D_MODEL // (2 * HEAD_DIM)
N_BUCKETS = 32
MAX_DISTANCE = 2048
N_BIAS_COLS = 16
EPS = 1e-6
NEG = -1e30
LANES = 128
SUBLANES = 8
LOG2_LANES = 7
QK_SCALE = HEAD_DIM ** -0.5
LOG2E = math.log2(math.e)
LN2 = math.log(2.0)

DIFF_CONST_TILE = (MAX_DISTANCE + LANES - 1) // LANES + 1
DIFF_MASK_TILE = DIFF_CONST_TILE + 1
DIFF_N_TILES = DIFF_MASK_TILE + 1
SUM_ROWS = 16

VMEM_LIMIT = 48 * 1024 * 1024
VMEM_LIMIT_PROJ = 56 * 1024 * 1024


def _cparams(sem, vmem_limit=VMEM_LIMIT):
    return pltpu.CompilerParams(dimension_semantics=sem, vmem_limit_bytes=vmem_limit)


def _rmsnorm_f32(x, g):
    ms = jnp.mean(x * x, axis=-1, keepdims=True)
    return x * lax.rsqrt(ms + EPS) * g


def _rel_bucket(dist):
    n = jnp.maximum(dist, 0)
    max_exact = N_BUCKETS // 2
    nf = jnp.maximum(n, 1).astype(F32)
    large = max_exact + (jnp.log(nf / max_exact) / math.log(MAX_DISTANCE / max_exact)
                         * (N_BUCKETS - max_exact)).astype(jnp.int32)
    large = jnp.minimum(large, N_BUCKETS - 1)
    return jnp.where(n < max_exact, n, large)


def _table_lookup(bucket, tab_ref, col):
    acc = jnp.zeros(bucket.shape, F32)
    for b in range(N_BUCKETS):
        acc = jnp.where(bucket == b, tab_ref[b, col], acc)
    return acc


def _dilated_bias_kernel(tab_ref, o_ref):
    g = pl.program_id(0)
    dilation = jnp.where(g == 0, DIL_CONFIGS[0][1],
                         jnp.where(g == 1, DIL_CONFIGS[1][1], DIL_CONFIGS[2][1]))
    kj = lax.broadcasted_iota(jnp.int32, (2 * BLOCK, BLOCK), 0)
    qi = lax.broadcasted_iota(jnp.int32, (2 * BLOCK, BLOCK), 1)
    sub = qi + BLOCK - kj
    band = (sub >= 0) & (sub <= BLOCK)
    band_first = band & (kj >= BLOCK)
    bucket = _rel_bucket(sub * dilation)
    for c in range(N_BIAS_COLS):
        lanes = slice((c % 2) * BLOCK, (c % 2 + 1) * BLOCK)
        bias = _table_lookup(bucket, tab_ref, c) * LOG2E
        o_ref[0, c // 2, :, lanes] = jnp.where(band, bias, NEG)
        o_ref[1, c // 2, :, lanes] = jnp.where(band_first, bias, NEG)


def _build_dilated_bias(rel_bias):
    return pl.pallas_call(
        _dilated_bias_kernel,
        grid=(N_GROUPS,),
        in_specs=[pl.BlockSpec(memory_space=pltpu.SMEM)],
        out_specs=pl.BlockSpec((None, 2, N_HEADS_A // 2, 2 * BLOCK, 2 * BLOCK),
                               lambda g: (g, 0, 0, 0, 0)),
        out_shape=jax.ShapeDtypeStruct((N_GROUPS, 2, N_HEADS_A // 2, 2 * BLOCK, 2 * BLOCK), F32),
        compiler_params=_cparams(("arbitrary",)),
        name="dilated_bias",
    )(rel_bias)


def _diff_bias_kernel(tab_ref, o_ref):
    t = pl.program_id(0)
    kj = lax.broadcasted_iota(jnp.int32, (LANES, LANES), 0)
    qi = lax.broadcasted_iota(jnp.int32, (LANES, LANES), 1)
    dist = t * LANES + qi - kj
    masked = (dist < 0) | (t == DIFF_MASK_TILE)
    bucket = _rel_bucket(dist)
    init = jnp.where(masked, NEG, 0.0)
    for c in range(N_BIAS_COLS):
        o_ref[c] = init
    live = jnp.where(masked, -1, bucket)

    def body(b, carry):
        hit = live == b
        for c in range(N_BIAS_COLS):
            o_ref[c] = jnp.where(hit, tab_ref[b, c] * LOG2E, o_ref[c])
        return carry

    lax.fori_loop(jnp.min(bucket), jnp.max(bucket) + 1, body, 0)


def _build_diff_bias(rel_bias):
    return pl.pallas_call(
        _diff_bias_kernel,
        grid=(DIFF_N_TILES,),
        in_specs=[pl.BlockSpec(memory_space=pltpu.SMEM)],
        out_specs=pl.BlockSpec((None, N_BIAS_COLS, LANES, LANES), lambda t: (t, 0, 0, 0)),
        out_shape=jax.ShapeDtypeStruct((DIFF_N_TILES, N_BIAS_COLS, LANES, LANES), F32),
        compiler_params=_cparams(("arbitrary",)),
        name="diff_bias",
    )(rel_bias)


def _qkv_dilated_kernel(*refs, dilation, n_slabs, chunk, whole, pre_stride):
    n_x = n_slabs or 1
    x_refs = refs[:n_x]
    g_ref, w_ref, o_ref, xn_sc = refs[n_x:n_x + 4]
    tm, d_model = xn_sc.shape
    rows = tm // dilation
    if pre_stride:
        pre_sc = refs[n_x + 4]
        pre_rows = tm // pre_stride
        outer = dilation // pre_stride

    def pre_regroup():
        for s, x_ref in enumerate(x_refs):
            for b in range(pre_stride):
                pre_sc[s, b * pre_rows:(b + 1) * pre_rows, :] = x_ref[pl.ds(b, pre_rows, stride=pre_stride), :]

    def normalise(r, a, b):
        if n_slabs is None:
            xs = [x_refs[0][a:b, :]]
        elif pre_stride:
            start = (r % pre_stride) * pre_rows + r // pre_stride + (a - r * rows) * outer
            xs = [pre_sc[s, pl.ds(start, b - a, stride=outer), :] for s in range(n_slabs)]
        else:
            xs = [x_ref[pl.ds(r + (a - r * rows) * dilation, b - a, stride=dilation), :]
                  for x_ref in x_refs]
        slab_w = d_model // len(xs)
        sq = xs[0] * xs[0]
        for x in xs[1:]:
            sq = sq + x * x
        scale = lax.rsqrt(jnp.sum(sq, axis=-1, keepdims=True) * (1.0 / d_model) + EPS)
        for s, x in enumerate(xs):
            cols = slice(s * slab_w, (s + 1) * slab_w)
            xn_sc[a:b, cols] = (x * scale * g_ref[:, cols]).astype(BF16)

    def store(c, y, lo, r, a, b, col0):
        if c < 2:
            val = y[a - lo:b - lo]
            if c == 0:
                val = val * (QK_SCALE * LOG2E)
            o_ref[a - r * rows:b - r * rows, col0:col0 + d_model] = val.astype(o_ref.dtype)
            return
        for blk in range((b - a) // BLOCK):
            y0 = a - lo + blk * BLOCK
            s0 = a - r * rows + blk * BLOCK
            for hp in range(d_model // LANES):
                tile = y[y0:y0 + BLOCK, hp * LANES:(hp + 1) * LANES]
                o_ref[s0:s0 + BLOCK, col0 + hp * LANES:col0 + (hp + 1) * LANES] = (
                    tile.T.astype(o_ref.dtype))

    def run(comps, with_norm):
        if with_norm and pre_stride:
            pre_regroup()
        for ch in range(tm // chunk):
            lo, hi = ch * chunk, (ch + 1) * chunk
            pieces = [(r, max(lo, r * rows), min(hi, (r + 1) * rows)) for r in range(dilation)
                      if max(lo, r * rows) < min(hi, (r + 1) * rows)]
            if with_norm:
                for r, a, b in pieces:
                    normalise(r, a, b)
            xn = xn_sc[lo:hi, :]
            for c in comps:
                w = w_ref[:, c * d_model:(c + 1) * d_model] if whole else w_ref[...]
                y = jnp.dot(xn, w, preferred_element_type=F32)
                for r, a, b in pieces:
                    store(c, y, lo, r, a, b, ((c * dilation if whole else 0) + r) * d_model)

    if whole:
        run((0, 1, 2), True)
    else:
        j = pl.program_id(1)
        for c in range(3):
            pl.when(j == c)(functools.partial(run, (c,), c == 0))


def _qkv_dilated(x, g, w, group, dilation, *, tm):
    n, d = x.shape
    assert w.shape[0] == d and w.shape[1] % (3 * d) == 0
    rows = tm // dilation
    assert tm % dilation == 0 and rows % BLOCK == 0
    if dilation == 1:
        n_slabs = None
        x_in = [x]
        x_specs = [pl.BlockSpec((tm, d), lambda i, j: (i, 0))]
    else:
        n_slabs = d // LANES
        x_in = [x] * n_slabs
        x_specs = [pl.BlockSpec((tm, LANES), functools.partial(lambda i, j, s: (i, s), s=s))
                   for s in range(n_slabs)]
    pre_stride = 4 if dilation % 8 == 0 else None
    scratch = [pltpu.VMEM((tm, d), BF16)]
    if pre_stride:
        scratch.append(pltpu.VMEM((n_slabs, tm, LANES), F32))
    out_shape = jax.ShapeDtypeStruct((n // dilation, 3 * dilation * d), BF16)
    whole_bytes = 2 * (tm * 3 * d * 2) + 2 * (d * 3 * d * 2) + 2 * (tm * d * 4) + 4 * QKV_CHUNK * d * 4
    assert tm % QKV_CHUNK == 0 and (QKV_CHUNK % rows == 0 or rows % QKV_CHUNK == 0)
    if whole_bytes <= VMEM_LIMIT:
        return pl.pallas_call(
            functools.partial(_qkv_dilated_kernel, dilation=dilation, n_slabs=n_slabs,
                              chunk=QKV_CHUNK, whole=True, pre_stride=pre_stride),
            grid=(n // tm, 1),
            in_specs=x_specs + [pl.BlockSpec((1, d), lambda i, j: (0, 0)),
                                pl.BlockSpec((d, 3 * d), lambda i, j: (0, group))],
            out_specs=pl.BlockSpec((rows, 3 * dilation * d), lambda i, j: (i, 0)),
            out_shape=out_shape,
            scratch_shapes=scratch,
            compiler_params=_cparams(("parallel", "arbitrary")),
            name=f"qkv_d{dilation}",
        )(*x_in, g.reshape(1, d), w)
    return pl.pallas_call(
        functools.partial(_qkv_dilated_kernel, dilation=dilation, n_slabs=n_slabs,
                          chunk=QKV_CHUNK, whole=False, pre_stride=pre_stride),
        grid=(n // tm, 3),
        in_specs=x_specs + [pl.BlockSpec((1, d), lambda i, j: (0, 0)),
                            pl.BlockSpec((d, d), lambda i, j: (0, 3 * group + j))],
        out_specs=pl.BlockSpec((rows, dilation * d), lambda i, j: (i, j)),
        out_shape=out_shape,
        scratch_shapes=scratch,
        compiler_params=_cparams(("parallel", "arbitrary")),
        name=f"qkv_d{dilation}",
    )(*x_in, g.reshape(1, d), w)


def _qkv_diff_kernel(x_ref, g_ref, w_ref, qk_ref, vt_ref, *, bk):
    tm, d = x_ref.shape
    for c in range(tm // bk):
        rows = slice(c * bk, (c + 1) * bk)
        xn = _rmsnorm_f32(x_ref[rows, :], g_ref[...]).astype(BF16)
        q = jnp.dot(xn, w_ref[:, :d], preferred_element_type=F32) * (QK_SCALE * LOG2E)
        qk_ref[rows, :d] = q.astype(qk_ref.dtype)
        qk_ref[rows, d:] = jnp.dot(xn, w_ref[:, d:2 * d], preferred_element_type=F32).astype(qk_ref.dtype)
        v = jnp.dot(xn, w_ref[:, 2 * d:], preferred_element_type=F32)
        for hp in range(d // LANES):
            cols = slice(hp * LANES, (hp + 1) * LANES)
            vt_ref[c, cols, :] = v[:, cols].T.astype(vt_ref.dtype)


def _qkv_diff(x, g, w_qkv, *, bk, tm=1024):
    n, d = x.shape
    return pl.pallas_call(
        functools.partial(_qkv_diff_kernel, bk=bk),
        grid=(n // tm,),
        in_specs=[pl.BlockSpec((tm, d), lambda i: (i, 0)),
                  pl.BlockSpec((1, d), lambda i: (0, 0)),
                  pl.BlockSpec((d, 3 * d), lambda i: (0, 0))],
        out_specs=[pl.BlockSpec((tm, 2 * d), lambda i: (i, 0)),
                   pl.BlockSpec((tm // bk, d, bk), lambda i: (i, 0, 0))],
        out_shape=[jax.ShapeDtypeStruct((n, 2 * d), BF16),
                   jax.ShapeDtypeStruct((n // bk, d, bk), BF16)],
        compiler_params=_cparams(("parallel",)),
        name="qkv_diff",
    )(x, g.reshape(1, d), w_qkv)


def _dilated_kernel(q_ref, kp_ref, kc_ref, vtp_ref, vtc_ref, bias_ref, o_ref, lse_ref,
                    s_sc, m_sc, pv_sc):
    first_step = (pl.program_id(2) == 0).astype(jnp.int32)
    lane = lax.broadcasted_iota(jnp.int32, (BLOCK, LANES), 1)
    row = lax.broadcasted_iota(jnp.int32, (BLOCK, LANES), 0)
    head_row = lax.broadcasted_iota(jnp.int32, (N_HEADS_A, BLOCK), 0)
    lo_row = row < HEAD_DIM
    mask_lo = jnp.where(lane < HEAD_DIM, 1.0, 0.0).astype(BF16)
    mask_hi = jnp.where(lane < HEAD_DIM, 0.0, 1.0).astype(BF16)
    nt = (((1,), (1,)), ((), ()))
    ones = jnp.ones((SUM_ROWS, 2 * BLOCK), BF16)
    n_pairs = N_HEADS_A // 2
    n_qb = q_ref.shape[0] // BLOCK
    items = [(qb, hp) for qb in range(n_qb) for hp in range(n_pairs)]

    def rows(qb):
        return slice(qb * BLOCK, (qb + 1) * BLOCK)

    def prev_cur(prev_ref, cur_ref, qb, sl):
        prev = prev_ref[:, sl] if qb == 0 else cur_ref[rows(qb - 1), sl]
        return prev, cur_ref[rows(qb), sl]

    def scores(it):
        qb, hp = items[it]
        sl = slice(hp * LANES, (hp + 1) * LANES)
        q = q_ref[rows(qb), sl]
        q2 = jnp.concatenate([q * mask_lo, q * mask_hi], axis=0)
        k = jnp.concatenate(prev_cur(kp_ref, kc_ref, qb, sl), axis=0)
        s = lax.dot_general(k, q2, nt, preferred_element_type=F32)
        s = s + bias_ref[first_step if qb == 0 else 0, hp]
        s_sc[it % DIL_RING] = s
        m_sc[it] = jnp.max(s, axis=0, keepdims=True)

    def value_product(it):
        qb, hp = items[it]
        sl = slice(hp * LANES, (hp + 1) * LANES)
        p = jnp.exp2(s_sc[it % DIL_RING] - m_sc[it]).astype(BF16)
        vt = jnp.concatenate(prev_cur(vtp_ref, vtc_ref, qb, sl), axis=1)
        pv_sc[it] = jnp.dot(jnp.concatenate([vt, ones], axis=0), p,
                            preferred_element_type=F32)

    for it in range(len(items) + DIL_RING - 1):
        if it < len(items):
            scores(it)
        if it >= DIL_RING - 1:
            value_product(it - (DIL_RING - 1))
    for qb in range(n_qb):
        lse_t = jnp.zeros((N_HEADS_A, BLOCK), F32)
        for hp in range(n_pairs):
            it = qb * n_pairs + hp
            l = pv_sc[it, LANES:LANES + 1]
            acc = pv_sc[it, :LANES] / l
            o_t = jnp.where(lo_row, acc[:, :BLOCK], acc[:, BLOCK:])
            o_ref[rows(qb), hp * LANES:(hp + 1) * LANES] = o_t.T.astype(o_ref.dtype)
            lse = (m_sc[it] + jnp.log2(l)) * LN2
            lse_t = jnp.where(head_row == 2 * hp, lse[:, :BLOCK], lse_t)
            lse_t = jnp.where(head_row == 2 * hp + 1, lse[:, BLOCK:], lse_t)
        lse_full = jnp.concatenate([lse_t, jnp.zeros((BLOCK - N_HEADS_A, BLOCK), F32)], axis=0)
        lse_ref[rows(qb), :] = lse_full.T


def _dilated_attention(qkv, bias, group, dilation, batch, seq):
    assert seq % (dilation * BLOCK) == 0
    sub_len = seq // dilation
    n_qb = min(DIL_QB, sub_len // BLOCK)
    assert sub_len % (BLOCK * n_qb) == 0
    steps = sub_len // (BLOCK * n_qb)
    qkv_v = qkv.reshape(batch, sub_len, 3 * dilation * D_MODEL)

    def col(c):
        return lambda b, r, n: (b, n, c * dilation + r)

    def col_prev(c):
        return lambda b, r, n: (b, jnp.maximum(n * n_qb - 1, 0), c * dilation + r)

    blk = (None, n_qb * BLOCK, D_MODEL)
    blk_prev = (None, BLOCK, D_MODEL)
    n_items = n_qb * N_HEADS_A // 2
    o, lse = pl.pallas_call(
        _dilated_kernel,
        grid=(batch, dilation, steps),
        in_specs=[pl.BlockSpec(blk, col(0)),
                  pl.BlockSpec(blk_prev, col_prev(1)),
                  pl.BlockSpec(blk, col(1)),
                  pl.BlockSpec(blk_prev, col_prev(2)),
                  pl.BlockSpec(blk, col(2)),
                  pl.BlockSpec((None, 2, N_HEADS_A // 2, 2 * BLOCK, 2 * BLOCK),
                               lambda b, r, n: (group, 0, 0, 0, 0))],
        out_specs=[pl.BlockSpec(blk, lambda b, r, n: (b, n, r)),
                   pl.BlockSpec((None, n_qb * BLOCK, LANES), lambda b, r, n: (b, n, r))],
        out_shape=[jax.ShapeDtypeStruct((batch, sub_len, dilation * D_MODEL), BF16),
                   jax.ShapeDtypeStruct((batch, sub_len, dilation * LANES), F32)],
        scratch_shapes=[pltpu.VMEM((DIL_RING, 2 * BLOCK, 2 * BLOCK), F32),
                        pltpu.VMEM((n_items, 1, 2 * BLOCK), F32),
                        pltpu.VMEM((n_items, LANES + SUM_ROWS, 2 * BLOCK), F32)],
        compiler_params=_cparams(("parallel", "parallel", "arbitrary")),
        name=f"dilated_attn_d{dilation}",
    )(qkv_v, qkv_v, qkv_v, qkv_v, qkv_v, bias)
    return (o.reshape(batch * sub_len, dilation * D_MODEL),
            lse.reshape(batch * sub_len, dilation * LANES))


def _combine_proj_kernel(h_ref, o0_ref, o1_ref, o2_ref, l0_ref, l1_ref, l2_ref,
                         e_ref, w_ref, out_ref, lse_sc, o_sc):
    o_refs = [o0_ref, o1_ref, o2_ref]
    l_refs = [l0_ref, l1_ref, l2_ref]
    tm, d = h_ref.shape
    n_slabs = d // LANES
    for g, (_, dilation) in enumerate(DIL_CONFIGS):
        rows = tm // dilation
        for r in range(dilation):
            dst = pl.ds(r, rows, stride=dilation) if dilation > 1 else slice(None)
            lse_sc[g, dst, :] = l_refs[g][:, r * LANES:(r + 1) * LANES]
            for c in range(n_slabs):
                o_sc[g, c, dst, :] = o_refs[g][:, r * d + c * LANES:r * d + (c + 1) * LANES].astype(F32)
    lses = [lse_sc[g] for g in range(N_GROUPS)]
    mx = jnp.maximum(jnp.maximum(lses[0], lses[1]), lses[2])
    ws = [jnp.exp(l - mx) for l in lses]
    tot = ws[0] + ws[1] + ws[2]
    head_lane = lax.broadcasted_iota(jnp.int32, (tm, LANES), 1) < N_HEADS_A
    packed = None
    for g in range(N_GROUPS):
        a = jnp.where(head_lane, ws[g] / tot, 0.0)
        a_hi = a.astype(BF16).astype(F32)
        for part, piece in enumerate((a_hi, a - a_hi)):
            shift = (part * N_GROUPS + g) * N_HEADS_A
            moved = piece if shift == 0 else pltpu.roll(piece, shift, axis=1)
            packed = moved if packed is None else packed + moved
    ae_all = jnp.dot(packed.astype(BF16), e_ref[...], preferred_element_type=F32)
    aes = [ae_all[:, g * d:(g + 1) * d] for g in range(N_GROUPS)]
    slabs = []
    for c in range(n_slabs):
        cols = slice(c * LANES, (c + 1) * LANES)
        slabs.append(aes[0][:, cols] * o_sc[0, c] + aes[1][:, cols] * o_sc[1, c]
                     + aes[2][:, cols] * o_sc[2, c])
    o = jnp.concatenate(slabs, axis=1).astype(BF16)
    out_ref[...] = h_ref[...] + jnp.dot(o, w_ref[...], preferred_element_type=F32)


def _combine_proj(h, outs, lses, w, *, tm=512):
    n, d = h.shape
    rows = jnp.arange(LANES, dtype=jnp.int32)
    cols = jnp.arange(N_GROUPS * d, dtype=jnp.int32)
    row_group = (rows // N_HEADS_A) % N_GROUPS
    row_head = rows % N_HEADS_A
    row_used = rows < 2 * N_GROUPS * N_HEADS_A
    expand = (row_used[:, None] & (row_group[:, None] == (cols // d)[None, :])
              & (row_head[:, None] == ((cols % d) // HEAD_DIM)[None, :])).astype(BF16)
    row = lambda i: (i, 0)
    full = lambda i: (0, 0)
    dils = [dilation for _, dilation in DIL_CONFIGS]
    assert all(tm % (16 * dilation) == 0 for dilation in dils)
    return pl.pallas_call(
        _combine_proj_kernel,
        grid=(n // tm,),
        in_specs=[pl.BlockSpec((tm, d), row)]
                 + [pl.BlockSpec((tm // dilation, dilation * d), row) for dilation in dils]
                 + [pl.BlockSpec((tm // dilation, dilation * LANES), row) for dilation in dils]
                 + [pl.BlockSpec((LANES, N_GROUPS * d), full), pl.BlockSpec((d, d), full)],
        out_specs=pl.BlockSpec((tm, d), row),
        out_shape=jax.ShapeDtypeStruct((n, d), F32),
        scratch_shapes=[pltpu.VMEM((N_GROUPS, tm, LANES), F32),
                        pltpu.VMEM((N_GROUPS, d // LANES, tm, LANES), F32)],
        compiler_params=_cparams(("parallel",)),
        name="combine_proj",
    )(h, *outs, *lses, expand, w)


def _mlp_ple_kernel(*refs, final_norm, with_proj, n_steps, chunk):
    if with_proj:
        h_ref, o_ref, wo_ref = refs[:3]
        refs = refs[3:]
    else:
        h_ref = refs[0]
        refs = refs[1:]
    p_ref, g_ref, w1_ref, w2_ref, gp_ref, wg_ref, wp_ref, fg_ref, out_ref, xn_ref, acc_ref = refs
    f = pl.program_id(1)
    tm = h_ref.shape[0]

    def step(first, last):
        for ch in range(tm // chunk):
            rows = slice(ch * chunk, (ch + 1) * chunk)
            if first:
                if with_proj:
                    h0 = h_ref[rows, :] + jnp.dot(o_ref[rows, :], wo_ref[...],
                                                  preferred_element_type=F32)
                    out_ref[rows, :] = h0
                else:
                    h0 = h_ref[rows, :]
                xn_ref[rows, :] = _rmsnorm_f32(h0, g_ref[...]).astype(BF16)
            a = jnp.dot(xn_ref[rows, :], w1_ref[...], preferred_element_type=F32)
            a = jnp.maximum(a, 0.0)
            a = (a * a).astype(BF16)
            acc = jnp.dot(a, w2_ref[...], preferred_element_type=F32)
            if not first:
                acc = acc_ref[rows, :] + acc
            if not last:
                acc_ref[rows, :] = acc
                continue
            x = (out_ref[rows, :] if with_proj else h_ref[rows, :]) + acc
            xn = _rmsnorm_f32(x, gp_ref[...]).astype(BF16)
            gate = jax.nn.sigmoid(jnp.dot(xn, wg_ref[...], preferred_element_type=F32))
            proj = jnp.dot(p_ref[rows, :].astype(BF16), wp_ref[...], preferred_element_type=F32)
            y = x + gate * proj
            if final_norm:
                y = _rmsnorm_f32(y, fg_ref[...])
            out_ref[rows, :] = y

    pl.when(f == 0)(functools.partial(step, True, n_steps == 1))
    if n_steps > 2:
        pl.when((f > 0) & (f < n_steps - 1))(functools.partial(step, False, False))
    if n_steps > 1:
        pl.when(f == n_steps - 1)(functools.partial(step, False, True))


def _mlp_ple(h, p_all, layer, g, w1, w2, gp, wg, wp, fg, *, final_norm, proj=None, tm=1024, tf=1024):
    n, d = h.shape
    dff = w1.shape[2]
    pd = p_all.shape[-1]
    vec = lambda i, f: (0, 0)
    lvec = lambda i, f: (layer, 0, 0)
    row = lambda i, f: (i, 0)
    lead_specs = [pl.BlockSpec((tm, d), row)]
    lead_args = [h]
    if proj is not None:
        o, wo = proj
        lead_specs += [pl.BlockSpec((tm, o.shape[1]), row), pl.BlockSpec(wo.shape, vec)]
        lead_args += [o, wo]
    return pl.pallas_call(
        functools.partial(_mlp_ple_kernel, final_norm=final_norm, with_proj=proj is not None,
                          n_steps=dff // tf, chunk=MLP_CHUNK),
        grid=(n // tm, dff // tf),
        in_specs=lead_specs + [
                  pl.BlockSpec((None, tm, pd), lambda i, f: (layer, i, 0)),
                  pl.BlockSpec((1, d), vec),
                  pl.BlockSpec((None, d, tf), lambda i, f: (layer, 0, f)),
                  pl.BlockSpec((None, tf, d), lambda i, f: (layer, f, 0)),
                  pl.BlockSpec((1, d), vec),
                  pl.BlockSpec((None, d, d), lvec),
                  pl.BlockSpec((None, pd, d), lvec),
                  pl.BlockSpec((1, d), vec)],
        out_specs=pl.BlockSpec((tm, d), lambda i, f: (i, 0)),
        out_shape=jax.ShapeDtypeStruct((n, d), F32),
        scratch_shapes=[pltpu.VMEM((tm, d), BF16), pltpu.VMEM((tm, d), F32)],
        compiler_params=_cparams(("parallel", "arbitrary"),
                                 vmem_limit=VMEM_LIMIT_PROJ if proj is not None else VMEM_LIMIT),
        name="mlp_ple",
    )(*lead_args, p_all, g.reshape(1, d), w1, w2, gp.reshape(1, d), wg, wp, fg.reshape(1, d))


def _diff_attn_kernel(lq1_ref, lk1_ref, lq2_ref, lk2_ref, q_ref, qn_ref, k_ref, vt_ref, u_ref, sg_ref,
                      o_ref, m_sc, acc_sc, sa_sc, samax_sc, sb_sc, sbmax_sc, *, bq, bk, lambda_init):
    n_parts = bq // bk
    assert bq == n_parts * bk and n_parts % 2 == 0
    i_q = pl.program_id(2)
    q0 = i_q * bq
    nt = (((1,), (1,)), ((), ()))
    lane = lax.broadcasted_iota(jnp.int32, (bq, LANES), 1)

    def query_columns(ref):
        qf = ref[...].astype(F32)
        a = jnp.where(lane < HEAD_DIM, qf, 0.0)
        b = jnp.where(lane >= HEAD_DIM, qf, 0.0)
        pieces = []
        for part in range(n_parts):
            pieces += [a[part * bk:(part + 1) * bk], b[part * bk:(part + 1) * bk]]
        return jnp.concatenate(pieces, axis=0).astype(BF16)

    q2 = query_columns(q_ref)
    q2_next = query_columns(qn_ref)

    m_sc[...] = jnp.full(m_sc.shape, NEG, F32)
    acc_sc[...] = jnp.zeros(acc_sc.shape, F32)

    per_map = bk // LANES
    part_blocks = 2 * per_map
    n_blocks = n_parts * part_blocks
    first_diag = q0 // bk
    ones = jnp.ones((SUM_ROWS, bk), BF16)

    far_bias = jnp.concatenate(
        [u_ref[DIFF_CONST_TILE, (ib % part_blocks) // per_map][0:1, :] for ib in range(n_blocks)],
        axis=1)

    def scores(c, q2x, q0x, s_ref, smax_ref, first_block=0, far=False):
        k0 = pl.multiple_of(c * bk, bk)
        lanes = slice(first_block * LANES, n_blocks * LANES)
        s = lax.dot_general(k_ref[pl.ds(k0, bk), :], q2x[lanes], nt,
                            preferred_element_type=F32)
        if far:
            s_ref[:, lanes] = s
            smax_ref[:, lanes] = jnp.max(s, axis=0, keepdims=True) + far_bias[:, lanes]
            return
        rows = []
        for jb in range(bk // LANES):
            tiles = []
            for ib in range(first_block, n_blocks):
                part, within = divmod(ib, part_blocks)
                col, blk = divmod(within, per_map)
                i0 = part * bk + blk * LANES
                t = lax.shift_right_arithmetic(q0x + i0 - k0 - jb * LANES, LOG2_LANES)
                t = jnp.where(t < 0, DIFF_MASK_TILE, jnp.minimum(t, DIFF_CONST_TILE))
                tiles.append(u_ref[t, col])
            rows.append(jnp.concatenate(tiles, axis=1))
        s = s + jnp.concatenate(rows, axis=0)
        s_ref[:, lanes] = s
        smax_ref[:, lanes] = jnp.max(s, axis=0, keepdims=True)

    def accumulate(c, s_ref, smax_ref, first_block=0, far=False):
        lanes = slice(first_block * LANES, n_blocks * LANES)
        m_prev = m_sc[:, lanes]
        m_new = jnp.maximum(m_prev, smax_ref[:, lanes])
        m_sc[:, lanes] = m_new
        shift = m_new - far_bias[:, lanes] if far else m_new
        p = jnp.exp2(s_ref[:, lanes] - shift).astype(BF16)
        v1 = jnp.concatenate([vt_ref[c], ones], axis=0)
        acc_sc[:, lanes] = (jnp.exp2(m_prev - m_new) * acc_sc[:, lanes]
                            + jnp.dot(v1, p, preferred_element_type=F32))

    @pl.when(i_q == 0)
    def _():
        scores(0, q2, q0, sa_sc, samax_sc)

    def body(far_b, i, carry):
        scores(2 * i + 1, q2, q0, sb_sc, sbmax_sc, far=far_b)
        accumulate(2 * i, sa_sc, samax_sc)
        scores(2 * i + 2, q2, q0, sa_sc, samax_sc)
        accumulate(2 * i + 1, sb_sc, sbmax_sc, far=far_b)
        return carry

    far_chunks = jnp.maximum(q0 - (MAX_DISTANCE - 1), 0) // bk
    far_trips = jnp.minimum(far_chunks // 2, first_diag // 2)
    lax.fori_loop(0, far_trips, functools.partial(body, True), 0)
    lax.fori_loop(far_trips, first_diag // 2, functools.partial(body, False), 0)

    for dt in range(n_parts // 2):
        ja, jb = 2 * dt, 2 * dt + 1
        scores(first_diag + jb, q2, q0, sb_sc, sbmax_sc, first_block=jb * part_blocks)
        accumulate(first_diag + ja, sa_sc, samax_sc, first_block=ja * part_blocks)
        if jb + 1 < n_parts:
            scores(first_diag + jb + 1, q2, q0, sa_sc, samax_sc, first_block=(jb + 1) * part_blocks)
        else:
            scores(0, q2_next, q0 + bq, sa_sc, samax_sc)
        accumulate(first_diag + jb, sb_sc, sbmax_sc, first_block=jb * part_blocks)

    lam = (jnp.exp(jnp.sum(lq1_ref[...] * lk1_ref[...], keepdims=True))
           - jnp.exp(jnp.sum(lq2_ref[...] * lk2_ref[...], keepdims=True)) + lambda_init)
    accl = acc_sc[...]
    acc = accl[:LANES] / accl[LANES:LANES + 1]
    map0 = jnp.concatenate([acc[:, 2 * part * bk:(2 * part + 1) * bk] for part in range(n_parts)], axis=1)
    map1 = jnp.concatenate([acc[:, (2 * part + 1) * bk:(2 * part + 2) * bk] for part in range(n_parts)], axis=1)
    o = (map0 - lam * map1).T
    y = _rmsnorm_f32(o, sg_ref[...]) * (1.0 - lambda_init)
    o_ref[...] = y.astype(o_ref.dtype)


def _diff_attention(qk, vt, u, lq1, lk1, lq2, lk2, subln, lambda_init, batch, seq, *, bq, bk):
    nh = N_HEADS_B
    nkt = seq // bk
    nq = seq // bq
    vec = lambda a: a.reshape(1, -1)
    small = pl.BlockSpec((1, HEAD_DIM), lambda b, h, i: (0, 0))
    return pl.pallas_call(
        functools.partial(_diff_attn_kernel, bq=bq, bk=bk, lambda_init=lambda_init),
        grid=(batch, nh, seq // bq),
        in_specs=[small, small, small, small,
                  pl.BlockSpec((None, bq, LANES), lambda b, h, i: (b, i, h)),
                  pl.BlockSpec((None, bq, LANES), lambda b, h, i: (b, jnp.minimum(i + 1, nq - 1), h)),
                  pl.BlockSpec((None, seq, LANES), lambda b, h, i: (b, 0, nh + h)),
                  pl.BlockSpec((nkt, LANES, bk), lambda b, h, i: (b, h, 0)),
                  pl.BlockSpec((DIFF_N_TILES, 2, LANES, LANES), lambda b, h, i: (0, h, 0, 0)),
                  pl.BlockSpec((1, LANES), lambda b, h, i: (0, 0))],
        out_specs=pl.BlockSpec((None, bq, LANES), lambda b, h, i: (b, i, h)),
        out_shape=jax.ShapeDtypeStruct((batch, seq, D_MODEL), BF16),
        scratch_shapes=[pltpu.VMEM((1, 2 * bq), F32),
                        pltpu.VMEM((LANES + SUM_ROWS, 2 * bq), F32),
                        pltpu.VMEM((bk, 2 * bq), F32),
                        pltpu.VMEM((1, 2 * bq), F32),
                        pltpu.VMEM((bk, 2 * bq), F32),
                        pltpu.VMEM((1, 2 * bq), F32)],
        compiler_params=_cparams(("parallel", "parallel", "arbitrary")),
        name="diff_attn",
    )(vec(lq1), vec(lk1), vec(lq2), vec(lk2), qk, qk, qk, vt, u, vec(subln))


def kernel(x, p, rel_bias, a_w_qkv, a_w_o, b_w_qkv, b_w_o, b_lambda_q1, b_lambda_k1, b_lambda_q2, b_lambda_k2, b_subln, norm_mix, norm_mlp, w_ff1, w_ff2, norm_ple, w_ple_gate, w_ple_proj, final_norm):
    batch, seq, d = x.shape
    depth = p.shape[0]
    n = batch * seq
    h = x.reshape(n, d)
    p_all = p.reshape(depth, n, p.shape[-1])
    n_mixers = 2
    w1_all, w2_all = w_ff1.astype(BF16), w_ff2.astype(BF16)
    wg_all, wp_all = w_ple_gate.astype(BF16), w_ple_proj.astype(BF16)

    for i in range(depth):
        j = i // n_mixers
        if i % n_mixers == 0:
            w_a = a_w_qkv[j].astype(BF16)
            bias = _build_dilated_bias(rel_bias)
            outs, lses = [], []
            for g, (_, dilation) in enumerate(DIL_CONFIGS):
                qkv = _qkv_dilated(h, norm_mix[i], w_a, g, dilation, tm=max(1024, BLOCK * dilation))
                o_g, lse_g = _dilated_attention(qkv, bias, g, dilation, batch, seq)
                outs.append(o_g)
                lses.append(lse_g)
            h = _combine_proj(h, outs, lses, a_w_o[j].astype(BF16))
            proj = None
        else:
            lambda_init = 0.8 - 0.6 * math.exp(-0.3 * i)
            bq, bk = 2048, 512
            w_b = b_w_qkv[j].astype(BF16)
            qk, vt = _qkv_diff(h, norm_mix[i], w_b, bk=bk)
            u = _build_diff_bias(rel_bias)
            o = _diff_attention(qk.reshape(batch, seq, 2 * d), vt, u,
                                b_lambda_q1[j], b_lambda_k1[j], b_lambda_q2[j], b_lambda_k2[j],
                                b_subln[j], lambda_init, batch, seq, bq=bq, bk=bk)
            proj = (o.reshape(n, d), b_w_o[j].astype(BF16))
        h = _mlp_ple(h, p_all, i, norm_mlp[i], w1_all, w2_all, norm_ple[i], wg_all, wp_all,
                     final_norm, final_norm=(i == depth - 1), proj=proj)
    return h.reshape(batch, seq, d)
```

```python
import functools
import math

import jax
import jax.numpy as jnp
from jax import lax
from jax.experimental import pallas as pl
from jax.experimental.pallas import tpu as pltpu

F32 = jnp.float32
BF16 = jnp.bfloat16

D_MODEL = 1024
HEAD_DIM = 64
BLOCK = 128
QKV_CHUNK = 512
MLP_CHUNK = 512
DIL_RING = 6
DIL_QB = 8
DIL_CONFIGS = ((128, 1), (512, 4), (2048, 16))
N_GROUPS = len(DIL_CONFIGS)
N_HEADS_A = D_MODEL // HEAD_DIM
N_HEADS_B = D_MODEL // (2 * HEAD_DIM)
N_BUCKETS = 32
MAX_DISTANCE = 2048
N_BIAS_COLS = 16
EPS = 1e-6
NEG = -1e30
LANES = 128
SUBLANES = 8
LOG2_LANES = 7
QK_SCALE = HEAD_DIM ** -0.5
LOG2E = math.log2(math.e)
LN2 = math.log(2.0)

DIFF_CONST_TILE = (MAX_DISTANCE + LANES - 1) // LANES + 1
DIFF_MASK_TILE = DIFF_CONST_TILE + 1
DIFF_N_TILES = DIFF_MASK_TILE + 1
SUM_ROWS = 16

VMEM_LIMIT = 48 * 1024 * 1024
VMEM_LIMIT_PROJ = 56 * 1024 * 1024


def _cparams(sem, vmem_limit=VMEM_LIMIT):
    return pltpu.CompilerParams(dimension_semantics=sem, vmem_limit_bytes=vmem_limit)


def _rmsnorm_f32(x, g):
    ms = jnp.mean(x * x, axis=-1, keepdims=True)
    return x * lax.rsqrt(ms + EPS) * g


def _rel_bucket(dist):
    n = jnp.maximum(dist, 0)
    max_exact = N_BUCKETS // 2
    nf = jnp.maximum(n, 1).astype(F32)
    large = max_exact + (jnp.log(nf / max_exact) / math.log(MAX_DISTANCE / max_exact)
                         * (N_BUCKETS - max_exact)).astype(jnp.int32)
    large = jnp.minimum(large, N_BUCKETS - 1)
    return jnp.where(n < max_exact, n, large)


def _table_lookup(bucket, tab_ref, col):
    acc = jnp.zeros(bucket.shape, F32)
    for b in range(N_BUCKETS):
        acc = jnp.where(bucket == b, tab_ref[b, col], acc)
    return acc


def _dilated_bias_kernel(tab_ref, o_ref):
    g = pl.program_id(0)
    dilation = jnp.where(g == 0, DIL_CONFIGS[0][1],
                         jnp.where(g == 1, DIL_CONFIGS[1][1], DIL_CONFIGS[2][1]))
    kj = lax.broadcasted_iota(jnp.int32, (2 * BLOCK, BLOCK), 0)
    qi = lax.broadcasted_iota(jnp.int32, (2 * BLOCK, BLOCK), 1)
    sub = qi + BLOCK - kj
    band = (sub >= 0) & (sub <= BLOCK)
    band_first = band & (kj >= BLOCK)
    bucket = _rel_bucket(sub * dilation)
    for c in range(N_BIAS_COLS):
        lanes = slice((c % 2) * BLOCK, (c % 2 + 1) * BLOCK)
        bias = _table_lookup(bucket, tab_ref, c) * LOG2E
        o_ref[0, c // 2, :, lanes] = jnp.where(band, bias, NEG)
        o_ref[1, c // 2, :, lanes] = jnp.where(band_first, bias, NEG)


def _build_dilated_bias(rel_bias):
    return pl.pallas_call(
        _dilated_bias_kernel,
        grid=(N_GROUPS,),
        in_specs=[pl.BlockSpec(memory_space=pltpu.SMEM)],
        out_specs=pl.BlockSpec((None, 2, N_HEADS_A // 2, 2 * BLOCK, 2 * BLOCK),
                               lambda g: (g, 0, 0, 0, 0)),
        out_shape=jax.ShapeDtypeStruct((N_GROUPS, 2, N_HEADS_A // 2, 2 * BLOCK, 2 * BLOCK), F32),
        compiler_params=_cparams(("arbitrary",)),
        name="dilated_bias",
    )(rel_bias)


def _diff_bias_kernel(tab_ref, o_ref):
    t = pl.program_id(0)
    kj = lax.broadcasted_iota(jnp.int32, (LANES, LANES), 0)
    qi = lax.broadcasted_iota(jnp.int32, (LANES, LANES), 1)
    dist = t * LANES + qi - kj
    masked = (dist < 0) | (t == DIFF_MASK_TILE)
    bucket = _rel_bucket(dist)
    init = jnp.where(masked, NEG, 0.0)
    for c in range(N_BIAS_COLS):
        o_ref[c] = init
    live = jnp.where(masked, -1, bucket)

    def body(b, carry):
        hit = live == b
        for c in range(N_BIAS_COLS):
            o_ref[c] = jnp.where(hit, tab_ref[b, c] * LOG2E, o_ref[c])
        return carry

    lax.fori_loop(jnp.min(bucket), jnp.max(bucket) + 1, body, 0)


def _build_diff_bias(rel_bias):
    return pl.pallas_call(
        _diff_bias_kernel,
        grid=(DIFF_N_TILES,),
        in_specs=[pl.BlockSpec(memory_space=pltpu.SMEM)],
        out_specs=pl.BlockSpec((None, N_BIAS_COLS, LANES, LANES), lambda t: (t, 0, 0, 0)),
        out_shape=jax.ShapeDtypeStruct((DIFF_N_TILES, N_BIAS_COLS, LANES, LANES), F32),
        compiler_params=_cparams(("arbitrary",)),
        name="diff_bias",
    )(rel_bias)


def _qkv_dilated_kernel(*refs, dilation, n_slabs, chunk, whole, pre_stride):
    n_x = n_slabs or 1
    x_refs = refs[:n_x]
    g_ref, w_ref, o_ref, xn_sc = refs[n_x:n_x + 4]
    tm, d_model = xn_sc.shape
    rows = tm // dilation
    if pre_stride:
        pre_sc = refs[n_x + 4]
        pre_rows = tm // pre_stride
        outer = dilation // pre_stride

    def pre_regroup():
        for s, x_ref in enumerate(x_refs):
            for b in range(pre_stride):
                pre_sc[s, b * pre_rows:(b + 1) * pre_rows, :] = x_ref[pl.ds(b, pre_rows, stride=pre_stride), :]

    def normalise(r, a, b):
        if n_slabs is None:
            xs = [x_refs[0][a:b, :]]
        elif pre_stride:
            start = (r % pre_stride) * pre_rows + r // pre_stride + (a - r * rows) * outer
            xs = [pre_sc[s, pl.ds(start, b - a, stride=outer), :] for s in range(n_slabs)]
        else:
            xs = [x_ref[pl.ds(r + (a - r * rows) * dilation, b - a, stride=dilation), :]
                  for x_ref in x_refs]
        slab_w = d_model // len(xs)
        sq = xs[0] * xs[0]
        for x in xs[1:]:
            sq = sq + x * x
        scale = lax.rsqrt(jnp.sum(sq, axis=-1, keepdims=True) * (1.0 / d_model) + EPS)
        for s, x in enumerate(xs):
            cols = slice(s * slab_w, (s + 1) * slab_w)
            xn_sc[a:b, cols] = (x * scale * g_ref[:, cols]).astype(BF16)

    def store(c, y, lo, r, a, b, col0):
        if c < 2:
            val = y[a - lo:b - lo]
            if c == 0:
                val = val * (QK_SCALE * LOG2E)
            o_ref[a - r * rows:b - r * rows, col0:col0 + d_model] = val.astype(o_ref.dtype)
            return
        for blk in range((b - a) // BLOCK):
            y0 = a - lo + blk * BLOCK
            s0 = a - r * rows + blk * BLOCK
            for hp in range(d_model // LANES):
                tile = y[y0:y0 + BLOCK, hp * LANES:(hp + 1) * LANES]
                o_ref[s0:s0 + BLOCK, col0 + hp * LANES:col0 + (hp + 1) * LANES] = (
                    tile.T.astype(o_ref.dtype))

    def run(comps, with_norm):
        if with_norm and pre_stride:
            pre_regroup()
        for ch in range(tm // chunk):
            lo, hi = ch * chunk, (ch + 1) * chunk
            pieces = [(r, max(lo, r * rows), min(hi, (r + 1) * rows)) for r in range(dilation)
                      if max(lo, r * rows) < min(hi, (r + 1) * rows)]
            if with_norm:
                for r, a, b in pieces:
                    normalise(r, a, b)
            xn = xn_sc[lo:hi, :]
            for c in comps:
                w = w_ref[:, c * d_model:(c + 1) * d_model] if whole else w_ref[...]
                y = jnp.dot(xn, w, preferred_element_type=F32)
                for r, a, b in pieces:
                    store(c, y, lo, r, a, b, ((c * dilation if whole else 0) + r) * d_model)

    if whole:
        run((0, 1, 2), True)
    else:
        j = pl.program_id(1)
        for c in range(3):
            pl.when(j == c)(functools.partial(run, (c,), c == 0))


def _qkv_dilated(x, g, w, group, dilation, *, tm):
    n, d = x.shape
    assert w.shape[0] == d and w.shape[1] % (3 * d) == 0
    rows = tm // dilation
    assert tm % dilation == 0 and rows % BLOCK == 0
    if dilation == 1:
        n_slabs = None
        x_in = [x]
        x_specs = [pl.BlockSpec((tm, d), lambda i, j: (i, 0))]
    else:
        n_slabs = d // LANES
        x_in = [x] * n_slabs
        x_specs = [pl.BlockSpec((tm, LANES), functools.partial(lambda i, j, s: (i, s), s=s))
                   for s in range(n_slabs)]
    pre_stride = 4 if dilation % 8 == 0 else None
    scratch = [pltpu.VMEM((tm, d), BF16)]
    if pre_stride:
        scratch.append(pltpu.VMEM((n_slabs, tm, LANES), F32))
    out_shape = jax.ShapeDtypeStruct((n // dilation, 3 * dilation * d), BF16)
    whole_bytes = 2 * (tm * 3 * d * 2) + 2 * (d * 3 * d * 2) + 2 * (tm * d * 4) + 4 * QKV_CHUNK * d * 4
    assert tm % QKV_CHUNK == 0 and (QKV_CHUNK % rows == 0 or rows % QKV_CHUNK == 0)
    if whole_bytes <= VMEM_LIMIT:
        return pl.pallas_call(
            functools.partial(_qkv_dilated_kernel, dilation=dilation, n_slabs=n_slabs,
                              chunk=QKV_CHUNK, whole=True, pre_stride=pre_stride),
            grid=(n // tm, 1),
            in_specs=x_specs + [pl.BlockSpec((1, d), lambda i, j: (0, 0)),
                                pl.BlockSpec((d, 3 * d), lambda i, j: (0, group))],
            out_specs=pl.BlockSpec((rows, 3 * dilation * d), lambda i, j: (i, 0)),
            out_shape=out_shape,
            scratch_shapes=scratch,
            compiler_params=_cparams(("parallel", "arbitrary")),
            name=f"qkv_d{dilation}",
        )(*x_in, g.reshape(1, d), w)
    return pl.pallas_call(
        functools.partial(_qkv_dilated_kernel, dilation=dilation, n_slabs=n_slabs,
                          chunk=QKV_CHUNK, whole=False, pre_stride=pre_stride),
        grid=(n // tm, 3),
        in_specs=x_specs + [pl.BlockSpec((1, d), lambda i, j: (0, 0)),
                            pl.BlockSpec((d, d), lambda i, j: (0, 3 * group + j))],
        out_specs=pl.BlockSpec((rows, dilation * d), lambda i, j: (i, j)),
        out_shape=out_shape,
        scratch_shapes=scratch,
        compiler_params=_cparams(("parallel", "arbitrary")),
        name=f"qkv_d{dilation}",
    )(*x_in, g.reshape(1, d), w)


def _qkv_diff_kernel(x_ref, g_ref, w_ref, qk_ref, vt_ref, *, bk):
    tm, d = x_ref.shape
    for c in range(tm // bk):
        rows = slice(c * bk, (c + 1) * bk)
        xn = _rmsnorm_f32(x_ref[rows, :], g_ref[...]).astype(BF16)
        q = jnp.dot(xn, w_ref[:, :d], preferred_element_type=F32) * (QK_SCALE * LOG2E)
        qk_ref[rows, :d] = q.astype(qk_ref.dtype)
        qk_ref[rows, d:] = jnp.dot(xn, w_ref[:, d:2 * d], preferred_element_type=F32).astype(qk_ref.dtype)
        v = jnp.dot(xn, w_ref[:, 2 * d:], preferred_element_type=F32)
        for hp in range(d // LANES):
            cols = slice(hp * LANES, (hp + 1) * LANES)
            vt_ref[c, cols, :] = v[:, cols].T.astype(vt_ref.dtype)


def _qkv_diff(x, g, w_qkv, *, bk, tm=1024):
    n, d = x.shape
    return pl.pallas_call(
        functools.partial(_qkv_diff_kernel, bk=bk),
        grid=(n // tm,),
        in_specs=[pl.BlockSpec((tm, d), lambda i: (i, 0)),
                  pl.BlockSpec((1, d), lambda i: (0, 0)),
                  pl.BlockSpec((d, 3 * d), lambda i: (0, 0))],
        out_specs=[pl.BlockSpec((tm, 2 * d), lambda i: (i, 0)),
                   pl.BlockSpec((tm // bk, d, bk), lambda i: (i, 0, 0))],
        out_shape=[jax.ShapeDtypeStruct((n, 2 * d), BF16),
                   jax.ShapeDtypeStruct((n // bk, d, bk), BF16)],
        compiler_params=_cparams(("parallel",)),
        name="qkv_diff",
    )(x, g.reshape(1, d), w_qkv)


def _dilated_kernel(q_ref, kp_ref, kc_ref, vtp_ref, vtc_ref, bias_ref, o_ref, lse_ref,
                    s_sc, m_sc, pv_sc):
    first_step = (pl.program_id(2) == 0).astype(jnp.int32)
    lane = lax.broadcasted_iota(jnp.int32, (BLOCK, LANES), 1)
    row = lax.broadcasted_iota(jnp.int32, (BLOCK, LANES), 0)
    head_row = lax.broadcasted_iota(jnp.int32, (N_HEADS_A, BLOCK), 0)
    lo_row = row < HEAD_DIM
    mask_lo = jnp.where(lane < HEAD_DIM, 1.0, 0.0).astype(BF16)
    mask_hi = jnp.where(lane < HEAD_DIM, 0.0, 1.0).astype(BF16)
    nt = (((1,), (1,)), ((), ()))
    ones = jnp.ones((SUM_ROWS, 2 * BLOCK), BF16)
    n_pairs = N_HEADS_A // 2
    n_qb = q_ref.shape[0] // BLOCK
    items = [(qb, hp) for qb in range(n_qb) for hp in range(n_pairs)]

    def rows(qb):
        return slice(qb * BLOCK, (qb + 1) * BLOCK)

    def prev_cur(prev_ref, cur_ref, qb, sl):
        prev = prev_ref[:, sl] if qb == 0 else cur_ref[rows(qb - 1), sl]
        return prev, cur_ref[rows(qb), sl]

    def scores(it):
        qb, hp = items[it]
        sl = slice(hp * LANES, (hp + 1) * LANES)
        q = q_ref[rows(qb), sl]
        q2 = jnp.concatenate([q * mask_lo, q * mask_hi], axis=0)
        k = jnp.concatenate(prev_cur(kp_ref, kc_ref, qb, sl), axis=0)
        s = lax.dot_general(k, q2, nt, preferred_element_type=F32)
        s = s + bias_ref[first_step if qb == 0 else 0, hp]
        s_sc[it % DIL_RING] = s
        m_sc[it] = jnp.max(s, axis=0, keepdims=True)

    def value_product(it):
        qb, hp = items[it]
        sl = slice(hp * LANES, (hp + 1) * LANES)
        p = jnp.exp2(s_sc[it % DIL_RING] - m_sc[it]).astype(BF16)
        vt = jnp.concatenate(prev_cur(vtp_ref, vtc_ref, qb, sl), axis=1)
        pv_sc[it] = jnp.dot(jnp.concatenate([vt, ones], axis=0), p,
                            preferred_element_type=F32)

    for it in range(len(items) + DIL_RING - 1):
        if it < len(items):
            scores(it)
        if it >= DIL_RING - 1:
            value_product(it - (DIL_RING - 1))
    for qb in range(n_qb):
        lse_t = jnp.zeros((N_HEADS_A, BLOCK), F32)
        for hp in range(n_pairs):
            it = qb * n_pairs + hp
            l = pv_sc[it, LANES:LANES + 1]
            acc = pv_sc[it, :LANES] / l
            o_t = jnp.where(lo_row, acc[:, :BLOCK], acc[:, BLOCK:])
            o_ref[rows(qb), hp * LANES:(hp + 1) * LANES] = o_t.T.astype(o_ref.dtype)
            lse = (m_sc[it] + jnp.log2(l)) * LN2
            lse_t = jnp.where(head_row == 2 * hp, lse[:, :BLOCK], lse_t)
            lse_t = jnp.where(head_row == 2 * hp + 1, lse[:, BLOCK:], lse_t)
        lse_full = jnp.concatenate([lse_t, jnp.zeros((BLOCK - N_HEADS_A, BLOCK), F32)], axis=0)
        lse_ref[rows(qb), :] = lse_full.T


def _dilated_attention(qkv, bias, group, dilation, batch, seq):
    assert seq % (dilation * BLOCK) == 0
    sub_len = seq // dilation
    n_qb = min(DIL_QB, sub_len // BLOCK)
    assert sub_len % (BLOCK * n_qb) == 0
    steps = sub_len // (BLOCK * n_qb)
    qkv_v = qkv.reshape(batch, sub_len, 3 * dilation * D_MODEL)

    def col(c):
        return lambda b, r, n: (b, n, c * dilation + r)

    def col_prev(c):
        return lambda b, r, n: (b, jnp.maximum(n * n_qb - 1, 0), c * dilation + r)

    blk = (None, n_qb * BLOCK, D_MODEL)
    blk_prev = (None, BLOCK, D_MODEL)
    n_items = n_qb * N_HEADS_A // 2
    o, lse = pl.pallas_call(
        _dilated_kernel,
        grid=(batch, dilation, steps),
        in_specs=[pl.BlockSpec(blk, col(0)),
                  pl.BlockSpec(blk_prev, col_prev(1)),
                  pl.BlockSpec(blk, col(1)),
                  pl.BlockSpec(blk_prev, col_prev(2)),
                  pl.BlockSpec(blk, col(2)),
                  pl.BlockSpec((None, 2, N_HEADS_A // 2, 2 * BLOCK, 2 * BLOCK),
                               lambda b, r, n: (group, 0, 0, 0, 0))],
        out_specs=[pl.BlockSpec(blk, lambda b, r, n: (b, n, r)),
                   pl.BlockSpec((None, n_qb * BLOCK, LANES), lambda b, r, n: (b, n, r))],
        out_shape=[jax.ShapeDtypeStruct((batch, sub_len, dilation * D_MODEL), BF16),
                   jax.ShapeDtypeStruct((batch, sub_len, dilation * LANES), F32)],
        scratch_shapes=[pltpu.VMEM((DIL_RING, 2 * BLOCK, 2 * BLOCK), F32),
                        pltpu.VMEM((n_items, 1, 2 * BLOCK), F32),
                        pltpu.VMEM((n_items, LANES + SUM_ROWS, 2 * BLOCK), F32)],
        compiler_params=_cparams(("parallel", "parallel", "arbitrary")),
        name=f"dilated_attn_d{dilation}",
    )(qkv_v, qkv_v, qkv_v, qkv_v, qkv_v, bias)
    return (o.reshape(batch * sub_len, dilation * D_MODEL),
            lse.reshape(batch * sub_len, dilation * LANES))


def _combine_proj_kernel(h_ref, o0_ref, o1_ref, o2_ref, l0_ref, l1_ref, l2_ref,
                         e_ref, w_ref, out_ref, lse_sc, o_sc):
    o_refs = [o0_ref, o1_ref, o2_ref]
    l_refs = [l0_ref, l1_ref, l2_ref]
    tm, d = h_ref.shape
    n_slabs = d // LANES
    for g, (_, dilation) in enumerate(DIL_CONFIGS):
        rows = tm // dilation
        for r in range(dilation):
            dst = pl.ds(r, rows, stride=dilation) if dilation > 1 else slice(None)
            lse_sc[g, dst, :] = l_refs[g][:, r * LANES:(r + 1) * LANES]
            for c in range(n_slabs):
                o_sc[g, c, dst, :] = o_refs[g][:, r * d + c * LANES:r * d + (c + 1) * LANES].astype(F32)
    lses = [lse_sc[g] for g in range(N_GROUPS)]
    mx = jnp.maximum(jnp.maximum(lses[0], lses[1]), lses[2])
    ws = [jnp.exp(l - mx) for l in lses]
    tot = ws[0] + ws[1] + ws[2]
    head_lane = lax.broadcasted_iota(jnp.int32, (tm, LANES), 1) < N_HEADS_A
    packed = None
    for g in range(N_GROUPS):
        a = jnp.where(head_lane, ws[g] / tot, 0.0)
        a_hi = a.astype(BF16).astype(F32)
        for part, piece in enumerate((a_hi, a - a_hi)):
            shift = (part * N_GROUPS + g) * N_HEADS_A
            moved = piece if shift == 0 else pltpu.roll(piece, shift, axis=1)
            packed = moved if packed is None else packed + moved
    ae_all = jnp.dot(packed.astype(BF16), e_ref[...], preferred_element_type=F32)
    aes = [ae_all[:, g * d:(g + 1) * d] for g in range(N_GROUPS)]
    slabs = []
    for c in range(n_slabs):
        cols = slice(c * LANES, (c + 1) * LANES)
        slabs.append(aes[0][:, cols] * o_sc[0, c] + aes[1][:, cols] * o_sc[1, c]
                     + aes[2][:, cols] * o_sc[2, c])
    o = jnp.concatenate(slabs, axis=1).astype(BF16)
    out_ref[...] = h_ref[...] + jnp.dot(o, w_ref[...], preferred_element_type=F32)


def _combine_proj(h, outs, lses, w, *, tm=512):
    n, d = h.shape
    rows = jnp.arange(LANES, dtype=jnp.int32)
    cols = jnp.arange(N_GROUPS * d, dtype=jnp.int32)
    row_group = (rows // N_HEADS_A) % N_GROUPS
    row_head = rows % N_HEADS_A
    row_used = rows < 2 * N_GROUPS * N_HEADS_A
    expand = (row_used[:, None] & (row_group[:, None] == (cols // d)[None, :])
              & (row_head[:, None] == ((cols % d) // HEAD_DIM)[None, :])).astype(BF16)
    row = lambda i: (i, 0)
    full = lambda i: (0, 0)
    dils = [dilation for _, dilation in DIL_CONFIGS]
    assert all(tm % (16 * dilation) == 0 for dilation in dils)
    return pl.pallas_call(
        _combine_proj_kernel,
        grid=(n // tm,),
        in_specs=[pl.BlockSpec((tm, d), row)]
                 + [pl.BlockSpec((tm // dilation, dilation * d), row) for dilation in dils]
                 + [pl.BlockSpec((tm // dilation, dilation * LANES), row) for dilation in dils]
                 + [pl.BlockSpec((LANES, N_GROUPS * d), full), pl.BlockSpec((d, d), full)],
        out_specs=pl.BlockSpec((tm, d), row),
        out_shape=jax.ShapeDtypeStruct((n, d), F32),
        scratch_shapes=[pltpu.VMEM((N_GROUPS, tm, LANES), F32),
                        pltpu.VMEM((N_GROUPS, d // LANES, tm, LANES), F32)],
        compiler_params=_cparams(("parallel",)),
        name="combine_proj",
    )(h, *outs, *lses, expand, w)


def _mlp_ple_kernel(*refs, final_norm, with_proj, n_steps, chunk):
    if with_proj:
        h_ref, o_ref, wo_ref = refs[:3]
        refs = refs[3:]
    else:
        h_ref = refs[0]
        refs = refs[1:]
    p_ref, g_ref, w1_ref, w2_ref, gp_ref, wg_ref, wp_ref, fg_ref, out_ref, xn_ref, acc_ref = refs
    f = pl.program_id(1)
    tm = h_ref.shape[0]

    def step(first, last):
        for ch in range(tm // chunk):
            rows = slice(ch * chunk, (ch + 1) * chunk)
            if first:
                if with_proj:
                    h0 = h_ref[rows, :] + jnp.dot(o_ref[rows, :], wo_ref[...],
                                                  preferred_element_type=F32)
                    out_ref[rows, :] = h0
                else:
                    h0 = h_ref[rows, :]
                xn_ref[rows, :] = _rmsnorm_f32(h0, g_ref[...]).astype(BF16)
            a = jnp.dot(xn_ref[rows, :], w1_ref[...], preferred_element_type=F32)
            a = jnp.maximum(a, 0.0)
            a = (a * a).astype(BF16)
            acc = jnp.dot(a, w2_ref[...], preferred_element_type=F32)
            if not first:
                acc = acc_ref[rows, :] + acc
            if not last:
                acc_ref[rows, :] = acc
                continue
            x = (out_ref[rows, :] if with_proj else h_ref[rows, :]) + acc
            xn = _rmsnorm_f32(x, gp_ref[...]).astype(BF16)
            gate = jax.nn.sigmoid(jnp.dot(xn, wg_ref[...], preferred_element_type=F32))
            proj = jnp.dot(p_ref[rows, :].astype(BF16), wp_ref[...], preferred_element_type=F32)
            y = x + gate * proj
            if final_norm:
                y = _rmsnorm_f32(y, fg_ref[...])
            out_ref[rows, :] = y

    pl.when(f == 0)(functools.partial(step, True, n_steps == 1))
    if n_steps > 2:
        pl.when((f > 0) & (f < n_steps - 1))(functools.partial(step, False, False))
    if n_steps > 1:
        pl.when(f == n_steps - 1)(functools.partial(step, False, True))


def _mlp_ple(h, p_all, layer, g, w1, w2, gp, wg, wp, fg, *, final_norm, proj=None, tm=1024, tf=1024):
    n, d = h.shape
    dff = w1.shape[2]
    pd = p_all.shape[-1]
    vec = lambda i, f: (0, 0)
    lvec = lambda i, f: (layer, 0, 0)
    row = lambda i, f: (i, 0)
    lead_specs = [pl.BlockSpec((tm, d), row)]
    lead_args = [h]
    if proj is not None:
        o, wo = proj
        lead_specs += [pl.BlockSpec((tm, o.shape[1]), row), pl.BlockSpec(wo.shape, vec)]
        lead_args += [o, wo]
    return pl.pallas_call(
        functools.partial(_mlp_ple_kernel, final_norm=final_norm, with_proj=proj is not None,
                          n_steps=dff // tf, chunk=MLP_CHUNK),
        grid=(n // tm, dff // tf),
        in_specs=lead_specs + [
                  pl.BlockSpec((None, tm, pd), lambda i, f: (layer, i, 0)),
                  pl.BlockSpec((1, d), vec),
                  pl.BlockSpec((None, d, tf), lambda i, f: (layer, 0, f)),
                  pl.BlockSpec((None, tf, d), lambda i, f: (layer, f, 0)),
                  pl.BlockSpec((1, d), vec),
                  pl.BlockSpec((None, d, d), lvec),
                  pl.BlockSpec((None, pd, d), lvec),
                  pl.BlockSpec((1, d), vec)],
        out_specs=pl.BlockSpec((tm, d), lambda i, f: (i, 0)),
        out_shape=jax.ShapeDtypeStruct((n, d), F32),
        scratch_shapes=[pltpu.VMEM((tm, d), BF16), pltpu.VMEM((tm, d), F32)],
        compiler_params=_cparams(("parallel", "arbitrary"),
                                 vmem_limit=VMEM_LIMIT_PROJ if proj is not None else VMEM_LIMIT),
        name="mlp_ple",
    )(*lead_args, p_all, g.reshape(1, d), w1, w2, gp.reshape(1, d), wg, wp, fg.reshape(1, d))


def _diff_attn_kernel(lq1_ref, lk1_ref, lq2_ref, lk2_ref, q_ref, qn_ref, k_ref, vt_ref, u_ref, sg_ref,
                      o_ref, m_sc, acc_sc, sa_sc, samax_sc, sb_sc, sbmax_sc, *, bq, bk, lambda_init):
    n_parts = bq // bk
    assert bq == n_parts * bk and n_parts % 2 == 0
    i_q = pl.program_id(2)
    q0 = i_q * bq
    nt = (((1,), (1,)), ((), ()))
    lane = lax.broadcasted_iota(jnp.int32, (bq, LANES), 1)

    def query_columns(ref):
        qf = ref[...].astype(F32)
        a = jnp.where(lane < HEAD_DIM, qf, 0.0)
        b = jnp.where(lane >= HEAD_DIM, qf, 0.0)
        pieces = []
        for part in range(n_parts):
            pieces += [a[part * bk:(part + 1) * bk], b[part * bk:(part + 1) * bk]]
        return jnp.concatenate(pieces, axis=0).astype(BF16)

    q2 = query_columns(q_ref)
    q2_next = query_columns(qn_ref)

    m_sc[...] = jnp.full(m_sc.shape, NEG, F32)
    acc_sc[...] = jnp.zeros(acc_sc.shape, F32)

    per_map = bk // LANES
    part_blocks = 2 * per_map
    n_blocks = n_parts * part_blocks
    first_diag = q0 // bk
    ones = jnp.ones((SUM_ROWS, bk), BF16)

    far_bias = jnp.concatenate(
        [u_ref[DIFF_CONST_TILE, (ib % part_blocks) // per_map][0:1, :] for ib in range(n_blocks)],
        axis=1)

    def scores(c, q2x, q0x, s_ref, smax_ref, first_block=0, far=False):
        k0 = pl.multiple_of(c * bk, bk)
        lanes = slice(first_block * LANES, n_blocks * LANES)
        s = lax.dot_general(k_ref[pl.ds(k0, bk), :], q2x[lanes], nt,
                            preferred_element_type=F32)
        if far:
            s_ref[:, lanes] = s
            smax_ref[:, lanes] = jnp.max(s, axis=0, keepdims=True) + far_bias[:, lanes]
            return
        rows = []
        t0 = lax.shift_right_arithmetic(q0x - k0, LOG2_LANES)
        for jb in range(bk // LANES):
            tiles = []
            for ib in range(first_block, n_blocks):
                part, within = divmod(ib, part_blocks)
                col, blk = divmod(within, per_map)
                t = t0 + (part * per_map + blk - jb)
                t = jnp.where(t < 0, DIFF_MASK_TILE, jnp.minimum(t, DIFF_CONST_TILE))
                tiles.append(u_ref[t, col])
            rows.append(jnp.concatenate(tiles, axis=1))
        s = s + jnp.concatenate(rows, axis=0)
        s_ref[:, lanes] = s
        smax_ref[:, lanes] = jnp.max(s, axis=0, keepdims=True)

    def accumulate(c, s_ref, smax_ref, first_block=0, far=False):
        lanes = slice(first_block * LANES, n_blocks * LANES)
        m_prev = m_sc[:, lanes]
        m_new = jnp.maximum(m_prev, smax_ref[:, lanes])
        m_sc[:, lanes] = m_new
        shift = m_new - far_bias[:, lanes] if far else m_new
        p = jnp.exp2(s_ref[:, lanes] - shift).astype(BF16)
        v1 = jnp.concatenate([vt_ref[c], ones], axis=0)
        acc_sc[:, lanes] = (jnp.exp2(m_prev - m_new) * acc_sc[:, lanes]
                            + jnp.dot(v1, p, preferred_element_type=F32))

    @pl.when(i_q == 0)
    def _():
        scores(0, q2, q0, sa_sc, samax_sc)

    def body(far_b, i, carry):
        scores(2 * i + 1, q2, q0, sb_sc, sbmax_sc, far=far_b)
        accumulate(2 * i, sa_sc, samax_sc)
        scores(2 * i + 2, q2, q0, sa_sc, samax_sc)
        accumulate(2 * i + 1, sb_sc, sbmax_sc, far=far_b)
        return carry

    far_chunks = jnp.maximum(q0 - (MAX_DISTANCE - 1), 0) // bk
    far_trips = jnp.minimum(far_chunks // 2, first_diag // 2)
    lax.fori_loop(0, far_trips, functools.partial(body, True), 0)
    lax.fori_loop(far_trips, first_diag // 2, functools.partial(body, False), 0)

    for dt in range(n_parts // 2):
        ja, jb = 2 * dt, 2 * dt + 1
        scores(first_diag + jb, q2, q0, sb_sc, sbmax_sc, first_block=jb * part_blocks)
        accumulate(first_diag + ja, sa_sc, samax_sc, first_block=ja * part_blocks)
        if jb + 1 < n_parts:
            scores(first_diag + jb + 1, q2, q0, sa_sc, samax_sc, first_block=(jb + 1) * part_blocks)
        else:
            scores(0, q2_next, q0 + bq, sa_sc, samax_sc)
        accumulate(first_diag + jb, sb_sc, sbmax_sc, first_block=jb * part_blocks)

    lam = (jnp.exp(jnp.sum(lq1_ref[...] * lk1_ref[...], keepdims=True))
           - jnp.exp(jnp.sum(lq2_ref[...] * lk2_ref[...], keepdims=True)) + lambda_init)
    accl = acc_sc[...]
    acc = accl[:LANES] / accl[LANES:LANES + 1]
    map0 = jnp.concatenate([acc[:, 2 * part * bk:(2 * part + 1) * bk] for part in range(n_parts)], axis=1)
    map1 = jnp.concatenate([acc[:, (2 * part + 1) * bk:(2 * part + 2) * bk] for part in range(n_parts)], axis=1)
    o = (map0 - lam * map1).T
    y = _rmsnorm_f32(o, sg_ref[...]) * (1.0 - lambda_init)
    o_ref[...] = y.astype(o_ref.dtype)


def _diff_attention(qk, vt, u, lq1, lk1, lq2, lk2, subln, lambda_init, batch, seq, *, bq, bk):
    nh = N_HEADS_B
    nkt = seq // bk
    nq = seq // bq
    vec = lambda a: a.reshape(1, -1)
    small = pl.BlockSpec((1, HEAD_DIM), lambda b, h, i: (0, 0))
    return pl.pallas_call(
        functools.partial(_diff_attn_kernel, bq=bq, bk=bk, lambda_init=lambda_init),
        grid=(batch, nh, seq // bq),
        in_specs=[small, small, small, small,
                  pl.BlockSpec((None, bq, LANES), lambda b, h, i: (b, i, h)),
                  pl.BlockSpec((None, bq, LANES), lambda b, h, i: (b, jnp.minimum(i + 1, nq - 1), h)),
                  pl.BlockSpec((None, seq, LANES), lambda b, h, i: (b, 0, nh + h)),
                  pl.BlockSpec((nkt, LANES, bk), lambda b, h, i: (b, h, 0)),
                  pl.BlockSpec((DIFF_N_TILES, 2, LANES, LANES), lambda b, h, i: (0, h, 0, 0)),
                  pl.BlockSpec((1, LANES), lambda b, h, i: (0, 0))],
        out_specs=pl.BlockSpec((None, bq, LANES), lambda b, h, i: (b, i, h)),
        out_shape=jax.ShapeDtypeStruct((batch, seq, D_MODEL), BF16),
        scratch_shapes=[pltpu.VMEM((1, 2 * bq), F32),
                        pltpu.VMEM((LANES + SUM_ROWS, 2 * bq), F32),
                        pltpu.VMEM((bk, 2 * bq), F32),
                        pltpu.VMEM((1, 2 * bq), F32),
                        pltpu.VMEM((bk, 2 * bq), F32),
                        pltpu.VMEM((1, 2 * bq), F32)],
        compiler_params=_cparams(("parallel", "parallel", "arbitrary")),
        name="diff_attn",
    )(vec(lq1), vec(lk1), vec(lq2), vec(lk2), qk, qk, qk, vt, u, vec(subln))


def kernel(x, p, rel_bias, a_w_qkv, a_w_o, b_w_qkv, b_w_o, b_lambda_q1, b_lambda_k1, b_lambda_q2, b_lambda_k2, b_subln, norm_mix, norm_mlp, w_ff1, w_ff2, norm_ple, w_ple_gate, w_ple_proj, final_norm):
    batch, seq, d = x.shape
    depth = p.shape[0]
    n = batch * seq
    h = x.reshape(n, d)
    p_all = p.reshape(depth, n, p.shape[-1])
    n_mixers = 2
    w1_all, w2_all = w_ff1.astype(BF16), w_ff2.astype(BF16)
    wg_all, wp_all = w_ple_gate.astype(BF16), w_ple_proj.astype(BF16)

    for i in range(depth):
        j = i // n_mixers
        if i % n_mixers == 0:
            w_a = a_w_qkv[j].astype(BF16)
            bias = _build_dilated_bias(rel_bias)
            outs, lses = [], []
            for g, (_, dilation) in enumerate(DIL_CONFIGS):
                qkv = _qkv_dilated(h, norm_mix[i], w_a, g, dilation, tm=max(1024, BLOCK * dilation))
                o_g, lse_g = _dilated_attention(qkv, bias, g, dilation, batch, seq)
                outs.append(o_g)
                lses.append(lse_g)
            h = _combine_proj(h, outs, lses, a_w_o[j].astype(BF16))
            proj = None
        else:
            lambda_init = 0.8 - 0.6 * math.exp(-0.3 * i)
            bq, bk = 2048, 512
            w_b = b_w_qkv[j].astype(BF16)
            qk, vt = _qkv_diff(h, norm_mix[i], w_b, bk=bk)
            u = _build_diff_bias(rel_bias)
            o = _diff_attention(qk.reshape(batch, seq, 2 * d), vt, u,
                                b_lambda_q1[j], b_lambda_k1[j], b_lambda_q2[j], b_lambda_k2[j],
                                b_subln[j], lambda_init, batch, seq, bq=bq, bk=bk)
            proj = (o.reshape(n, d), b_w_o[j].astype(BF16))
        h = _mlp_ple(h, p_all, i, norm_mlp[i], w1_all, w2_all, norm_ple[i], wg_all, wp_all,
                     final_norm, final_norm=(i == depth - 1), proj=proj)
    return h.reshape(batch, seq, d)
```
